```python
import math
import jax, jax.numpy as jnp
from jax import lax
import numpy as np

D_MODEL = 2048
BATCH = 4
SEQ = 4096
DEPTH = 2

MIX_WIDTH = D_MODEL
S5_WIDTH = MIX_WIDTH // 4
S5_GROUP = 16
S5_GROUPS = S5_WIDTH // S5_GROUP
S5_STATE = 64
DT_MIN = 1e-3
DT_MAX = 1e-1
SB_HEAD_DIM = 128
SB_HEADS = (MIX_WIDTH - S5_WIDTH) // SB_HEAD_DIM
SB_WIDTH = SB_HEADS * SB_HEAD_DIM
Q_BLOCK = 128
AB_IN_WIDTH = S5_WIDTH + 3 * SB_WIDTH
POOL_WINDOWS = (2, 4, 8, 16)
POOL_GROUPS = len(POOL_WINDOWS)
POOL_GROUP_WIDTH = MIX_WIDTH // POOL_GROUPS
N_EXPERTS = 16
N_EXPERT_GROUPS = 4
EXPERTS_PER_GROUP = N_EXPERTS // N_EXPERT_GROUPS
TOP_K = 2
D_FF_EXPERT = D_MODEL // 2
ALPHA = (2.0 * DEPTH) ** 0.25
BETA = (8.0 * DEPTH) ** -0.25
LN_EPS = 1e-5
N_EVEN = (DEPTH + 1) // 2
N_ODD = DEPTH // 2

kernel_name = "hybrid_s5_stickbreak_pool_grouped_moe_deepnorm"


def layer_norm(x, g, b):
    xf = x.astype(jnp.float32)
    mu = jnp.mean(xf, axis=-1, keepdims=True)
    var = jnp.mean(jnp.square(xf - mu), axis=-1, keepdims=True)
    y = (xf - mu) * lax.rsqrt(var + LN_EPS) * g.astype(jnp.float32) + b.astype(jnp.float32)
    return y.astype(x.dtype)


def s5_mixer(u, lam_re, lam_im, log_dt, b_re, b_im, c_re, c_im, d, w_glu, b_glu):
    Bsz, S, _ = u.shape
    uf = u.astype(jnp.float32)
    ug = uf.reshape(Bsz, S, S5_GROUPS, S5_GROUP)
    lr = lam_re.astype(jnp.float32)
    li = lam_im.astype(jnp.float32)
    dt = jnp.exp(log_dt.astype(jnp.float32))[:, None]
    mag = jnp.exp(lr * dt)
    lbr = mag * jnp.cos(li * dt)
    lbi = mag * jnp.sin(li * dt)
    den = lr * lr + li * li
    coef_re = ((lbr - 1.0) * lr + lbi * li) / den
    coef_im = (lbi * lr - (lbr - 1.0) * li) / den
    br = b_re.astype(jnp.float32)
    bi = b_im.astype(jnp.float32)
    bbar_re = coef_re[..., None] * br - coef_im[..., None] * bi
    bbar_im = coef_re[..., None] * bi + coef_im[..., None] * br
    bu_re = jnp.einsum('bsgh,gnh->bsgn', ug, bbar_re)
    bu_im = jnp.einsum('bsgh,gnh->bsgn', ug, bbar_im)
    a_re = jnp.broadcast_to(lbr, bu_re.shape)
    a_im = jnp.broadcast_to(lbi, bu_im.shape)

    def combine(e1, e2):
        a1r, a1i, b1r, b1i = e1
        a2r, a2i, b2r, b2i = e2
        ar = a2r * a1r - a2i * a1i
        ai = a2r * a1i + a2i * a1r
        b_r = a2r * b1r - a2i * b1i + b2r
        b_i = a2r * b1i + a2i * b1r + b2i
        return (ar, ai, b_r, b_i)

    _, _, s_re, s_im = lax.associative_scan(combine, (a_re, a_im, bu_re, bu_im), axis=1)
    y = (jnp.einsum('bsgn,ghn->bsgh', s_re, c_re.astype(jnp.float32))
         - jnp.einsum('bsgn,ghn->bsgh', s_im, c_im.astype(jnp.float32)))
    y = y.reshape(Bsz, S, S5_WIDTH) + d.astype(jnp.float32) * uf
    g = jax.nn.gelu(y)
    out = g * jax.nn.sigmoid(g @ w_glu.astype(jnp.float32) + b_glu.astype(jnp.float32))
    return out.astype(u.dtype)


def stick_breaking_attention(q, k, v):
    Bsz, S, H, dh = q.shape
    nb = S // Q_BLOCK
    scale = 1.0 / math.sqrt(dh)
    qh = q.astype(jnp.float32).transpose(0, 2, 1, 3)
    kh = k.astype(jnp.float32).transpose(0, 2, 1, 3)
    vh = v.astype(jnp.float32).transpose(0, 2, 1, 3)
    qb = qh.reshape(Bsz, H, nb, Q_BLOCK, dh).transpose(2, 0, 1, 3, 4)
    k_pos = jnp.arange(S)

    def block(args):
        q_blk, blk = args
        q_pos = blk * Q_BLOCK + jnp.arange(Q_BLOCK)
        causal = k_pos[None, :] < q_pos[:, None]
        z = jnp.einsum('bhqd,bhkd->bhqk', q_blk, kh) * scale
        log_beta = jax.nn.log_sigmoid(z)
        log_keep = jnp.where(causal, jax.nn.log_sigmoid(-z), 0.0)
        tail = jnp.flip(jnp.cumsum(jnp.flip(log_keep, -1), axis=-1), -1) - log_keep
        w = jnp.where(causal, jnp.exp(log_beta + tail), 0.0)
        return jnp.einsum('bhqk,bhkd->bhqd', w, vh)

    out = lax.map(block, (qb, jnp.arange(nb)))
    out = out.transpose(1, 0, 3, 2, 4).reshape(Bsz, S, H * dh)
    return out.astype(q.dtype)


def mixer_ab(x, w_in, lam_re, lam_im, log_dt, b_re, b_im, c_re, c_im, d, w_glu, b_glu, w_out):
    Bsz, S, _ = x.shape
    h = x @ w_in
    u = h[..., :S5_WIDTH]
    q = h[..., S5_WIDTH:S5_WIDTH + SB_WIDTH].reshape(Bsz, S, SB_HEADS, SB_HEAD_DIM)
    k = h[..., S5_WIDTH + SB_WIDTH:S5_WIDTH + 2 * SB_WIDTH].reshape(Bsz, S, SB_HEADS, SB_HEAD_DIM)
    v = h[..., S5_WIDTH + 2 * SB_WIDTH:].reshape(Bsz, S, SB_HEADS, SB_HEAD_DIM)
    y_a = s5_mixer(u, lam_re, lam_im, log_dt, b_re, b_im, c_re, c_im, d, w_glu, b_glu)
    y_b = stick_breaking_attention(q, k, v)
    return jnp.concatenate([y_a, y_b], axis=-1) @ w_out


def mixer_pool(x, w_in, w_group, scale, w_out):
    Bsz, S, _ = x.shape
    h = (x @ w_in).astype(jnp.float32).reshape(Bsz, S, POOL_GROUPS, POOL_GROUP_WIDTH)
    cs = jnp.cumsum(h, axis=1)
    cs0 = jnp.concatenate([jnp.zeros_like(cs[:, :1]), cs], axis=1)
    windows = jnp.array(POOL_WINDOWS, dtype=jnp.int32)
    t = jnp.arange(S, dtype=jnp.int32)[:, None]
    lo = jnp.maximum(t + 1 - windows[None, :], 0)
    g_idx = jnp.arange(POOL_GROUPS)[None, :]
    window_sum = cs0[:, 1:] - cs0[:, lo, g_idx, :]
    count = jnp.minimum(t + 1, windows[None, :]).astype(jnp.float32)
    pooled = window_sum / count[None, :, :, None] - h
    y = jnp.einsum('bsgc,gce->bsge', pooled, w_group.astype(jnp.float32))
    y = y.reshape(Bsz, S, MIX_WIDTH) * scale.astype(jnp.float32)
    return y.astype(x.dtype) @ w_out


def grouped_moe(x, router_w, router_b, w_gate, w_up, w_down):
    Bsz, S, D = x.shape
    tok = x.reshape(-1, D)
    logits = (tok @ router_w + router_b).astype(jnp.float32)
    probs = jax.nn.softmax(logits, axis=-1)
    pg = probs.reshape(-1, N_EXPERT_GROUPS, EXPERTS_PER_GROUP)
    group_score = lax.top_k(pg, TOP_K)[0].sum(-1)
    g_sel = jnp.argmax(group_score, axis=-1)
    in_group = (jnp.arange(N_EXPERTS) // EXPERTS_PER_GROUP)[None, :] == g_sel[:, None]
    masked = jnp.where(in_group, probs, -1.0)
    top_w, top_i = lax.top_k(masked, TOP_K)
    top_w = top_w / jnp.sum(top_w, axis=-1, keepdims=True)
    gates = jnp.einsum('tk,tke->te', top_w, jax.nn.one_hot(top_i, N_EXPERTS, dtype=jnp.float32))
    gates = gates.astype(tok.dtype)
    out = jnp.zeros_like(tok)
    for e in range(N_EXPERTS):
        hid = jax.nn.silu(tok @ w_gate[e]) * (tok @ w_up[e])
        out = out + gates[:, e:e + 1] * (hid @ w_down[e])
    return out.reshape(Bsz, S, D)


def setup_inputs(seed: int = 0) -> dict:
    key = jax.random.key(seed)
    ks = jax.random.split(key, 26)
    f32 = jnp.float32
    nrm = lambda k, shape, s: jax.random.normal(k, shape, f32) * s
    x = jax.random.normal(ks[0], (BATCH, SEQ, D_MODEL), f32)
    ab_w_in = nrm(ks[1], (N_EVEN, D_MODEL, AB_IN_WIDTH), D_MODEL ** -0.5)
    n_idx = jnp.arange(S5_STATE, dtype=f32)
    ab_lambda_re = -0.5 + nrm(ks[2], (N_EVEN, S5_GROUPS, S5_STATE), 0.01)
    ab_lambda_im = math.pi * n_idx + nrm(ks[3], (N_EVEN, S5_GROUPS, S5_STATE), 0.01)
    ab_log_dt = jax.random.uniform(ks[4], (N_EVEN, S5_GROUPS), f32, math.log(DT_MIN), math.log(DT_MAX))
    ab_b_re = nrm(ks[5], (N_EVEN, S5_GROUPS, S5_STATE, S5_GROUP), (2 * S5_GROUP) ** -0.5)
    ab_b_im = nrm(ks[6], (N_EVEN, S5_GROUPS, S5_STATE, S5_GROUP), (2 * S5_GROUP) ** -0.5)
    ab_c_re = nrm(ks[7], (N_EVEN, S5_GROUPS, S5_GROUP, S5_STATE), (2 * S5_STATE) ** -0.5)
    ab_c_im = nrm(ks[8], (N_EVEN, S5_GROUPS, S5_GROUP, S5_STATE), (2 * S5_STATE) ** -0.5)
    ab_d = 1.0 + nrm(ks[9], (N_EVEN, S5_WIDTH), 0.1)
    ab_w_glu = nrm(ks[10], (N_EVEN, S5_WIDTH, S5_WIDTH), S5_WIDTH ** -0.5)
    ab_b_glu = nrm(ks[11], (N_EVEN, S5_WIDTH), 0.01)
    ab_w_out = nrm(ks[12], (N_EVEN, MIX_WIDTH, D_MODEL), BETA * MIX_WIDTH ** -0.5)
    c_w_in = nrm(ks[13], (N_ODD, D_MODEL, MIX_WIDTH), D_MODEL ** -0.5)
    c_w_group = nrm(ks[14], (N_ODD, POOL_GROUPS, POOL_GROUP_WIDTH, POOL_GROUP_WIDTH), POOL_GROUP_WIDTH ** -0.5)
    c_scale = 1.0 + nrm(ks[15], (N_ODD, MIX_WIDTH), 0.02)
    c_w_out = nrm(ks[16], (N_ODD, MIX_WIDTH, D_MODEL), BETA * MIX_WIDTH ** -0.5)
    ln_g = 1.0 + nrm(ks[17], (DEPTH, 2, D_MODEL), 0.02)
    ln_b = nrm(ks[18], (DEPTH, 2, D_MODEL), 0.02)
    router_w = nrm(ks[19], (D_MODEL, N_EXPERTS), D_MODEL ** -0.5)
    router_b = nrm(ks[20], (N_EXPERTS,), 0.01)
    moe_w_gate = nrm(ks[21], (DEPTH, N_EXPERTS, D_MODEL, D_FF_EXPERT), D_MODEL ** -0.5)
    moe_w_up = nrm(ks[22], (DEPTH, N_EXPERTS, D_MODEL, D_FF_EXPERT), D_MODEL ** -0.5)
    moe_w_down = nrm(ks[23], (DEPTH, N_EXPERTS, D_FF_EXPERT, D_MODEL), BETA * D_FF_EXPERT ** -0.5)
    return {"x": x, "ab_w_in": ab_w_in, "ab_lambda_re": ab_lambda_re, "ab_lambda_im": ab_lambda_im,
            "ab_log_dt": ab_log_dt, "ab_b_re": ab_b_re, "ab_b_im": ab_b_im, "ab_c_re": ab_c_re,
            "ab_c_im": ab_c_im, "ab_d": ab_d, "ab_w_glu": ab_w_glu, "ab_b_glu": ab_b_glu,
            "ab_w_out": ab_w_out, "c_w_in": c_w_in, "c_w_group": c_w_group, "c_scale": c_scale,
            "c_w_out": c_w_out, "ln_g": ln_g, "ln_b": ln_b, "router_w": router_w, "router_b": router_b,
            "moe_w_gate": moe_w_gate, "moe_w_up": moe_w_up, "moe_w_down": moe_w_down}


def reference(x, ab_w_in, ab_lambda_re, ab_lambda_im, ab_log_dt, ab_b_re, ab_b_im, ab_c_re, ab_c_im,
              ab_d, ab_w_glu, ab_b_glu, ab_w_out, c_w_in, c_w_group, c_scale, c_w_out, ln_g, ln_b,
              router_w, router_b, moe_w_gate, moe_w_up, moe_w_down):
    h = x
    for i in range(DEPTH):
        j = i // 2
        if i % 2 == 0:
            mix = mixer_ab(h, ab_w_in[j], ab_lambda_re[j], ab_lambda_im[j], ab_log_dt[j], ab_b_re[j],
                           ab_b_im[j], ab_c_re[j], ab_c_im[j], ab_d[j], ab_w_glu[j], ab_b_glu[j], ab_w_out[j])
        else:
            mix = mixer_pool(h, c_w_in[j], c_w_group[j], c_scale[j], c_w_out[j])
        h = layer_norm(ALPHA * h + mix, ln_g[i, 0], ln_b[i, 0])
        ffn = grouped_moe(h, router_w, router_b, moe_w_gate[i], moe_w_up[i], moe_w_down[i])
        h = layer_norm(ALPHA * h + ffn, ln_g[i, 1], ln_b[i, 1])
    return h
```

```python
import functools
import math

import jax
import jax.numpy as jnp
from jax import lax
from jax.experimental import pallas as pl
from jax.experimental.pallas import tpu as pltpu

S5_WIDTH = 512
S5_GROUP = 16
S5_GROUPS = 32
S5_STATE = 64
SB_HEAD_DIM = 128
SB_HEADS = 12
SB_WIDTH = SB_HEADS * SB_HEAD_DIM
POOL_WINDOWS = (2, 4, 8, 16)
POOL_GROUP_WIDTH = 512
N_EXPERTS = 16
N_EXPERT_GROUPS = 4
EXPERTS_PER_GROUP = 4
DEPTH = 2
ALPHA = (2.0 * DEPTH) ** 0.25
LN_EPS = 1e-5

LANES = 128
VMEM_LIMIT = 56 * 1024 * 1024

S5_CHUNK = 32
ATT_TQ = 256
ATT_BK = 256
ATT_HEADS_PER_STEP = 2
ATT_LOG_UNDERFLOW = -88.0
MOE_TM = 256
ROW_TILE = 512

BF16 = jnp.bfloat16
F32 = jnp.float32


def _params(sem, vmem=VMEM_LIMIT):
    return pltpu.CompilerParams(dimension_semantics=sem, vmem_limit_bytes=vmem)


def _mm_kernel(x_ref, w_ref, o_ref):
    o_ref[...] = jnp.dot(x_ref[...], w_ref[...], preferred_element_type=F32).astype(o_ref.dtype)


def _mm(x, w, out_dtype, tm, tn):
    m, k = x.shape
    n = w.shape[1]
    return pl.pallas_call(
        _mm_kernel,
        grid=(m // tm, n // tn),
        in_specs=[pl.BlockSpec((tm, k), lambda i, j: (i, 0)),
                  pl.BlockSpec((k, tn), lambda i, j: (0, j))],
        out_specs=pl.BlockSpec((tm, tn), lambda i, j: (i, j)),
        out_shape=jax.ShapeDtypeStruct((m, n), out_dtype),
        compiler_params=_params(("parallel", "parallel")),
        name="mm",
    )(x, w)


def _bmm_f32_kernel(x_ref, y_ref, o_ref):
    o_ref[...] = jnp.dot(x_ref[...], y_ref[...], preferred_element_type=F32,
                         precision=lax.Precision.HIGHEST)


def _bmm_f32(x, y):
    g, m, k = x.shape
    n = y.shape[2]
    return pl.pallas_call(
        _bmm_f32_kernel,
        grid=(g,),
        in_specs=[pl.BlockSpec((None, m, k), lambda i: (i, 0, 0)),
                  pl.BlockSpec((None, k, n), lambda i: (i, 0, 0))],
        out_specs=pl.BlockSpec((None, m, n), lambda i: (i, 0, 0)),
        out_shape=jax.ShapeDtypeStruct((g, m, n), F32),
        compiler_params=_params(("parallel",)),
        name="s5_taps",
    )(x, y)


def _s5_kernel(u_ref, wt_ref, ws_ref, wo_ref, a1_ref, a2_ref, o_ref, *, chunks, nsteps):
    u = u_ref[...]
    y_intra = jnp.dot(u, wt_ref[...], preferred_element_type=F32)
    s = jnp.dot(u, ws_ref[...], preferred_element_type=F32)
    m = s.shape[0]
    c_idx = lax.broadcasted_iota(jnp.int32, s.shape, 0) % chunks
    half = s.shape[1] // 2
    for k in range(nsteps):
        d = 1 << k
        sh = jnp.where(c_idx >= d, pltpu.roll(s, d, 0), 0.0)
        s = s + a1_ref[k:k + 1, :] * sh + a2_ref[k:k + 1, :] * pltpu.roll(sh, half, 1)
    s_prev = jnp.where(c_idx >= 1, pltpu.roll(s, 1, 0), 0.0)
    y_inter = jnp.dot(s_prev.astype(BF16), wo_ref[...], preferred_element_type=F32)
    o_ref[...] = y_intra + y_inter


def _s5_operands(lam_re, lam_im, log_dt, b_re, b_im, c_re, c_im, chunk, nsteps):
    g, n = lam_re.shape
    h = S5_GROUP
    dt = jnp.exp(log_dt)[:, None]
    lr, li = lam_re, lam_im
    mag = jnp.exp(lr * dt)
    lbr = mag * jnp.cos(li * dt)
    lbi = mag * jnp.sin(li * dt)
    den = lr * lr + li * li
    coef_re = ((lbr - 1.0) * lr + lbi * li) / den
    coef_im = (lbi * lr - (lbr - 1.0) * li) / den
    bbar_re = coef_re[..., None] * b_re - coef_im[..., None] * b_im
    bbar_im = coef_re[..., None] * b_im + coef_im[..., None] * b_re

    def power(mult):
        mult = mult.astype(F32)[None, :, None]
        pm = jnp.exp(lr[:, None, :] * dt[:, None, :] * mult)
        ang = li[:, None, :] * dt[:, None, :] * mult
        return pm * jnp.cos(ang), pm * jnp.sin(ang)

    pr, pi = power(jnp.arange(chunk + 1))
    cre = jnp.swapaxes(c_re, 1, 2)
    cim = jnp.swapaxes(c_im, 1, 2)
    cb_re = cre[:, :, :, None] * bbar_re[:, :, None, :] - cim[:, :, :, None] * bbar_im[:, :, None, :]
    cb_im = cre[:, :, :, None] * bbar_im[:, :, None, :] + cim[:, :, :, None] * bbar_re[:, :, None, :]
    y_op = jnp.concatenate([cb_re, cb_im], axis=1).reshape(g, 2 * n, h * h)
    x_op = jnp.concatenate([pr[:, :chunk], -pi[:, :chunk]], axis=-1)
    taps = _bmm_f32(x_op, y_op).reshape(g, chunk, h, h)
    jj = jnp.arange(chunk)[:, None]
    tt = jnp.arange(chunk)[None, :]
    lag = jnp.clip(tt - jj, 0, chunk - 1)
    wt = jnp.where((tt >= jj)[None, :, :, None, None], taps[:, lag], 0.0)
    wt = wt.transpose(0, 1, 4, 2, 3).reshape(g, chunk * h, chunk * h)
    prr = pr[:, chunk - 1::-1][:, :chunk]
    pir = pi[:, chunk - 1::-1][:, :chunk]
    ws_re = prr[:, :, None, :] * jnp.swapaxes(bbar_re, 1, 2)[:, None] - pir[:, :, None, :] * jnp.swapaxes(bbar_im, 1, 2)[:, None]
    ws_im = prr[:, :, None, :] * jnp.swapaxes(bbar_im, 1, 2)[:, None] + pir[:, :, None, :] * jnp.swapaxes(bbar_re, 1, 2)[:, None]
    ws = jnp.concatenate([ws_re, ws_im], axis=-1).reshape(g, chunk * h, 2 * n)
    p1r = pr[:, 1:chunk + 1]
    p1i = pi[:, 1:chunk + 1]
    wo_re = c_re[:, None] * p1r[:, :, None, :] - c_im[:, None] * p1i[:, :, None, :]
    wo_im = -(c_re[:, None] * p1i[:, :, None, :] + c_im[:, None] * p1r[:, :, None, :])
    wo = jnp.concatenate([wo_re, wo_im], axis=-1)
    wo = wo.transpose(0, 3, 1, 2).reshape(g, 2 * n, chunk * h)
    sr, si = power(chunk * (2 ** jnp.arange(max(nsteps, 1))))
    a1 = jnp.concatenate([sr, sr], axis=-1)
    a2 = jnp.concatenate([-si, si], axis=-1)
    return wt.astype(BF16), ws.astype(BF16), wo.astype(BF16), a1, a2


def _s5_scan(u, ops, bsz, seq):
    wt, ws, wo, a1, a2 = ops
    g, h, chunk = S5_GROUPS, S5_GROUP, S5_CHUNK
    chunks = seq // chunk
    m = bsz * chunks
    nsteps = a1.shape[1] if chunks > 1 else 0
    up = u.reshape(bsz, chunks, chunk, g, h).transpose(3, 0, 1, 2, 4).reshape(g, m, chunk * h).astype(BF16)
    n2 = ws.shape[2]
    ns = a1.shape[1]
    yp = pl.pallas_call(
        functools.partial(_s5_kernel, chunks=chunks, nsteps=nsteps),
        grid=(g,),
        in_specs=[pl.BlockSpec((None, m, chunk * h), lambda i: (i, 0, 0)),
                  pl.BlockSpec((None, chunk * h, chunk * h), lambda i: (i, 0, 0)),
                  pl.BlockSpec((None, chunk * h, n2), lambda i: (i, 0, 0)),
                  pl.BlockSpec((None, n2, chunk * h), lambda i: (i, 0, 0)),
                  pl.BlockSpec((None, ns, n2), lambda i: (i, 0, 0)),
                  pl.BlockSpec((None, ns, n2), lambda i: (i, 0, 0))],
        out_specs=pl.BlockSpec((None, m, chunk * h), lambda i: (i, 0, 0)),
        out_shape=jax.ShapeDtypeStruct((g, m, chunk * h), F32),
        compiler_params=_params(("parallel",)),
        name="s5_scan",
    )(up, wt, ws, wo, a1, a2)
    return yp.reshape(g, bsz, chunks, chunk, h).transpose(1, 2, 3, 0, 4).reshape(bsz * seq, g * h)


def _s5_glu_kernel(y_ref, u_ref, d_ref, w_ref, b_ref, o_ref):
    y = y_ref[...] + d_ref[...] * u_ref[...]
    gl = jax.nn.gelu(y)
    gate = jnp.dot(gl.astype(BF16), w_ref[...], preferred_element_type=F32) + b_ref[...]
    o_ref[...] = (gl * jax.nn.sigmoid(gate)).astype(o_ref.dtype)


def _s5_glu(y, u, d, w_glu, b_glu, tm):
    t, c = y.shape
    return pl.pallas_call(
        _s5_glu_kernel,
        grid=(t // tm,),
        in_specs=[pl.BlockSpec((tm, c), lambda i: (i, 0)),
                  pl.BlockSpec((tm, c), lambda i: (i, 0)),
                  pl.BlockSpec((1, c), lambda i: (0, 0)),
                  pl.BlockSpec((c, c), lambda i: (0, 0)),
                  pl.BlockSpec((1, c), lambda i: (0, 0))],
        out_specs=pl.BlockSpec((tm, c), lambda i: (i, 0)),
        out_shape=jax.ShapeDtypeStruct((t, c), BF16),
        compiler_params=_params(("parallel",)),
        name="s5_glu",
    )(y, u, d.reshape(1, c), w_glu.astype(BF16), b_glu.reshape(1, c))


def _sb_attn_kernel(q_ref, k_ref, v_ref, o_ref, acc_ref, r_ref, tri_ref, *, seq, tq, bk, scale):
    dh = SB_HEAD_DIM
    row = lax.broadcasted_iota(jnp.int32, (bk, bk), 0)
    col = lax.broadcasted_iota(jnp.int32, (bk, bk), 1)
    tri_ref[...] = jnp.where(row > col, 1.0, 0.0).astype(BF16)
    nq = seq // tq
    for hh in range(ATT_HEADS_PER_STEP):
        cs = slice(hh * dh, (hh + 1) * dh)

        def q_tile(i, carry, cs=cs):
            q = q_ref[pl.ds(pl.multiple_of(i * tq, tq), tq), cs]
            acc_ref[...] = jnp.zeros_like(acc_ref)
            r_ref[...] = jnp.zeros_like(r_ref)
            q_pos = i * tq + lax.broadcasted_iota(jnp.int32, (tq, bk), 0)

            def cond(c):
                j, rmax = c
                return jnp.logical_and(j >= 0, rmax > ATT_LOG_UNDERFLOW)

            def body(c):
                j, _ = c
                k0 = pl.multiple_of(j * bk, bk)
                kb = k_ref[pl.ds(k0, bk), cs]
                vb = v_ref[pl.ds(k0, bk), cs]
                z = lax.dot_general(q, kb, (((1,), (1,)), ((), ())), preferred_element_type=F32) * scale
                causal = (j * bk + lax.broadcasted_iota(jnp.int32, (tq, bk), 1)) < q_pos
                sp = jnp.maximum(z, 0.0) + jnp.log1p(jnp.exp(-jnp.abs(z)))
                log_keep = jnp.where(causal, -sp, 0.0)
                hi = log_keep.astype(BF16)
                lo = (log_keep - hi.astype(F32)).astype(BF16)
                tri = tri_ref[...]
                r_old = r_ref[...]
                tail = (jnp.dot(hi, tri, preferred_element_type=F32)
                        + jnp.dot(lo, tri, preferred_element_type=F32) + r_old)
                w = jnp.where(causal, jnp.exp(z - sp + tail), 0.0)
                acc_ref[...] += jnp.dot(w.astype(BF16), vb, preferred_element_type=F32)
                r_new = r_old + jnp.sum(log_keep, axis=1, keepdims=True)
                r_ref[...] = r_new
                return j - 1, jnp.max(r_new)

            lax.while_loop(cond, body, (i * (tq // bk) + (tq // bk - 1), jnp.float32(0.0)))
            o_ref[pl.ds(pl.multiple_of(i * tq, tq), tq), cs] = acc_ref[...].astype(o_ref.dtype)
            return carry

        lax.fori_loop(0, nq, q_tile, 0)


def _sb_attention(qkv, bsz, seq):
    tq = min(ATT_TQ, seq)
    bk = min(ATT_BK, seq)
    assert tq % bk == 0
    wcols = ATT_HEADS_PER_STEP * SB_HEAD_DIM
    nhp = SB_WIDTH // wcols
    kern = functools.partial(_sb_attn_kernel, seq=seq, tq=tq, bk=bk, scale=1.0 / math.sqrt(SB_HEAD_DIM))
    return pl.pallas_call(
        kern,
        grid=(bsz, nhp),
        in_specs=[pl.BlockSpec((None, seq, wcols), lambda b, h: (b, 0, h)),
                  pl.BlockSpec((None, seq, wcols), lambda b, h: (b, 0, nhp + h)),
                  pl.BlockSpec((None, seq, wcols), lambda b, h: (b, 0, 2 * nhp + h))],
        out_specs=pl.BlockSpec((None, seq, wcols), lambda b, h: (b, 0, h)),
        out_shape=jax.ShapeDtypeStruct((bsz, seq, SB_WIDTH), BF16),
        scratch_shapes=[pltpu.VMEM((tq, SB_HEAD_DIM), F32),
                        pltpu.VMEM((tq, 1), F32),
                        pltpu.VMEM((bk, bk), BF16)],
        compiler_params=_params(("parallel", "parallel")),
        name="sb_attention",
    )(qkv, qkv, qkv)


def _pool_in_kernel(x_ref, w_ref, o_ref, halo_ref, *, seq, tm):
    i = pl.program_id(0)
    hmax = POOL_WINDOWS[-1]
    acc = jnp.dot(x_ref[...], w_ref[...], preferred_element_type=F32)
    t0 = (i * tm) % seq

    @pl.when(t0 == 0)
    def _():
        halo_ref[...] = jnp.zeros_like(halo_ref)

    ext = jnp.concatenate([halo_ref[...], acc], axis=0)
    t_idx = t0 + lax.broadcasted_iota(jnp.int32, (tm, POOL_GROUP_WIDTH), 0)
    for gi, win in enumerate(POOL_WINDOWS):
        cs = slice(gi * POOL_GROUP_WIDTH, (gi + 1) * POOL_GROUP_WIDTH)
        e = ext[:, cs]
        span = 1
        while span < win:
            e = e + pltpu.roll(e, span, 0)
            span *= 2
        wsum = e[hmax:, :]
        cnt = jnp.minimum(t_idx + 1, win).astype(F32)
        o_ref[:, cs] = (wsum / cnt - acc[:, cs]).astype(o_ref.dtype)
    halo_ref[...] = acc[tm - hmax:, :]


def _pool_in(x, w, seq, tm):
    t, k = x.shape
    n = w.shape[1]
    return pl.pallas_call(
        functools.partial(_pool_in_kernel, seq=seq, tm=tm),
        grid=(t // tm,),
        in_specs=[pl.BlockSpec((tm, k), lambda i: (i, 0)),
                  pl.BlockSpec((k, n), lambda i: (0, 0))],
        out_specs=pl.BlockSpec((tm, n), lambda i: (i, 0)),
        out_shape=jax.ShapeDtypeStruct((t, n), BF16),
        scratch_shapes=[pltpu.VMEM((POOL_WINDOWS[-1], n), F32)],
        compiler_params=_params(("arbitrary",)),
        name="pool_in",
    )(x, w)


def _layer_norm(x, g, b):
    mu = jnp.mean(x, axis=-1, keepdims=True)
    xc = x - mu
    var = jnp.mean(xc * xc, axis=-1, keepdims=True)
    return xc * lax.rsqrt(var + LN_EPS) * g + b


def _route(h, rwh_ref, rwl_ref, rb_ref, base_ref, tm):
    h_hi = h.astype(BF16)
    h_lo = (h - h_hi.astype(F32)).astype(BF16)
    rwh = rwh_ref[...]
    logits = (jnp.dot(h_hi, rwh, preferred_element_type=F32)
              + jnp.dot(h_lo, rwh, preferred_element_type=F32)
              + jnp.dot(h_hi, rwl_ref[...], preferred_element_type=F32)) + rb_ref[...]
    lt = logits.T[:N_EXPERTS, :]
    mx = jnp.max(lt, axis=0, keepdims=True)
    ex = jnp.exp(lt - mx)
    probs = ex / jnp.sum(ex, axis=0, keepdims=True)
    p = [probs[e:e + 1, :] for e in range(N_EXPERTS)]

    def first_max(vals):
        best, idx = vals[0], jnp.zeros_like(vals[0], dtype=jnp.int32)
        for n in range(1, len(vals)):
            upd = vals[n] > best
            idx = jnp.where(upd, n, idx)
            best = jnp.where(upd, vals[n], best)
        return best, idx

    scores = []
    for gi in range(N_EXPERT_GROUPS):
        p0, p1, p2, p3 = p[4 * gi:4 * gi + 4]
        a, b = jnp.maximum(p0, p1), jnp.minimum(p0, p1)
        c, d = jnp.maximum(p2, p3), jnp.minimum(p2, p3)
        scores.append(jnp.maximum(a, c) + jnp.maximum(jnp.minimum(a, c), jnp.maximum(b, d)))
    _, g_sel = first_max(scores)
    sel = []
    for n in range(EXPERTS_PER_GROUP):
        v = p[n]
        for gi in range(1, N_EXPERT_GROUPS):
            v = jnp.where(g_sel == gi, p[4 * gi + n], v)
        sel.append(v)
    v1, i1 = first_max(sel)
    v2, i2 = first_max([jnp.where(i1 == n, -1.0, sel[n]) for n in range(EXPERTS_PER_GROUP)])
    den = v1 + v2
    e1 = g_sel * EXPERTS_PER_GROUP + i1
    e2 = g_sel * EXPERTS_PER_GROUP + i2
    e_iota = lax.broadcasted_iota(jnp.int32, (N_EXPERTS, tm), 0)
    oh1 = e_iota == e1
    oh2 = e_iota == e2
    oh = jnp.where(jnp.logical_or(oh1, oh2), 1.0, 0.0)
    r_i = lax.broadcasted_iota(jnp.int32, (tm, tm), 0)
    c_i = lax.broadcasted_iota(jnp.int32, (tm, tm), 1)
    before = jnp.where(r_i < c_i, 1.0, 0.0).astype(BF16)
    cnt = jnp.dot(oh.astype(BF16), before, preferred_element_type=F32) + base_ref[...]
    rank1 = jnp.sum(jnp.where(oh1, cnt, 0.0), axis=0, keepdims=True)
    rank2 = jnp.sum(jnp.where(oh2, cnt, 0.0), axis=0, keepdims=True)
    base_ref[...] += jnp.sum(oh, axis=1, keepdims=True)
    e_out = jnp.concatenate([e1, e2], axis=0)
    w_out = jnp.concatenate([v1 / den, v2 / den], axis=0)
    r_out = jnp.concatenate([rank1, rank2], axis=0).astype(jnp.int32)
    return e_out, w_out, r_out


def _proj_ln_kernel(*refs, n_x, grouped, tm):
    xs = refs[:n_x]
    ws = refs[n_x:2 * n_x]
    pos = 2 * n_x
    if grouped:
        wg_ref, sc_ref = refs[pos:pos + 2]
        pos += 2
    res_ref, g_ref, b_ref, rwh_ref, rwl_ref, rb_ref = refs[pos:pos + 6]
    h_ref, e_ref, gw_ref, rk_ref, cnt_ref, base_ref = refs[pos + 6:pos + 12]

    @pl.when(pl.program_id(0) == 0)
    def _():
        base_ref[...] = jnp.zeros_like(base_ref)

    if grouped:
        x = xs[0][...]
        parts = []
        for gi in range(len(POOL_WINDOWS)):
            cs = slice(gi * POOL_GROUP_WIDTH, (gi + 1) * POOL_GROUP_WIDTH)
            parts.append(jnp.dot(x[:, cs], wg_ref[gi], preferred_element_type=F32))
        y = (jnp.concatenate(parts, axis=1) * sc_ref[...]).astype(BF16)
        mix = jnp.dot(y, ws[0][...], preferred_element_type=F32)
    else:
        mix = jnp.dot(xs[0][...], ws[0][...], preferred_element_type=F32)
        for n in range(1, n_x):
            mix = mix + jnp.dot(xs[n][...], ws[n][...], preferred_element_type=F32)
    h = _layer_norm(ALPHA * res_ref[...] + mix, g_ref[...], b_ref[...])
    h_ref[...] = h
    e_out, w_out, r_out = _route(h, rwh_ref, rwl_ref, rb_ref, base_ref, tm)
    e_ref[...] = e_out
    gw_ref[...] = w_out
    rk_ref[...] = r_out
    cnt_ref[...] = jnp.broadcast_to(base_ref[...], cnt_ref.shape)


def _proj_ln_route(xs, ws, res, ln_g, ln_b, router, tm, grouped=None):
    t, d = res.shape
    rwh, rwl, rb = router
    n_x = len(xs)
    full = lambda a: pl.BlockSpec(a.shape, lambda i, nd=a.ndim: (0,) * nd)
    in_specs = [pl.BlockSpec((tm, x.shape[1]), lambda i: (i, 0)) for x in xs]
    in_specs += [full(w) for w in ws]
    args = list(xs) + list(ws)
    if grouped is not None:
        in_specs += [full(a) for a in grouped]
        args += list(grouped)
    in_specs += [pl.BlockSpec((tm, d), lambda i: (i, 0)), full(ln_g), full(ln_b), full(rwh), full(rwl), full(rb)]
    args += [res, ln_g, ln_b, rwh, rwl, rb]
    pair = lambda dt: jax.ShapeDtypeStruct((2, t), dt)
    out = pl.pallas_call(
        functools.partial(_proj_ln_kernel, n_x=n_x, grouped=grouped is not None, tm=tm),
        grid=(t // tm,),
        in_specs=in_specs,
        out_specs=[pl.BlockSpec((tm, d), lambda i: (i, 0)),
                   pl.BlockSpec((2, tm), lambda i: (0, i)),
                   pl.BlockSpec((2, tm), lambda i: (0, i)),
                   pl.BlockSpec((2, tm), lambda i: (0, i)),
                   pl.BlockSpec((N_EXPERTS, LANES), lambda i: (0, 0))],
        out_shape=[jax.ShapeDtypeStruct((t, d), F32), pair(jnp.int32), pair(F32), pair(jnp.int32),
                   jax.ShapeDtypeStruct((N_EXPERTS, LANES), F32)],
        scratch_shapes=[pltpu.VMEM((N_EXPERTS, 1), F32)],
        compiler_params=_params(("arbitrary",)),
        name="proj_ln_route",
    )(*args)
    h, e, gw, rk, cnt = out
    return h, e, gw, rk, cnt[:, 0]


def _dispatch_kernel(p1_ref, p2_ref, fill_ref, h_ref, xs_ref, zero_ref, sem, *, tm, n_fill):
    base = pl.program_id(0) * tm

    @pl.when(pl.program_id(0) == 0)
    def _():
        zero_ref[...] = jnp.zeros_like(zero_ref)
        n_zero = zero_ref.shape[0]
        fill = lambda k: pltpu.make_async_copy(
            zero_ref, xs_ref.at[pl.ds(pl.multiple_of(jnp.maximum(fill_ref[k], 0), n_zero), n_zero)], sem.at[2])
        for k in range(n_fill):
            pl.when(fill_ref[k] >= 0)(lambda k=k: fill(k).start())
        for k in range(n_fill):
            pl.when(fill_ref[k] >= 0)(lambda k=k: fill(k).wait())

    def issue(r, carry):
        src = h_ref.at[pl.ds(r, 1)]
        pltpu.make_async_copy(src, xs_ref.at[pl.ds(p1_ref[base + r], 1)], sem.at[0]).start()
        pltpu.make_async_copy(src, xs_ref.at[pl.ds(p2_ref[base + r], 1)], sem.at[1]).start()
        return carry

    lax.fori_loop(0, tm, issue, 0)
    pltpu.make_async_copy(h_ref, xs_ref.at[pl.ds(0, tm)], sem.at[0]).wait()
    pltpu.make_async_copy(h_ref, xs_ref.at[pl.ds(0, tm)], sem.at[1]).wait()


def _dispatch(h, pos1, pos2, fill_rows, n_rows, tm, tm_fill):
    t, d = h.shape
    return pl.pallas_call(
        functools.partial(_dispatch_kernel, tm=tm, n_fill=fill_rows.shape[0]),
        grid_spec=pltpu.PrefetchScalarGridSpec(
            num_scalar_prefetch=3,
            grid=(t // tm,),
            in_specs=[pl.BlockSpec((tm, d), lambda i, p1, p2, fr: (i, 0))],
            out_specs=pl.BlockSpec(memory_space=pl.ANY),
            scratch_shapes=[pltpu.VMEM((tm_fill, d), h.dtype), pltpu.SemaphoreType.DMA((3,))],
        ),
        out_shape=jax.ShapeDtypeStruct((n_rows, d), h.dtype),
        compiler_params=_params(("arbitrary",)),
        name="moe_dispatch",
    )(pos1, pos2, fill_rows, h)


def _moe_kernel(te_ref, na_ref, x_ref, wg_ref, wu_ref, wd_ref, o_ref):
    @pl.when(pl.program_id(0) < na_ref[0])
    def _():
        x = x_ref[...].astype(BF16)
        gate = jnp.dot(x, wg_ref[...], preferred_element_type=F32)
        up = jnp.dot(x, wu_ref[...], preferred_element_type=F32)
        hid = (jax.nn.silu(gate) * up).astype(BF16)
        o_ref[...] = jnp.dot(hid, wd_ref[...], preferred_element_type=F32)

    @pl.when(pl.program_id(0) >= na_ref[0])
    def _():
        o_ref[...] = jnp.zeros_like(o_ref)


def _moe_experts(xs, tile_expert, n_active, w_gate, w_up, w_down, tm):
    p, d = xs.shape
    f = w_gate.shape[2]
    row_blk = lambda i, te, na: (jnp.minimum(i, na[0] - 1), 0)
    w_blk = lambda i, te, na: (te[i], 0, 0)
    return pl.pallas_call(
        _moe_kernel,
        grid_spec=pltpu.PrefetchScalarGridSpec(
            num_scalar_prefetch=2,
            grid=(p // tm,),
            in_specs=[pl.BlockSpec((tm, d), row_blk),
                      pl.BlockSpec((None, d, f), w_blk),
                      pl.BlockSpec((None, d, f), w_blk),
                      pl.BlockSpec((None, f, d), w_blk)],
            out_specs=pl.BlockSpec((tm, d), lambda i, te, na: (i, 0)),
        ),
        out_shape=jax.ShapeDtypeStruct((p, d), F32),
        compiler_params=_params(("arbitrary",)),
        name="moe_experts",
    )(tile_expert, n_active, xs, w_gate, w_up, w_down)


def _combine_ln_kernel(p1_ref, p2_ref, h_ref, g1_ref, g2_ref, lg_ref, lb_ref, ys_ref, o_ref,
                       buf1, buf2, sem, *, tm):
    base = pl.program_id(0) * tm

    def issue(r, carry):
        pltpu.make_async_copy(ys_ref.at[pl.ds(p1_ref[base + r], 1)], buf1.at[pl.ds(r, 1)], sem.at[0]).start()
        pltpu.make_async_copy(ys_ref.at[pl.ds(p2_ref[base + r], 1)], buf2.at[pl.ds(r, 1)], sem.at[1]).start()
        return carry

    lax.fori_loop(0, tm, issue, 0)
    pltpu.make_async_copy(ys_ref.at[pl.ds(0, tm)], buf1, sem.at[0]).wait()
    pltpu.make_async_copy(ys_ref.at[pl.ds(0, tm)], buf2, sem.at[1]).wait()
    ffn = g1_ref[...] * buf1[...] + g2_ref[...] * buf2[...]
    o_ref[...] = _layer_norm(ALPHA * h_ref[...] + ffn, lg_ref[...], lb_ref[...])


def _combine_ln(h, ys, pos1, pos2, g1, g2, ln_g, ln_b, tm):
    t, d = h.shape
    row = lambda i, p1, p2: (i, 0)
    const = lambda i, p1, p2: (0, 0)
    return pl.pallas_call(
        functools.partial(_combine_ln_kernel, tm=tm),
        grid_spec=pltpu.PrefetchScalarGridSpec(
            num_scalar_prefetch=2,
            grid=(t // tm,),
            in_specs=[pl.BlockSpec((tm, d), row),
                      pl.BlockSpec((tm, 1), row),
                      pl.BlockSpec((tm, 1), row),
                      pl.BlockSpec((1, d), const),
                      pl.BlockSpec((1, d), const),
                      pl.BlockSpec(memory_space=pl.ANY)],
            out_specs=pl.BlockSpec((tm, d), row),
            scratch_shapes=[pltpu.VMEM((tm, d), F32), pltpu.VMEM((tm, d), F32),
                            pltpu.SemaphoreType.DMA((2,))],
        ),
        out_shape=jax.ShapeDtypeStruct((t, d), F32),
        compiler_params=_params(("arbitrary",)),
        name="moe_combine_ln",
    )(pos1, pos2, h, g1, g2, ln_g, ln_b, ys)


def _moe_block(h, e, gw, rk, counts, w_gate, w_up, w_down, ln_g, ln_b, tm_tok):
    t, d = h.shape
    tm = MOE_TM
    n_rows = 2 * t + N_EXPERTS * tm
    n_tiles = n_rows // tm
    cnt = counts.astype(jnp.int32)
    padded = ((cnt + tm - 1) // tm) * tm
    ends = jnp.cumsum(padded)
    offs = ends - padded
    pos = jnp.take(offs, e) + rk
    n_active = jnp.maximum(ends[-1] // tm, 1)
    tile_start = jnp.minimum(jnp.arange(n_tiles, dtype=jnp.int32), n_active - 1) * tm
    tile_expert = jnp.minimum(jnp.searchsorted(ends, tile_start, side="right"), N_EXPERTS - 1).astype(jnp.int32)
    last_tile = jnp.where(padded > 0, ends - tm, -1)
    tail = ends[-1] + jnp.arange(N_EXPERTS, dtype=jnp.int32) * tm
    fill_rows = jnp.concatenate([last_tile, jnp.where(tail < n_rows, tail, -1)]).astype(jnp.int32)
    xs = _dispatch(h, pos[0], pos[1], fill_rows, n_rows, tm_tok, tm)
    ys = _moe_experts(xs, tile_expert, n_active.reshape(1).astype(jnp.int32),
                      w_gate.astype(BF16), w_up.astype(BF16), w_down.astype(BF16), tm)
    return _combine_ln(h, ys, pos[0], pos[1], gw[0].reshape(t, 1), gw[1].reshape(t, 1), ln_g, ln_b, tm_tok)


def kernel(x, ab_w_in, ab_lambda_re, ab_lambda_im, ab_log_dt, ab_b_re, ab_b_im, ab_c_re, ab_c_im, ab_d,
           ab_w_glu, ab_b_glu, ab_w_out, c_w_in, c_w_group, c_scale, c_w_out, ln_g, ln_b, router_w,
           router_b, moe_w_gate, moe_w_up, moe_w_down):
    bsz, seq, d = x.shape
    t = bsz * seq
    tm = min(ROW_TILE, t)
    assert t % tm == 0 and seq % tm == 0 and seq % S5_CHUNK == 0
    row = lambda a: a.reshape(1, -1)

    rw = jnp.zeros((d, LANES), F32).at[:, :N_EXPERTS].set(router_w)
    rw_hi = rw.astype(BF16)
    rw_lo = (rw - rw_hi.astype(F32)).astype(BF16)
    rb = jnp.zeros((1, LANES), F32).at[0, :N_EXPERTS].set(router_b)
    router = (rw_hi, rw_lo, rb)

    h = x.reshape(t, d)

    hb = h.astype(BF16)
    w_in = ab_w_in[0].astype(BF16)
    u = _mm(hb, w_in[:, :S5_WIDTH], F32, tm, S5_WIDTH)
    qkv = _mm(hb, w_in[:, S5_WIDTH:], BF16, tm, 768)
    chunks = seq // S5_CHUNK
    nsteps = max(int(math.log2(chunks)), 0)
    assert (1 << nsteps) == chunks
    ops = _s5_operands(ab_lambda_re[0], ab_lambda_im[0], ab_log_dt[0], ab_b_re[0], ab_b_im[0],
                       ab_c_re[0], ab_c_im[0], S5_CHUNK, nsteps)
    y_ssm = _s5_scan(u, ops, bsz, seq)
    y_a = _s5_glu(y_ssm, u, ab_d[0], ab_w_glu[0], ab_b_glu[0], tm)
    y_b = _sb_attention(qkv.reshape(bsz, seq, 3 * SB_WIDTH), bsz, seq).reshape(t, SB_WIDTH)
    w_out = ab_w_out[0].astype(BF16)
    h, e, gw, rk, cnt = _proj_ln_route([y_a, y_b], [w_out[:S5_WIDTH], w_out[S5_WIDTH:]], h,
                                       row(ln_g[0, 0]), row(ln_b[0, 0]), router, tm)
    h = _moe_block(h, e, gw, rk, cnt, moe_w_gate[0], moe_w_up[0], moe_w_down[0],
                   row(ln_g[0, 1]), row(ln_b[0, 1]), tm)

    pooled = _pool_in(h.astype(BF16), c_w_in[0].astype(BF16), seq, tm)
    h, e, gw, rk, cnt = _proj_ln_route([pooled], [c_w_out[0].astype(BF16)], h,
                                       row(ln_g[1, 0]), row(ln_b[1, 0]), router, tm,
                                       grouped=(c_w_group[0].astype(BF16), row(c_scale[0])))
    h = _moe_block(h, e, gw, rk, cnt, moe_w_gate[1], moe_w_up[1], moe_w_down[1],
                   row(ln_g[1, 1]), row(ln_b[1, 1]), tm)
    return h.reshape(bsz, seq, d)
```

```python
import functools
import math

import jax
import jax.numpy as jnp
from jax import lax
from jax.experimental import pallas as pl
from jax.experimental.pallas import tpu as pltpu

S5_WIDTH = 512
S5_GROUP = 16
S5_GROUPS = 32
S5_STATE = 64
SB_HEAD_DIM = 128
SB_HEADS = 12
SB_WIDTH = SB_HEADS * SB_HEAD_DIM
POOL_WINDOWS = (2, 4, 8, 16)
POOL_GROUP_WIDTH = 512
N_EXPERTS = 16
N_EXPERT_GROUPS = 4
EXPERTS_PER_GROUP = 4
DEPTH = 2
ALPHA = (2.0 * DEPTH) ** 0.25
LN_EPS = 1e-5

LANES = 128
VMEM_LIMIT = 56 * 1024 * 1024

S5_CHUNK = 32
ATT_TQ = 256
ATT_BK = 256
ATT_HEADS_PER_STEP = 2
ATT_LOG_UNDERFLOW = -88.0
MOE_TM = 256
ROW_TILE = 512

BF16 = jnp.bfloat16
F32 = jnp.float32


def _params(sem, vmem=VMEM_LIMIT):
    return pltpu.CompilerParams(dimension_semantics=sem, vmem_limit_bytes=vmem)


def _in_proj_ab_kernel(x_ref, w_ref, u_ref, qkv_ref, xb_ref):
    j = pl.program_id(1)

    @pl.when(j == 0)
    def _():
        xb_ref[...] = x_ref[...].astype(BF16)

    acc = jnp.dot(xb_ref[...], w_ref[...], preferred_element_type=F32)

    @pl.when(j == 0)
    def _():
        u_ref[...] = acc

    @pl.when(j > 0)
    def _():
        qkv_ref[...] = acc.astype(qkv_ref.dtype)


def _in_proj_ab(x, w, tm):
    m, k = x.shape
    tn = S5_WIDTH
    n_blocks = w.shape[1] // tn
    return pl.pallas_call(
        _in_proj_ab_kernel,
        grid=(m // tm, n_blocks),
        in_specs=[pl.BlockSpec((tm, k), lambda i, j: (i, 0)),
                  pl.BlockSpec((k, tn), lambda i, j: (0, j))],
        out_specs=[pl.BlockSpec((tm, tn), lambda i, j: (i, 0)),
                   pl.BlockSpec((tm, tn), lambda i, j: (i, jnp.maximum(j - 1, 0)))],
        out_shape=[jax.ShapeDtypeStruct((m, tn), F32),
                   jax.ShapeDtypeStruct((m, w.shape[1] - tn), BF16)],
        scratch_shapes=[pltpu.VMEM((tm, k), BF16)],
        compiler_params=_params(("parallel", "arbitrary")),
        name="in_proj_ab",
    )(x, w)


def _bmm_f32_kernel(x_ref, y_ref, o_ref):
    o_ref[...] = jnp.dot(x_ref[...], y_ref[...], preferred_element_type=F32,
                         precision=lax.Precision.HIGHEST)


def _bmm_f32(x, y):
    g, m, k = x.shape
    n = y.shape[2]
    return pl.pallas_call(
        _bmm_f32_kernel,
        grid=(g,),
        in_specs=[pl.BlockSpec((None, m, k), lambda i: (i, 0, 0)),
                  pl.BlockSpec((None, k, n), lambda i: (i, 0, 0))],
        out_specs=pl.BlockSpec((None, m, n), lambda i: (i, 0, 0)),
        out_shape=jax.ShapeDtypeStruct((g, m, n), F32),
        compiler_params=_params(("parallel",)),
        name="s5_taps",
    )(x, y)


def _s5_kernel(u_ref, wt_ref, ws_ref, wo_ref, a1_ref, a2_ref, o_ref, *, chunks, nsteps):
    u = u_ref[...]
    y_intra = jnp.dot(u, wt_ref[...], preferred_element_type=F32)
    s = jnp.dot(u, ws_ref[...], preferred_element_type=F32)
    m = s.shape[0]
    c_idx = lax.broadcasted_iota(jnp.int32, s.shape, 0) % chunks
    half = s.shape[1] // 2
    for k in range(nsteps):
        d = 1 << k
        sh = jnp.where(c_idx >= d, pltpu.roll(s, d, 0), 0.0)
        s = s + a1_ref[k:k + 1, :] * sh + a2_ref[k:k + 1, :] * pltpu.roll(sh, half, 1)
    s_prev = jnp.where(c_idx >= 1, pltpu.roll(s, 1, 0), 0.0)
    y_inter = jnp.dot(s_prev.astype(BF16), wo_ref[...], preferred_element_type=F32)
    o_ref[...] = y_intra + y_inter


def _s5_operands(lam_re, lam_im, log_dt, b_re, b_im, c_re, c_im, chunk, nsteps):
    g, n = lam_re.shape
    h = S5_GROUP
    dt = jnp.exp(log_dt)[:, None]
    lr, li = lam_re, lam_im
    mag = jnp.exp(lr * dt)
    lbr = mag * jnp.cos(li * dt)
    lbi = mag * jnp.sin(li * dt)
    den = lr * lr + li * li
    coef_re = ((lbr - 1.0) * lr + lbi * li) / den
    coef_im = (lbi * lr - (lbr - 1.0) * li) / den
    bbar_re = coef_re[..., None] * b_re - coef_im[..., None] * b_im
    bbar_im = coef_re[..., None] * b_im + coef_im[..., None] * b_re

    def power(mult):
        mult = mult.astype(F32)[None, :, None]
        pm = jnp.exp(lr[:, None, :] * dt[:, None, :] * mult)
        ang = li[:, None, :] * dt[:, None, :] * mult
        return pm * jnp.cos(ang), pm * jnp.sin(ang)

    pr, pi = power(jnp.arange(chunk + 1))
    cre = jnp.swapaxes(c_re, 1, 2)
    cim = jnp.swapaxes(c_im, 1, 2)
    cb_re = cre[:, :, :, None] * bbar_re[:, :, None, :] - cim[:, :, :, None] * bbar_im[:, :, None, :]
    cb_im = cre[:, :, :, None] * bbar_im[:, :, None, :] + cim[:, :, :, None] * bbar_re[:, :, None, :]
    y_op = jnp.concatenate([cb_re, cb_im], axis=1).reshape(g, 2 * n, h * h)
    x_op = jnp.concatenate([pr[:, :chunk], -pi[:, :chunk]], axis=-1)
    taps = _bmm_f32(x_op, y_op).reshape(g, chunk, h, h)
    jj = jnp.arange(chunk)[:, None]
    tt = jnp.arange(chunk)[None, :]
    lag = jnp.clip(tt - jj, 0, chunk - 1)
    wt = jnp.where((tt >= jj)[None, :, :, None, None], taps[:, lag], 0.0)
    wt = wt.transpose(0, 1, 4, 2, 3).reshape(g, chunk * h, chunk * h)
    prr = pr[:, chunk - 1::-1][:, :chunk]
    pir = pi[:, chunk - 1::-1][:, :chunk]
    ws_re = prr[:, :, None, :] * jnp.swapaxes(bbar_re, 1, 2)[:, None] - pir[:, :, None, :] * jnp.swapaxes(bbar_im, 1, 2)[:, None]
    ws_im = prr[:, :, None, :] * jnp.swapaxes(bbar_im, 1, 2)[:, None] + pir[:, :, None, :] * jnp.swapaxes(bbar_re, 1, 2)[:, None]
    ws = jnp.concatenate([ws_re, ws_im], axis=-1).reshape(g, chunk * h, 2 * n)
    p1r = pr[:, 1:chunk + 1]
    p1i = pi[:, 1:chunk + 1]
    wo_re = c_re[:, None] * p1r[:, :, None, :] - c_im[:, None] * p1i[:, :, None, :]
    wo_im = -(c_re[:, None] * p1i[:, :, None, :] + c_im[:, None] * p1r[:, :, None, :])
    wo = jnp.concatenate([wo_re, wo_im], axis=-1)
    wo = wo.transpose(0, 3, 1, 2).reshape(g, 2 * n, chunk * h)
    sr, si = power(chunk * (2 ** jnp.arange(max(nsteps, 1))))
    a1 = jnp.concatenate([sr, sr], axis=-1)
    a2 = jnp.concatenate([-si, si], axis=-1)
    return wt.astype(BF16), ws.astype(BF16), wo.astype(BF16), a1, a2


def _s5_scan(u, ops, bsz, seq):
    wt, ws, wo, a1, a2 = ops
    g, h, chunk = S5_GROUPS, S5_GROUP, S5_CHUNK
    chunks = seq // chunk
    m = bsz * chunks
    nsteps = a1.shape[1] if chunks > 1 else 0
    up = u.reshape(bsz, chunks, chunk, g, h).transpose(3, 0, 1, 2, 4).reshape(g, m, chunk * h).astype(BF16)
    n2 = ws.shape[2]
    ns = a1.shape[1]
    yp = pl.pallas_call(
        functools.partial(_s5_kernel, chunks=chunks, nsteps=nsteps),
        grid=(g,),
        in_specs=[pl.BlockSpec((None, m, chunk * h), lambda i: (i, 0, 0)),
                  pl.BlockSpec((None, chunk * h, chunk * h), lambda i: (i, 0, 0)),
                  pl.BlockSpec((None, chunk * h, n2), lambda i: (i, 0, 0)),
                  pl.BlockSpec((None, n2, chunk * h), lambda i: (i, 0, 0)),
                  pl.BlockSpec((None, ns, n2), lambda i: (i, 0, 0)),
                  pl.BlockSpec((None, ns, n2), lambda i: (i, 0, 0))],
        out_specs=pl.BlockSpec((None, m, chunk * h), lambda i: (i, 0, 0)),
        out_shape=jax.ShapeDtypeStruct((g, m, chunk * h), F32),
        compiler_params=_params(("parallel",)),
        name="s5_scan",
    )(up, wt, ws, wo, a1, a2)
    return yp.reshape(g, bsz, chunks, chunk, h).transpose(1, 2, 3, 0, 4).reshape(bsz * seq, g * h)


def _s5_glu_kernel(y_ref, u_ref, d_ref, w_ref, b_ref, o_ref):
    y = y_ref[...] + d_ref[...] * u_ref[...]
    gl = jax.nn.gelu(y)
    gate = jnp.dot(gl.astype(BF16), w_ref[...], preferred_element_type=F32) + b_ref[...]
    o_ref[...] = (gl * jax.nn.sigmoid(gate)).astype(o_ref.dtype)


def _s5_glu(y, u, d, w_glu, b_glu, tm):
    t, c = y.shape
    return pl.pallas_call(
        _s5_glu_kernel,
        grid=(t // tm,),
        in_specs=[pl.BlockSpec((tm, c), lambda i: (i, 0)),
                  pl.BlockSpec((tm, c), lambda i: (i, 0)),
                  pl.BlockSpec((1, c), lambda i: (0, 0)),
                  pl.BlockSpec((c, c), lambda i: (0, 0)),
                  pl.BlockSpec((1, c), lambda i: (0, 0))],
        out_specs=pl.BlockSpec((tm, c), lambda i: (i, 0)),
        out_shape=jax.ShapeDtypeStruct((t, c), BF16),
        compiler_params=_params(("parallel",)),
        name="s5_glu",
    )(y, u, d.reshape(1, c), w_glu.astype(BF16), b_glu.reshape(1, c))


def _sb_attn_kernel(q_ref, k_ref, v_ref, o_ref, acc_ref, r_ref, tri_ref, *, seq, tq, bk, scale):
    dh = SB_HEAD_DIM
    row = lax.broadcasted_iota(jnp.int32, (bk, bk), 0)
    col = lax.broadcasted_iota(jnp.int32, (bk, bk), 1)
    tri_ref[...] = jnp.where(row > col, 1.0, 0.0).astype(BF16)
    nq = seq // tq
    for hh in range(ATT_HEADS_PER_STEP):
        cs = slice(hh * dh, (hh + 1) * dh)

        def q_tile(i, carry, cs=cs):
            q = q_ref[pl.ds(pl.multiple_of(i * tq, tq), tq), cs]
            acc_ref[...] = jnp.zeros_like(acc_ref)
            r_ref[...] = jnp.zeros_like(r_ref)
            q_pos = i * tq + lax.broadcasted_iota(jnp.int32, (tq, bk), 0)

            def cond(c):
                j, rmax = c
                return jnp.logical_and(j >= 0, rmax > ATT_LOG_UNDERFLOW)

            def body(c):
                j, _ = c
                k0 = pl.multiple_of(j * bk, bk)
                kb = k_ref[pl.ds(k0, bk), cs]
                vb = v_ref[pl.ds(k0, bk), cs]
                z = lax.dot_general(q, kb, (((1,), (1,)), ((), ())), preferred_element_type=F32) * scale
                causal = (j * bk + lax.broadcasted_iota(jnp.int32, (tq, bk), 1)) < q_pos
                sp = jnp.maximum(z, 0.0) + jnp.log1p(jnp.exp(-jnp.abs(z)))
                log_keep = jnp.where(causal, -sp, 0.0)
                hi = log_keep.astype(BF16)
                lo = (log_keep - hi.astype(F32)).astype(BF16)
                tri = tri_ref[...]
                r_old = r_ref[...]
                tail = (jnp.dot(hi, tri, preferred_element_type=F32)
                        + jnp.dot(lo, tri, preferred_element_type=F32) + r_old)
                w = jnp.where(causal, jnp.exp(z - sp + tail), 0.0)
                acc_ref[...] += jnp.dot(w.astype(BF16), vb, preferred_element_type=F32)
                r_new = r_old + jnp.sum(log_keep, axis=1, keepdims=True)
                r_ref[...] = r_new
                return j - 1, jnp.max(r_new)

            lax.while_loop(cond, body, (i * (tq // bk) + (tq // bk - 1), jnp.float32(0.0)))
            o_ref[pl.ds(pl.multiple_of(i * tq, tq), tq), cs] = acc_ref[...].astype(o_ref.dtype)
            return carry

        lax.fori_loop(0, nq, q_tile, 0)


def _sb_attention(qkv, bsz, seq):
    tq = min(ATT_TQ, seq)
    bk = min(ATT_BK, seq)
    assert tq % bk == 0
    wcols = ATT_HEADS_PER_STEP * SB_HEAD_DIM
    nhp = SB_WIDTH // wcols
    kern = functools.partial(_sb_attn_kernel, seq=seq, tq=tq, bk=bk, scale=1.0 / math.sqrt(SB_HEAD_DIM))
    return pl.pallas_call(
        kern,
        grid=(bsz, nhp),
        in_specs=[pl.BlockSpec((None, seq, wcols), lambda b, h: (b, 0, h)),
                  pl.BlockSpec((None, seq, wcols), lambda b, h: (b, 0, nhp + h)),
                  pl.BlockSpec((None, seq, wcols), lambda b, h: (b, 0, 2 * nhp + h))],
        out_specs=pl.BlockSpec((None, seq, wcols), lambda b, h: (b, 0, h)),
        out_shape=jax.ShapeDtypeStruct((bsz, seq, SB_WIDTH), BF16),
        scratch_shapes=[pltpu.VMEM((tq, SB_HEAD_DIM), F32),
                        pltpu.VMEM((tq, 1), F32),
                        pltpu.VMEM((bk, bk), BF16)],
        compiler_params=_params(("parallel", "parallel")),
        name="sb_attention",
    )(qkv, qkv, qkv)


def _pool_in_kernel(x_ref, w_ref, o_ref, halo_ref, *, seq, tm):
    i = pl.program_id(0)
    hmax = POOL_WINDOWS[-1]
    acc = jnp.dot(x_ref[...].astype(BF16), w_ref[...], preferred_element_type=F32)
    t0 = (i * tm) % seq

    @pl.when(t0 == 0)
    def _():
        halo_ref[...] = jnp.zeros_like(halo_ref)

    ext = jnp.concatenate([halo_ref[...], acc], axis=0)
    t_idx = t0 + lax.broadcasted_iota(jnp.int32, (tm, POOL_GROUP_WIDTH), 0)
    for gi, win in enumerate(POOL_WINDOWS):
        cs = slice(gi * POOL_GROUP_WIDTH, (gi + 1) * POOL_GROUP_WIDTH)
        e = ext[:, cs]
        span = 1
        while span < win:
            e = e + pltpu.roll(e, span, 0)
            span *= 2
        wsum = e[hmax:, :]
        cnt = jnp.minimum(t_idx + 1, win).astype(F32)
        o_ref[:, cs] = (wsum / cnt - acc[:, cs]).astype(o_ref.dtype)
    halo_ref[...] = acc[tm - hmax:, :]


def _pool_in(x, w, seq, tm):
    t, k = x.shape
    n = w.shape[1]
    return pl.pallas_call(
        functools.partial(_pool_in_kernel, seq=seq, tm=tm),
        grid=(t // tm,),
        in_specs=[pl.BlockSpec((tm, k), lambda i: (i, 0)),
                  pl.BlockSpec((k, n), lambda i: (0, 0))],
        out_specs=pl.BlockSpec((tm, n), lambda i: (i, 0)),
        out_shape=jax.ShapeDtypeStruct((t, n), BF16),
        scratch_shapes=[pltpu.VMEM((POOL_WINDOWS[-1], n), F32)],
        compiler_params=_params(("arbitrary",)),
        name="pool_in",
    )(x, w)


def _layer_norm(x, g, b):
    mu = jnp.mean(x, axis=-1, keepdims=True)
    xc = x - mu
    var = jnp.mean(xc * xc, axis=-1, keepdims=True)
    return xc * lax.rsqrt(var + LN_EPS) * g + b


def _route(h, rwh_ref, rwl_ref, rb_ref, base_ref, tm):
    h_hi = h.astype(BF16)
    h_lo = (h - h_hi.astype(F32)).astype(BF16)
    rwh = rwh_ref[...]
    logits = (jnp.dot(h_hi, rwh, preferred_element_type=F32)
              + jnp.dot(h_lo, rwh, preferred_element_type=F32)
              + jnp.dot(h_hi, rwl_ref[...], preferred_element_type=F32)) + rb_ref[...]
    lt = logits.T[:N_EXPERTS, :]
    mx = jnp.max(lt, axis=0, keepdims=True)
    ex = jnp.exp(lt - mx)
    probs = ex / jnp.sum(ex, axis=0, keepdims=True)
    p = [probs[e:e + 1, :] for e in range(N_EXPERTS)]

    def first_max(vals):
        best, idx = vals[0], jnp.zeros_like(vals[0], dtype=jnp.int32)
        for n in range(1, len(vals)):
            upd = vals[n] > best
            idx = jnp.where(upd, n, idx)
            best = jnp.where(upd, vals[n], best)
        return best, idx

    scores = []
    for gi in range(N_EXPERT_GROUPS):
        p0, p1, p2, p3 = p[4 * gi:4 * gi + 4]
        a, b = jnp.maximum(p0, p1), jnp.minimum(p0, p1)
        c, d = jnp.maximum(p2, p3), jnp.minimum(p2, p3)
        scores.append(jnp.maximum(a, c) + jnp.maximum(jnp.minimum(a, c), jnp.maximum(b, d)))
    _, g_sel = first_max(scores)
    sel = []
    for n in range(EXPERTS_PER_GROUP):
        v = p[n]
        for gi in range(1, N_EXPERT_GROUPS):
            v = jnp.where(g_sel == gi, p[4 * gi + n], v)
        sel.append(v)
    v1, i1 = first_max(sel)
    v2, i2 = first_max([jnp.where(i1 == n, -1.0, sel[n]) for n in range(EXPERTS_PER_GROUP)])
    den = v1 + v2
    e1 = g_sel * EXPERTS_PER_GROUP + i1
    e2 = g_sel * EXPERTS_PER_GROUP + i2
    e_iota = lax.broadcasted_iota(jnp.int32, (N_EXPERTS, tm), 0)
    oh1 = e_iota == e1
    oh2 = e_iota == e2
    oh = jnp.where(jnp.logical_or(oh1, oh2), 1.0, 0.0)
    r_i = lax.broadcasted_iota(jnp.int32, (tm, tm), 0)
    c_i = lax.broadcasted_iota(jnp.int32, (tm, tm), 1)
    before = jnp.where(r_i < c_i, 1.0, 0.0).astype(BF16)
    cnt = jnp.dot(oh.astype(BF16), before, preferred_element_type=F32) + base_ref[...]
    rank1 = jnp.sum(jnp.where(oh1, cnt, 0.0), axis=0, keepdims=True)
    rank2 = jnp.sum(jnp.where(oh2, cnt, 0.0), axis=0, keepdims=True)
    base_ref[...] += jnp.sum(oh, axis=1, keepdims=True)
    e_out = jnp.concatenate([e1, e2], axis=0)
    w_out = jnp.concatenate([v1 / den, v2 / den], axis=0)
    r_out = jnp.concatenate([rank1, rank2], axis=0).astype(jnp.int32)
    return e_out, w_out, r_out


def _proj_ln_kernel(*refs, n_x, grouped, tm):
    xs = refs[:n_x]
    ws = refs[n_x:2 * n_x]
    pos = 2 * n_x
    if grouped:
        wg_ref, sc_ref = refs[pos:pos + 2]
        pos += 2
    res_ref, g_ref, b_ref, rwh_ref, rwl_ref, rb_ref = refs[pos:pos + 6]
    h_ref, e_ref, gw_ref, rk_ref, cnt_ref, base_ref = refs[pos + 6:pos + 12]

    @pl.when(pl.program_id(0) == 0)
    def _():
        base_ref[...] = jnp.zeros_like(base_ref)

    if grouped:
        x = xs[0][...]
        parts = []
        for gi in range(len(POOL_WINDOWS)):
            cs = slice(gi * POOL_GROUP_WIDTH, (gi + 1) * POOL_GROUP_WIDTH)
            parts.append(jnp.dot(x[:, cs], wg_ref[gi], preferred_element_type=F32))
        y = (jnp.concatenate(parts, axis=1) * sc_ref[...]).astype(BF16)
        mix = jnp.dot(y, ws[0][...], preferred_element_type=F32)
    else:
        mix = jnp.dot(xs[0][...], ws[0][...], preferred_element_type=F32)
        for n in range(1, n_x):
            mix = mix + jnp.dot(xs[n][...], ws[n][...], preferred_element_type=F32)
    h = _layer_norm(ALPHA * res_ref[...] + mix, g_ref[...], b_ref[...])
    h_ref[...] = h
    e_out, w_out, r_out = _route(h, rwh_ref, rwl_ref, rb_ref, base_ref, tm)
    e_ref[...] = e_out
    gw_ref[...] = w_out
    rk_ref[...] = r_out
    cnt_ref[...] = jnp.broadcast_to(base_ref[...], cnt_ref.shape)


def _proj_ln_route(xs, ws, res, ln_g, ln_b, router, tm, grouped=None):
    t, d = res.shape
    rwh, rwl, rb = router
    n_x = len(xs)
    full = lambda a: pl.BlockSpec(a.shape, lambda i, nd=a.ndim: (0,) * nd)
    in_specs = [pl.BlockSpec((tm, x.shape[1]), lambda i: (i, 0)) for x in xs]
    in_specs += [full(w) for w in ws]
    args = list(xs) + list(ws)
    if grouped is not None:
        in_specs += [full(a) for a in grouped]
        args += list(grouped)
    in_specs += [pl.BlockSpec((tm, d), lambda i: (i, 0)), full(ln_g), full(ln_b), full(rwh), full(rwl), full(rb)]
    args += [res, ln_g, ln_b, rwh, rwl, rb]
    pair = lambda dt: jax.ShapeDtypeStruct((2, t), dt)
    out = pl.pallas_call(
        functools.partial(_proj_ln_kernel, n_x=n_x, grouped=grouped is not None, tm=tm),
        grid=(t // tm,),
        in_specs=in_specs,
        out_specs=[pl.BlockSpec((tm, d), lambda i: (i, 0)),
                   pl.BlockSpec((2, tm), lambda i: (0, i)),
                   pl.BlockSpec((2, tm), lambda i: (0, i)),
                   pl.BlockSpec((2, tm), lambda i: (0, i)),
                   pl.BlockSpec((N_EXPERTS, LANES), lambda i: (0, 0))],
        out_shape=[jax.ShapeDtypeStruct((t, d), F32), pair(jnp.int32), pair(F32), pair(jnp.int32),
                   jax.ShapeDtypeStruct((N_EXPERTS, LANES), F32)],
        scratch_shapes=[pltpu.VMEM((N_EXPERTS, 1), F32)],
        compiler_params=_params(("arbitrary",)),
        name="proj_ln_route",
    )(*args)
    h, e, gw, rk, cnt = out
    return h, e, gw, rk, cnt[:, 0]


def _dispatch_kernel(p1_ref, p2_ref, fill_ref, h_ref, xs_ref, zero_ref, sem, *, tm, n_fill):
    base = pl.program_id(0) * tm

    @pl.when(pl.program_id(0) == 0)
    def _():
        zero_ref[...] = jnp.zeros_like(zero_ref)
        n_zero = zero_ref.shape[0]
        fill = lambda k: pltpu.make_async_copy(
            zero_ref, xs_ref.at[pl.ds(pl.multiple_of(jnp.maximum(fill_ref[k], 0), n_zero), n_zero)], sem.at[2])
        for k in range(n_fill):
            pl.when(fill_ref[k] >= 0)(lambda k=k: fill(k).start())
        for k in range(n_fill):
            pl.when(fill_ref[k] >= 0)(lambda k=k: fill(k).wait())

    def issue(r, carry):
        src = h_ref.at[pl.ds(r, 1)]
        pltpu.make_async_copy(src, xs_ref.at[pl.ds(p1_ref[base + r], 1)], sem.at[0]).start()
        pltpu.make_async_copy(src, xs_ref.at[pl.ds(p2_ref[base + r], 1)], sem.at[1]).start()
        return carry

    lax.fori_loop(0, tm, issue, 0)
    pltpu.make_async_copy(h_ref, xs_ref.at[pl.ds(0, tm)], sem.at[0]).wait()
    pltpu.make_async_copy(h_ref, xs_ref.at[pl.ds(0, tm)], sem.at[1]).wait()


def _dispatch(h, pos1, pos2, fill_rows, n_rows, tm, tm_fill):
    t, d = h.shape
    return pl.pallas_call(
        functools.partial(_dispatch_kernel, tm=tm, n_fill=fill_rows.shape[0]),
        grid_spec=pltpu.PrefetchScalarGridSpec(
            num_scalar_prefetch=3,
            grid=(t // tm,),
            in_specs=[pl.BlockSpec((tm, d), lambda i, p1, p2, fr: (i, 0))],
            out_specs=pl.BlockSpec(memory_space=pl.ANY),
            scratch_shapes=[pltpu.VMEM((tm_fill, d), h.dtype), pltpu.SemaphoreType.DMA((3,))],
        ),
        out_shape=jax.ShapeDtypeStruct((n_rows, d), h.dtype),
        compiler_params=_params(("arbitrary",)),
        name="moe_dispatch",
    )(pos1, pos2, fill_rows, h)


def _moe_kernel(te_ref, na_ref, x_ref, wg_ref, wu_ref, wd_ref, o_ref):
    @pl.when(pl.program_id(0) < na_ref[0])
    def _():
        x = x_ref[...].astype(BF16)
        gate = jnp.dot(x, wg_ref[...], preferred_element_type=F32)
        up = jnp.dot(x, wu_ref[...], preferred_element_type=F32)
        hid = (jax.nn.silu(gate) * up).astype(BF16)
        o_ref[...] = jnp.dot(hid, wd_ref[...], preferred_element_type=F32)

    @pl.when(pl.program_id(0) >= na_ref[0])
    def _():
        o_ref[...] = jnp.zeros_like(o_ref)


def _moe_experts(xs, tile_expert, n_active, w_gate, w_up, w_down, layer, tm):
    p, d = xs.shape
    f = w_gate.shape[3]
    row_blk = lambda i, te, na: (jnp.minimum(i, na[0] - 1), 0)
    w_blk = lambda i, te, na: (layer, te[i], 0, 0)
    return pl.pallas_call(
        _moe_kernel,
        grid_spec=pltpu.PrefetchScalarGridSpec(
            num_scalar_prefetch=2,
            grid=(p // tm,),
            in_specs=[pl.BlockSpec((tm, d), row_blk),
                      pl.BlockSpec((None, None, d, f), w_blk),
                      pl.BlockSpec((None, None, d, f), w_blk),
                      pl.BlockSpec((None, None, f, d), w_blk)],
            out_specs=pl.BlockSpec((tm, d), lambda i, te, na: (i, 0)),
        ),
        out_shape=jax.ShapeDtypeStruct((p, d), F32),
        compiler_params=_params(("arbitrary",)),
        name="moe_experts",
    )(tile_expert, n_active, xs, w_gate, w_up, w_down)


def _combine_ln_kernel(p1_ref, p2_ref, h_ref, g1_ref, g2_ref, lg_ref, lb_ref, ys_ref, o_ref,
                       buf1, buf2, sem, *, tm):
    base = pl.program_id(0) * tm

    def issue(r, carry):
        pltpu.make_async_copy(ys_ref.at[pl.ds(p1_ref[base + r], 1)], buf1.at[pl.ds(r, 1)], sem.at[0]).start()
        pltpu.make_async_copy(ys_ref.at[pl.ds(p2_ref[base + r], 1)], buf2.at[pl.ds(r, 1)], sem.at[1]).start()
        return carry

    lax.fori_loop(0, tm, issue, 0)
    pltpu.make_async_copy(ys_ref.at[pl.ds(0, tm)], buf1, sem.at[0]).wait()
    pltpu.make_async_copy(ys_ref.at[pl.ds(0, tm)], buf2, sem.at[1]).wait()
    ffn = g1_ref[...] * buf1[...] + g2_ref[...] * buf2[...]
    o_ref[...] = _layer_norm(ALPHA * h_ref[...] + ffn, lg_ref[...], lb_ref[...])


def _combine_ln(h, ys, pos1, pos2, g1, g2, ln_g, ln_b, tm):
    t, d = h.shape
    row = lambda i, p1, p2: (i, 0)
    const = lambda i, p1, p2: (0, 0)
    return pl.pallas_call(
        functools.partial(_combine_ln_kernel, tm=tm),
        grid_spec=pltpu.PrefetchScalarGridSpec(
            num_scalar_prefetch=2,
            grid=(t // tm,),
            in_specs=[pl.BlockSpec((tm, d), row),
                      pl.BlockSpec((tm, 1), row),
                      pl.BlockSpec((tm, 1), row),
                      pl.BlockSpec((1, d), const),
                      pl.BlockSpec((1, d), const),
                      pl.BlockSpec(memory_space=pl.ANY)],
            out_specs=pl.BlockSpec((tm, d), row),
            scratch_shapes=[pltpu.VMEM((tm, d), F32), pltpu.VMEM((tm, d), F32),
                            pltpu.SemaphoreType.DMA((2,))],
        ),
        out_shape=jax.ShapeDtypeStruct((t, d), F32),
        compiler_params=_params(("arbitrary",)),
        name="moe_combine_ln",
    )(pos1, pos2, h, g1, g2, ln_g, ln_b, ys)


def _moe_block(h, e, gw, rk, counts, w_gate, w_up, w_down, layer, ln_g, ln_b, tm_tok):
    t, d = h.shape
    tm = MOE_TM
    n_rows = 2 * t + N_EXPERTS * tm
    n_tiles = n_rows // tm
    cnt = counts.astype(jnp.int32)
    padded = ((cnt + tm - 1) // tm) * tm
    ends = jnp.cumsum(padded)
    offs = ends - padded
    e_ids = jnp.arange(N_EXPERTS, dtype=jnp.int32)
    pos = jnp.sum(jnp.where(e[..., None] == e_ids, offs, 0), axis=-1) + rk
    n_active = jnp.maximum(ends[-1] // tm, 1)
    tile_start = jnp.minimum(jnp.arange(n_tiles, dtype=jnp.int32), n_active - 1) * tm
    tile_expert = jnp.minimum(jnp.sum((ends[None, :] <= tile_start[:, None]).astype(jnp.int32), axis=1),
                              N_EXPERTS - 1)
    last_tile = jnp.where(padded > 0, ends - tm, -1)
    tail = ends[-1] + jnp.arange(N_EXPERTS, dtype=jnp.int32) * tm
    fill_rows = jnp.concatenate([last_tile, jnp.where(tail < n_rows, tail, -1)]).astype(jnp.int32)
    xs = _dispatch(h, pos[0], pos[1], fill_rows, n_rows, tm_tok, tm)
    ys = _moe_experts(xs, tile_expert, n_active.reshape(1).astype(jnp.int32), w_gate, w_up, w_down, layer, tm)
    return _combine_ln(h, ys, pos[0], pos[1], gw[0].reshape(t, 1), gw[1].reshape(t, 1), ln_g, ln_b, tm_tok)


def kernel(x, ab_w_in, ab_lambda_re, ab_lambda_im, ab_log_dt, ab_b_re, ab_b_im, ab_c_re, ab_c_im, ab_d,
           ab_w_glu, ab_b_glu, ab_w_out, c_w_in, c_w_group, c_scale, c_w_out, ln_g, ln_b, router_w,
           router_b, moe_w_gate, moe_w_up, moe_w_down):
    bsz, seq, d = x.shape
    t = bsz * seq
    tm = min(ROW_TILE, t)
    assert t % tm == 0 and seq % tm == 0 and seq % S5_CHUNK == 0
    row = lambda a: a.reshape(1, -1)

    rw = jnp.zeros((d, LANES), F32).at[:, :N_EXPERTS].set(router_w)
    rw_hi = rw.astype(BF16)
    rw_lo = (rw - rw_hi.astype(F32)).astype(BF16)
    rb = jnp.zeros((1, LANES), F32).at[0, :N_EXPERTS].set(router_b)
    router = (rw_hi, rw_lo, rb)

    h = x.reshape(t, d)

    wg_b, wu_b, wd_b = moe_w_gate.astype(BF16), moe_w_up.astype(BF16), moe_w_down.astype(BF16)
    u, qkv = _in_proj_ab(h, ab_w_in[0].astype(BF16), min(2 * ROW_TILE, t))
    chunks = seq // S5_CHUNK
    nsteps = max(int(math.log2(chunks)), 0)
    assert (1 << nsteps) == chunks
    ops = _s5_operands(ab_lambda_re[0], ab_lambda_im[0], ab_log_dt[0], ab_b_re[0], ab_b_im[0],
                       ab_c_re[0], ab_c_im[0], S5_CHUNK, nsteps)
    y_ssm = _s5_scan(u, ops, bsz, seq)
    y_a = _s5_glu(y_ssm, u, ab_d[0], ab_w_glu[0], ab_b_glu[0], tm)
    y_b = _sb_attention(qkv.reshape(bsz, seq, 3 * SB_WIDTH), bsz, seq).reshape(t, SB_WIDTH)
    w_out = ab_w_out[0].astype(BF16)
    h, e, gw, rk, cnt = _proj_ln_route([y_a, y_b], [w_out[:S5_WIDTH], w_out[S5_WIDTH:]], h,
                                       row(ln_g[0, 0]), row(ln_b[0, 0]), router, tm)
    h = _moe_block(h, e, gw, rk, cnt, wg_b, wu_b, wd_b, 0, row(ln_g[0, 1]), row(ln_b[0, 1]), tm)

    pooled = _pool_in(h, c_w_in[0].astype(BF16), seq, tm)
    h, e, gw, rk, cnt = _proj_ln_route([pooled], [c_w_out[0].astype(BF16)], h,
                                       row(ln_g[1, 0]), row(ln_b[1, 0]), router, tm,
                                       grouped=(c_w_group[0].astype(BF16), row(c_scale[0])))
    h = _moe_block(h, e, gw, rk, cnt, wg_b, wu_b, wd_b, 1, row(ln_g[1, 1]), row(ln_b[1, 1]), tm)
    return h.reshape(bsz, seq, d)
```

```python
import functools
import math

import jax
import jax.numpy as jnp
from jax import lax
from jax.experimental import pallas as pl
from jax.experimental.pallas import tpu as pltpu

S5_WIDTH = 512
S5_GROUP = 16
S5_GROUPS = 32
S5_STATE = 64
SB_HEAD_DIM = 128
SB_HEADS = 12
SB_WIDTH = SB_HEADS * SB_HEAD_DIM
POOL_WINDOWS = (2, 4, 8, 16)
POOL_GROUP_WIDTH = 512
N_EXPERTS = 16
N_EXPERT_GROUPS = 4
EXPERTS_PER_GROUP = 4
DEPTH = 2
ALPHA = (2.0 * DEPTH) ** 0.25
LN_EPS = 1e-5

LANES = 128
VMEM_LIMIT = 56 * 1024 * 1024

S5_CHUNK = 32
ATT_TQ = 128
ATT_HEADS_PER_STEP = 2
ATT_TILES_PER_ITER = 2
ATT_LOG_UNDERFLOW = -88.0
MOE_TM = 256
ROW_TILE = 512

BF16 = jnp.bfloat16
F32 = jnp.float32


def _params(sem, vmem=VMEM_LIMIT):
    return pltpu.CompilerParams(dimension_semantics=sem, vmem_limit_bytes=vmem)


def _in_proj_ab_kernel(x_ref, w_ref, u_ref, qkv_ref, xb_ref):
    j = pl.program_id(1)

    @pl.when(j == 0)
    def _():
        xb_ref[...] = x_ref[...].astype(BF16)

    acc = jnp.dot(xb_ref[...], w_ref[...], preferred_element_type=F32)

    @pl.when(j == 0)
    def _():
        u_ref[...] = acc

    @pl.when(j > 0)
    def _():
        qkv_ref[...] = acc.astype(qkv_ref.dtype)


def _in_proj_ab(x, w, tm):
    m, k = x.shape
    tn = S5_WIDTH
    n_blocks = w.shape[1] // tn
    return pl.pallas_call(
        _in_proj_ab_kernel,
        grid=(m // tm, n_blocks),
        in_specs=[pl.BlockSpec((tm, k), lambda i, j: (i, 0)),
                  pl.BlockSpec((k, tn), lambda i, j: (0, j))],
        out_specs=[pl.BlockSpec((tm, tn), lambda i, j: (i, 0)),
                   pl.BlockSpec((tm, tn), lambda i, j: (i, jnp.maximum(j - 1, 0)))],
        out_shape=[jax.ShapeDtypeStruct((m, tn), F32),
                   jax.ShapeDtypeStruct((m, w.shape[1] - tn), BF16)],
        scratch_shapes=[pltpu.VMEM((tm, k), BF16)],
        compiler_params=_params(("parallel", "arbitrary")),
        name="in_proj_ab",
    )(x, w)


def _bmm_f32_kernel(x_ref, y_ref, o_ref):
    o_ref[...] = jnp.dot(x_ref[...], y_ref[...], preferred_element_type=F32,
                         precision=lax.Precision.HIGHEST)


def _bmm_f32(x, y):
    g, m, k = x.shape
    n = y.shape[2]
    return pl.pallas_call(
        _bmm_f32_kernel,
        grid=(g,),
        in_specs=[pl.BlockSpec((None, m, k), lambda i: (i, 0, 0)),
                  pl.BlockSpec((None, k, n), lambda i: (i, 0, 0))],
        out_specs=pl.BlockSpec((None, m, n), lambda i: (i, 0, 0)),
        out_shape=jax.ShapeDtypeStruct((g, m, n), F32),
        compiler_params=_params(("parallel",)),
        name="s5_taps",
    )(x, y)


def _s5_kernel(u_ref, wt_ref, ws_ref, wo_ref, a1_ref, a2_ref, o_ref, *, chunks, nsteps):
    u = u_ref[...]
    y_intra = jnp.dot(u, wt_ref[...], preferred_element_type=F32)
    s = jnp.dot(u, ws_ref[...], preferred_element_type=F32)
    m = s.shape[0]
    c_idx = lax.broadcasted_iota(jnp.int32, s.shape, 0) % chunks
    half = s.shape[1] // 2
    for k in range(nsteps):
        d = 1 << k
        sh = jnp.where(c_idx >= d, pltpu.roll(s, d, 0), 0.0)
        s = s + a1_ref[k:k + 1, :] * sh + a2_ref[k:k + 1, :] * pltpu.roll(sh, half, 1)
    s_prev = jnp.where(c_idx >= 1, pltpu.roll(s, 1, 0), 0.0)
    y_inter = jnp.dot(s_prev.astype(BF16), wo_ref[...], preferred_element_type=F32)
    o_ref[...] = y_intra + y_inter


def _s5_operands(lam_re, lam_im, log_dt, b_re, b_im, c_re, c_im, chunk, nsteps):
    g, n = lam_re.shape
    h = S5_GROUP
    dt = jnp.exp(log_dt)[:, None]
    lr, li = lam_re, lam_im
    mag = jnp.exp(lr * dt)
    lbr = mag * jnp.cos(li * dt)
    lbi = mag * jnp.sin(li * dt)
    den = lr * lr + li * li
    coef_re = ((lbr - 1.0) * lr + lbi * li) / den
    coef_im = (lbi * lr - (lbr - 1.0) * li) / den
    bbar_re = coef_re[..., None] * b_re - coef_im[..., None] * b_im
    bbar_im = coef_re[..., None] * b_im + coef_im[..., None] * b_re

    def power(mult):
        mult = mult.astype(F32)[None, :, None]
        pm = jnp.exp(lr[:, None, :] * dt[:, None, :] * mult)
        ang = li[:, None, :] * dt[:, None, :] * mult
        return pm * jnp.cos(ang), pm * jnp.sin(ang)

    pr, pi = power(jnp.arange(chunk + 1))
    cre = jnp.swapaxes(c_re, 1, 2)
    cim = jnp.swapaxes(c_im, 1, 2)
    cb_re = cre[:, :, :, None] * bbar_re[:, :, None, :] - cim[:, :, :, None] * bbar_im[:, :, None, :]
    cb_im = cre[:, :, :, None] * bbar_im[:, :, None, :] + cim[:, :, :, None] * bbar_re[:, :, None, :]
    y_op = jnp.concatenate([cb_re, cb_im], axis=1).reshape(g, 2 * n, h * h)
    x_op = jnp.concatenate([pr[:, :chunk], -pi[:, :chunk]], axis=-1)
    taps = _bmm_f32(x_op, y_op).reshape(g, chunk, h, h)
    jj = jnp.arange(chunk)[:, None]
    tt = jnp.arange(chunk)[None, :]
    lag = jnp.clip(tt - jj, 0, chunk - 1)
    wt = jnp.where((tt >= jj)[None, :, :, None, None], taps[:, lag], 0.0)
    wt = wt.transpose(0, 1, 4, 2, 3).reshape(g, chunk * h, chunk * h)
    prr = pr[:, chunk - 1::-1][:, :chunk]
    pir = pi[:, chunk - 1::-1][:, :chunk]
    ws_re = prr[:, :, None, :] * jnp.swapaxes(bbar_re, 1, 2)[:, None] - pir[:, :, None, :] * jnp.swapaxes(bbar_im, 1, 2)[:, None]
    ws_im = prr[:, :, None, :] * jnp.swapaxes(bbar_im, 1, 2)[:, None] + pir[:, :, None, :] * jnp.swapaxes(bbar_re, 1, 2)[:, None]
    ws = jnp.concatenate([ws_re, ws_im], axis=-1).reshape(g, chunk * h, 2 * n)
    p1r = pr[:, 1:chunk + 1]
    p1i = pi[:, 1:chunk + 1]
    wo_re = c_re[:, None] * p1r[:, :, None, :] - c_im[:, None] * p1i[:, :, None, :]
    wo_im = -(c_re[:, None] * p1i[:, :, None, :] + c_im[:, None] * p1r[:, :, None, :])
    wo = jnp.concatenate([wo_re, wo_im], axis=-1)
    wo = wo.transpose(0, 3, 1, 2).reshape(g, 2 * n, chunk * h)
    sr, si = power(chunk * (2 ** jnp.arange(max(nsteps, 1))))
    a1 = jnp.concatenate([sr, sr], axis=-1)
    a2 = jnp.concatenate([-si, si], axis=-1)
    return wt.astype(BF16), ws.astype(BF16), wo.astype(BF16), a1, a2


def _s5_scan(u, ops, bsz, seq):
    wt, ws, wo, a1, a2 = ops
    g, h, chunk = S5_GROUPS, S5_GROUP, S5_CHUNK
    chunks = seq // chunk
    m = bsz * chunks
    nsteps = a1.shape[1] if chunks > 1 else 0
    up = u.reshape(bsz, chunks, chunk, g, h).transpose(3, 0, 1, 2, 4).reshape(g, m, chunk * h).astype(BF16)
    n2 = ws.shape[2]
    ns = a1.shape[1]
    yp = pl.pallas_call(
        functools.partial(_s5_kernel, chunks=chunks, nsteps=nsteps),
        grid=(g,),
        in_specs=[pl.BlockSpec((None, m, chunk * h), lambda i: (i, 0, 0)),
                  pl.BlockSpec((None, chunk * h, chunk * h), lambda i: (i, 0, 0)),
                  pl.BlockSpec((None, chunk * h, n2), lambda i: (i, 0, 0)),
                  pl.BlockSpec((None, n2, chunk * h), lambda i: (i, 0, 0)),
                  pl.BlockSpec((None, ns, n2), lambda i: (i, 0, 0)),
                  pl.BlockSpec((None, ns, n2), lambda i: (i, 0, 0))],
        out_specs=pl.BlockSpec((None, m, chunk * h), lambda i: (i, 0, 0)),
        out_shape=jax.ShapeDtypeStruct((g, m, chunk * h), F32),
        compiler_params=_params(("parallel",)),
        name="s5_scan",
    )(up, wt, ws, wo, a1, a2)
    return yp.reshape(g, bsz, chunks, chunk, h).transpose(1, 2, 3, 0, 4).reshape(bsz * seq, g * h)


def _s5_glu_kernel(y_ref, u_ref, d_ref, w_ref, b_ref, o_ref):
    y = y_ref[...] + d_ref[...] * u_ref[...]
    gl = jax.nn.gelu(y)
    gate = jnp.dot(gl.astype(BF16), w_ref[...], preferred_element_type=F32) + b_ref[...]
    o_ref[...] = (gl * jax.nn.sigmoid(gate)).astype(o_ref.dtype)


def _s5_glu(y, u, d, w_glu, b_glu, tm):
    t, c = y.shape
    return pl.pallas_call(
        _s5_glu_kernel,
        grid=(t // tm,),
        in_specs=[pl.BlockSpec((tm, c), lambda i: (i, 0)),
                  pl.BlockSpec((tm, c), lambda i: (i, 0)),
                  pl.BlockSpec((1, c), lambda i: (0, 0)),
                  pl.BlockSpec((c, c), lambda i: (0, 0)),
                  pl.BlockSpec((1, c), lambda i: (0, 0))],
        out_specs=pl.BlockSpec((tm, c), lambda i: (i, 0)),
        out_shape=jax.ShapeDtypeStruct((t, c), BF16),
        compiler_params=_params(("parallel",)),
        name="s5_glu",
    )(y, u, d.reshape(1, c), w_glu.astype(BF16), b_glu.reshape(1, c))


def _sb_steps(tiles, tri, scale):
    dh = SB_HEAD_DIM
    nk = tiles[0][1].shape[0]
    left = lax.broadcasted_iota(jnp.int32, (nk, 2 * dh), 1) < dh
    pre = []
    stack = []
    for q2, k2, v2, causal, _ in tiles:
        zero = jnp.zeros_like(k2)
        k_bd = jnp.concatenate([jnp.where(left, k2, zero), jnp.where(left, zero, k2)], axis=0)
        z2 = lax.dot_general(q2, k_bd, (((1,), (1,)), ((), ())), preferred_element_type=F32) * scale
        per_head = []
        for hh in range(2):
            z = z2[:, hh * nk:(hh + 1) * nk]
            sp = jnp.maximum(z, 0.0) + jnp.log(1.0 + jnp.exp(-jnp.abs(z)))
            log_keep = -sp if causal is None else jnp.where(causal, -sp, 0.0)
            hi = log_keep.astype(BF16)
            lo = (log_keep - hi.astype(F32)).astype(BF16)
            stack += [hi, lo]
            per_head.append((z - sp, log_keep))
        pre.append(per_head)
    tq = tiles[0][0].shape[0]
    sums = jnp.dot(jnp.concatenate(stack, axis=0), tri, preferred_element_type=F32)
    results = []
    for ti, (q2, k2, v2, causal, r_in) in enumerate(tiles):
        ws, rs = [], []
        for hh in range(2):
            log_beta, log_keep = pre[ti][hh]
            base = (ti * 2 + hh) * 2 * tq
            tail = sums[base:base + tq] + sums[base + tq:base + 2 * tq]
            if r_in is not None:
                tail = tail + r_in[hh]
            w = jnp.exp(log_beta + tail)
            if causal is not None:
                w = jnp.where(causal, w, 0.0)
            ws.append(w.astype(BF16))
            rs.append(jnp.sum(log_keep, axis=1, keepdims=True))
        zero = jnp.zeros_like(v2)
        v_bd = jnp.concatenate([jnp.where(left, v2, zero), jnp.where(left, zero, v2)], axis=0)
        out = jnp.dot(jnp.concatenate(ws, axis=1), v_bd, preferred_element_type=F32)
        results.append((out, rs))
    return results


def _sb_attn_kernel(q_ref, k_ref, v_ref, o_ref, acc_ref, r_ref, tri_ref, *, seq, tq, scale):
    dh = SB_HEAD_DIM
    win = 2 * tq
    row = lax.broadcasted_iota(jnp.int32, (win, win), 0)
    col = lax.broadcasted_iota(jnp.int32, (win, win), 1)
    tri_ref[...] = jnp.where(row > col, 1.0, 0.0).astype(BF16)
    assert ATT_HEADS_PER_STEP == 2

    def q_tiles(p, carry):
        first, where = [], []
        for ti in range(ATT_TILES_PER_ITER):
            i = p * ATT_TILES_PER_ITER + ti
            q0 = pl.multiple_of(i * tq, tq)
            kblk = jnp.maximum(i - 1, 0)
            k0 = pl.multiple_of(kblk * tq, tq)
            q_pos = q0 + lax.broadcasted_iota(jnp.int32, (tq, win), 0)
            causal = (k0 + lax.broadcasted_iota(jnp.int32, (tq, win), 1)) < q_pos
            first.append((q_ref[pl.ds(q0, tq), :], k_ref[pl.ds(k0, win), :], v_ref[pl.ds(k0, win), :], causal, None))
            where.append((q0, kblk))
        rmax = []
        for ti, (out, rs) in enumerate(_sb_steps(first, tri_ref[...], scale)):
            acc_ref[ti] = out
            r_ref[2 * ti] = rs[0]
            r_ref[2 * ti + 1] = rs[1]
            rmax.append(jnp.maximum(jnp.max(rs[0]), jnp.max(rs[1])))
        for ti, (q0, kblk) in enumerate(where):
            def cond(c):
                j, rm = c
                return jnp.logical_and(j >= 0, rm > ATT_LOG_UNDERFLOW)

            def body(c, ti=ti, q0=q0):
                j, _ = c
                kj = pl.multiple_of(j * tq, tq)
                r_old = [r_ref[2 * ti], r_ref[2 * ti + 1]]
                tile = (q_ref[pl.ds(q0, tq), :], k_ref[pl.ds(kj, tq), :], v_ref[pl.ds(kj, tq), :], None, r_old)
                (out, rs), = _sb_steps([tile], tri_ref[:tq, :tq], scale)
                acc_ref[ti] += out
                r_new = [r_old[0] + rs[0], r_old[1] + rs[1]]
                r_ref[2 * ti] = r_new[0]
                r_ref[2 * ti + 1] = r_new[1]
                return j - 1, jnp.maximum(jnp.max(r_new[0]), jnp.max(r_new[1]))

            lax.while_loop(cond, body, (kblk - 1, rmax[ti]))
            o_ref[pl.ds(q0, tq), :] = acc_ref[ti].astype(o_ref.dtype)
        return carry

    lax.fori_loop(0, seq // (tq * ATT_TILES_PER_ITER), q_tiles, 0)


def _sb_attention(qkv, bsz, seq):
    tq = ATT_TQ
    assert seq % (tq * ATT_TILES_PER_ITER) == 0 and seq >= 2 * tq
    wcols = ATT_HEADS_PER_STEP * SB_HEAD_DIM
    nhp = SB_WIDTH // wcols
    kern = functools.partial(_sb_attn_kernel, seq=seq, tq=tq, scale=1.0 / math.sqrt(SB_HEAD_DIM))
    return pl.pallas_call(
        kern,
        grid=(bsz, nhp),
        in_specs=[pl.BlockSpec((None, seq, wcols), lambda b, h: (b, 0, h)),
                  pl.BlockSpec((None, seq, wcols), lambda b, h: (b, 0, nhp + h)),
                  pl.BlockSpec((None, seq, wcols), lambda b, h: (b, 0, 2 * nhp + h))],
        out_specs=pl.BlockSpec((None, seq, wcols), lambda b, h: (b, 0, h)),
        out_shape=jax.ShapeDtypeStruct((bsz, seq, SB_WIDTH), BF16),
        scratch_shapes=[pltpu.VMEM((ATT_TILES_PER_ITER, tq, wcols), F32),
                        pltpu.VMEM((ATT_HEADS_PER_STEP * ATT_TILES_PER_ITER, tq, 1), F32),
                        pltpu.VMEM((2 * tq, 2 * tq), BF16)],
        compiler_params=_params(("parallel", "parallel")),
        name="sb_attention",
    )(qkv, qkv, qkv)


def _pool_in_kernel(x_ref, w_ref, o_ref, halo_ref, *, seq, tm):
    i = pl.program_id(0)
    hmax = POOL_WINDOWS[-1]
    acc = jnp.dot(x_ref[...].astype(BF16), w_ref[...], preferred_element_type=F32)
    t0 = (i * tm) % seq

    @pl.when(t0 == 0)
    def _():
        halo_ref[...] = jnp.zeros_like(halo_ref)

    ext = jnp.concatenate([halo_ref[...], acc], axis=0)
    t_idx = t0 + lax.broadcasted_iota(jnp.int32, (tm, POOL_GROUP_WIDTH), 0)
    for gi, win in enumerate(POOL_WINDOWS):
        cs = slice(gi * POOL_GROUP_WIDTH, (gi + 1) * POOL_GROUP_WIDTH)
        e = ext[:, cs]
        span = 1
        while span < win:
            e = e + pltpu.roll(e, span, 0)
            span *= 2
        wsum = e[hmax:, :]
        cnt = jnp.minimum(t_idx + 1, win).astype(F32)
        o_ref[:, cs] = (wsum / cnt - acc[:, cs]).astype(o_ref.dtype)
    halo_ref[...] = acc[tm - hmax:, :]


def _pool_in(x, w, seq, tm):
    t, k = x.shape
    n = w.shape[1]
    return pl.pallas_call(
        functools.partial(_pool_in_kernel, seq=seq, tm=tm),
        grid=(t // tm,),
        in_specs=[pl.BlockSpec((tm, k), lambda i: (i, 0)),
                  pl.BlockSpec((k, n), lambda i: (0, 0))],
        out_specs=pl.BlockSpec((tm, n), lambda i: (i, 0)),
        out_shape=jax.ShapeDtypeStruct((t, n), BF16),
        scratch_shapes=[pltpu.VMEM((POOL_WINDOWS[-1], n), F32)],
        compiler_params=_params(("arbitrary",)),
        name="pool_in",
    )(x, w)


def _layer_norm(x, g, b):
    mu = jnp.mean(x, axis=-1, keepdims=True)
    xc = x - mu
    var = jnp.mean(xc * xc, axis=-1, keepdims=True)
    return xc * lax.rsqrt(var + LN_EPS) * g + b


def _route(h, rwh_ref, rwl_ref, rb_ref, base_ref, tm):
    h_hi = h.astype(BF16)
    h_lo = (h - h_hi.astype(F32)).astype(BF16)
    rwh = rwh_ref[...]
    logits = (jnp.dot(h_hi, rwh, preferred_element_type=F32)
              + jnp.dot(h_lo, rwh, preferred_element_type=F32)
              + jnp.dot(h_hi, rwl_ref[...], preferred_element_type=F32)) + rb_ref[...]
    lt = logits.T[:N_EXPERTS, :]
    mx = jnp.max(lt, axis=0, keepdims=True)
    ex = jnp.exp(lt - mx)
    probs = ex / jnp.sum(ex, axis=0, keepdims=True)
    p = [probs[e:e + 1, :] for e in range(N_EXPERTS)]

    def first_max(vals):
        best, idx = vals[0], jnp.zeros_like(vals[0], dtype=jnp.int32)
        for n in range(1, len(vals)):
            upd = vals[n] > best
            idx = jnp.where(upd, n, idx)
            best = jnp.where(upd, vals[n], best)
        return best, idx

    scores = []
    for gi in range(N_EXPERT_GROUPS):
        p0, p1, p2, p3 = p[4 * gi:4 * gi + 4]
        a, b = jnp.maximum(p0, p1), jnp.minimum(p0, p1)
        c, d = jnp.maximum(p2, p3), jnp.minimum(p2, p3)
        scores.append(jnp.maximum(a, c) + jnp.maximum(jnp.minimum(a, c), jnp.maximum(b, d)))
    _, g_sel = first_max(scores)
    sel = []
    for n in range(EXPERTS_PER_GROUP):
        v = p[n]
        for gi in range(1, N_EXPERT_GROUPS):
            v = jnp.where(g_sel == gi, p[4 * gi + n], v)
        sel.append(v)
    v1, i1 = first_max(sel)
    v2, i2 = first_max([jnp.where(i1 == n, -1.0, sel[n]) for n in range(EXPERTS_PER_GROUP)])
    den = v1 + v2
    e1 = g_sel * EXPERTS_PER_GROUP + i1
    e2 = g_sel * EXPERTS_PER_GROUP + i2
    e_iota = lax.broadcasted_iota(jnp.int32, (N_EXPERTS, tm), 0)
    oh1 = e_iota == e1
    oh2 = e_iota == e2
    oh = jnp.where(jnp.logical_or(oh1, oh2), 1.0, 0.0)
    r_i = lax.broadcasted_iota(jnp.int32, (tm, tm), 0)
    c_i = lax.broadcasted_iota(jnp.int32, (tm, tm), 1)
    before = jnp.where(r_i < c_i, 1.0, 0.0).astype(BF16)
    cnt = jnp.dot(oh.astype(BF16), before, preferred_element_type=F32) + base_ref[...]
    rank1 = jnp.sum(jnp.where(oh1, cnt, 0.0), axis=0, keepdims=True)
    rank2 = jnp.sum(jnp.where(oh2, cnt, 0.0), axis=0, keepdims=True)
    base_ref[...] += jnp.sum(oh, axis=1, keepdims=True)
    e_out = jnp.concatenate([e1, e2], axis=0)
    w_out = jnp.concatenate([v1 / den, v2 / den], axis=0)
    r_out = jnp.concatenate([rank1, rank2], axis=0).astype(jnp.int32)
    return e_out, w_out, r_out


def _proj_ln_kernel(*refs, n_x, grouped, tm):
    xs = refs[:n_x]
    ws = refs[n_x:2 * n_x]
    pos = 2 * n_x
    if grouped:
        wg_ref, sc_ref = refs[pos:pos + 2]
        pos += 2
    res_ref, g_ref, b_ref, rwh_ref, rwl_ref, rb_ref = refs[pos:pos + 6]
    h_ref, e_ref, gw_ref, rk_ref, cnt_ref, base_ref = refs[pos + 6:pos + 12]

    @pl.when(pl.program_id(0) == 0)
    def _():
        base_ref[...] = jnp.zeros_like(base_ref)

    if grouped:
        x = xs[0][...]
        parts = []
        for gi in range(len(POOL_WINDOWS)):
            cs = slice(gi * POOL_GROUP_WIDTH, (gi + 1) * POOL_GROUP_WIDTH)
            parts.append(jnp.dot(x[:, cs], wg_ref[gi], preferred_element_type=F32))
        y = (jnp.concatenate(parts, axis=1) * sc_ref[...]).astype(BF16)
        mix = jnp.dot(y, ws[0][...], preferred_element_type=F32)
    else:
        mix = jnp.dot(xs[0][...], ws[0][...], preferred_element_type=F32)
        for n in range(1, n_x):
            mix = mix + jnp.dot(xs[n][...], ws[n][...], preferred_element_type=F32)
    h = _layer_norm(ALPHA * res_ref[...] + mix, g_ref[...], b_ref[...])
    h_ref[...] = h
    e_out, w_out, r_out = _route(h, rwh_ref, rwl_ref, rb_ref, base_ref, tm)
    e_ref[...] = e_out
    gw_ref[...] = w_out
    rk_ref[...] = r_out
    cnt_ref[...] = jnp.broadcast_to(base_ref[...], cnt_ref.shape)


def _proj_ln_route(xs, ws, res, ln_g, ln_b, router, tm, grouped=None):
    t, d = res.shape
    rwh, rwl, rb = router
    n_x = len(xs)
    full = lambda a: pl.BlockSpec(a.shape, lambda i, nd=a.ndim: (0,) * nd)
    in_specs = [pl.BlockSpec((tm, x.shape[1]), lambda i: (i, 0)) for x in xs]
    in_specs += [full(w) for w in ws]
    args = list(xs) + list(ws)
    if grouped is not None:
        in_specs += [full(a) for a in grouped]
        args += list(grouped)
    in_specs += [pl.BlockSpec((tm, d), lambda i: (i, 0)), full(ln_g), full(ln_b), full(rwh), full(rwl), full(rb)]
    args += [res, ln_g, ln_b, rwh, rwl, rb]
    pair = lambda dt: jax.ShapeDtypeStruct((2, t), dt)
    out = pl.pallas_call(
        functools.partial(_proj_ln_kernel, n_x=n_x, grouped=grouped is not None, tm=tm),
        grid=(t // tm,),
        in_specs=in_specs,
        out_specs=[pl.BlockSpec((tm, d), lambda i: (i, 0)),
                   pl.BlockSpec((2, tm), lambda i: (0, i)),
                   pl.BlockSpec((2, tm), lambda i: (0, i)),
                   pl.BlockSpec((2, tm), lambda i: (0, i)),
                   pl.BlockSpec((N_EXPERTS, LANES), lambda i: (0, 0))],
        out_shape=[jax.ShapeDtypeStruct((t, d), F32), pair(jnp.int32), pair(F32), pair(jnp.int32),
                   jax.ShapeDtypeStruct((N_EXPERTS, LANES), F32)],
        scratch_shapes=[pltpu.VMEM((N_EXPERTS, 1), F32)],
        compiler_params=_params(("arbitrary",)),
        name="proj_ln_route",
    )(*args)
    h, e, gw, rk, cnt = out
    return h, e, gw, rk, cnt[:, 0]


def _dispatch_kernel(p1_ref, p2_ref, fill_ref, h_ref, xs_ref, zero_ref, sem, *, tm, n_fill):
    base = pl.program_id(0) * tm

    @pl.when(pl.program_id(0) == 0)
    def _():
        zero_ref[...] = jnp.zeros_like(zero_ref)
        n_zero = zero_ref.shape[0]
        fill = lambda k: pltpu.make_async_copy(
            zero_ref, xs_ref.at[pl.ds(pl.multiple_of(jnp.maximum(fill_ref[k], 0), n_zero), n_zero)], sem.at[2])
        for k in range(n_fill):
            pl.when(fill_ref[k] >= 0)(lambda k=k: fill(k).start())
        for k in range(n_fill):
            pl.when(fill_ref[k] >= 0)(lambda k=k: fill(k).wait())

    def issue(r, carry):
        src = h_ref.at[pl.ds(r, 1)]
        pltpu.make_async_copy(src, xs_ref.at[pl.ds(p1_ref[base + r], 1)], sem.at[0]).start()
        pltpu.make_async_copy(src, xs_ref.at[pl.ds(p2_ref[base + r], 1)], sem.at[1]).start()
        return carry

    lax.fori_loop(0, tm, issue, 0)
    pltpu.make_async_copy(h_ref, xs_ref.at[pl.ds(0, tm)], sem.at[0]).wait()
    pltpu.make_async_copy(h_ref, xs_ref.at[pl.ds(0, tm)], sem.at[1]).wait()


def _dispatch(h, pos1, pos2, fill_rows, n_rows, tm, tm_fill):
    t, d = h.shape
    return pl.pallas_call(
        functools.partial(_dispatch_kernel, tm=tm, n_fill=fill_rows.shape[0]),
        grid_spec=pltpu.PrefetchScalarGridSpec(
            num_scalar_prefetch=3,
            grid=(t // tm,),
            in_specs=[pl.BlockSpec((tm, d), lambda i, p1, p2, fr: (i, 0))],
            out_specs=pl.BlockSpec(memory_space=pl.ANY),
            scratch_shapes=[pltpu.VMEM((tm_fill, d), h.dtype), pltpu.SemaphoreType.DMA((3,))],
        ),
        out_shape=jax.ShapeDtypeStruct((n_rows, d), h.dtype),
        compiler_params=_params(("arbitrary",)),
        name="moe_dispatch",
    )(pos1, pos2, fill_rows, h)


def _moe_kernel(te_ref, na_ref, x_ref, wg_ref, wu_ref, wd_ref, o_ref):
    @pl.when(pl.program_id(0) < na_ref[0])
    def _():
        x = x_ref[...].astype(BF16)
        gate = jnp.dot(x, wg_ref[...], preferred_element_type=F32)
        up = jnp.dot(x, wu_ref[...], preferred_element_type=F32)
        hid = (jax.nn.silu(gate) * up).astype(BF16)
        o_ref[...] = jnp.dot(hid, wd_ref[...], preferred_element_type=F32)

    @pl.when(pl.program_id(0) >= na_ref[0])
    def _():
        o_ref[...] = jnp.zeros_like(o_ref)


def _moe_experts(xs, tile_expert, n_active, w_gate, w_up, w_down, layer, tm):
    p, d = xs.shape
    f = w_gate.shape[3]
    row_blk = lambda i, te, na: (jnp.minimum(i, na[0] - 1), 0)
    w_blk = lambda i, te, na: (layer, te[i], 0, 0)
    return pl.pallas_call(
        _moe_kernel,
        grid_spec=pltpu.PrefetchScalarGridSpec(
            num_scalar_prefetch=2,
            grid=(p // tm,),
            in_specs=[pl.BlockSpec((tm, d), row_blk),
                      pl.BlockSpec((None, None, d, f), w_blk),
                      pl.BlockSpec((None, None, d, f), w_blk),
                      pl.BlockSpec((None, None, f, d), w_blk)],
            out_specs=pl.BlockSpec((tm, d), lambda i, te, na: (i, 0)),
        ),
        out_shape=jax.ShapeDtypeStruct((p, d), F32),
        compiler_params=_params(("arbitrary",)),
        name="moe_experts",
    )(tile_expert, n_active, xs, w_gate, w_up, w_down)


def _combine_ln_kernel(p1_ref, p2_ref, h_ref, g1_ref, g2_ref, lg_ref, lb_ref, ys_ref, o_ref,
                       buf1, buf2, sem, *, tm):
    base = pl.program_id(0) * tm

    def issue(r, carry):
        pltpu.make_async_copy(ys_ref.at[pl.ds(p1_ref[base + r], 1)], buf1.at[pl.ds(r, 1)], sem.at[0]).start()
        pltpu.make_async_copy(ys_ref.at[pl.ds(p2_ref[base + r], 1)], buf2.at[pl.ds(r, 1)], sem.at[1]).start()
        return carry

    lax.fori_loop(0, tm, issue, 0)
    pltpu.make_async_copy(ys_ref.at[pl.ds(0, tm)], buf1, sem.at[0]).wait()
    pltpu.make_async_copy(ys_ref.at[pl.ds(0, tm)], buf2, sem.at[1]).wait()
    ffn = g1_ref[...] * buf1[...] + g2_ref[...] * buf2[...]
    o_ref[...] = _layer_norm(ALPHA * h_ref[...] + ffn, lg_ref[...], lb_ref[...])


def _combine_ln(h, ys, pos1, pos2, g1, g2, ln_g, ln_b, tm):
    t, d = h.shape
    row = lambda i, p1, p2: (i, 0)
    const = lambda i, p1, p2: (0, 0)
    return pl.pallas_call(
        functools.partial(_combine_ln_kernel, tm=tm),
        grid_spec=pltpu.PrefetchScalarGridSpec(
            num_scalar_prefetch=2,
            grid=(t // tm,),
            in_specs=[pl.BlockSpec((tm, d), row),
                      pl.BlockSpec((tm, 1), row),
                      pl.BlockSpec((tm, 1), row),
                      pl.BlockSpec((1, d), const),
                      pl.BlockSpec((1, d), const),
                      pl.BlockSpec(memory_space=pl.ANY)],
            out_specs=pl.BlockSpec((tm, d), row),
            scratch_shapes=[pltpu.VMEM((tm, d), F32), pltpu.VMEM((tm, d), F32),
                            pltpu.SemaphoreType.DMA((2,))],
        ),
        out_shape=jax.ShapeDtypeStruct((t, d), F32),
        compiler_params=_params(("arbitrary",)),
        name="moe_combine_ln",
    )(pos1, pos2, h, g1, g2, ln_g, ln_b, ys)


def _moe_block(h, e, gw, rk, counts, w_gate, w_up, w_down, layer, ln_g, ln_b, tm_tok):
    t, d = h.shape
    tm = MOE_TM
    n_rows = 2 * t + N_EXPERTS * tm
    n_tiles = n_rows // tm
    cnt = counts.astype(jnp.int32)
    padded = ((cnt + tm - 1) // tm) * tm
    ends = jnp.cumsum(padded)
    offs = ends - padded
    e_ids = jnp.arange(N_EXPERTS, dtype=jnp.int32)
    pos = jnp.sum(jnp.where(e[..., None] == e_ids, offs, 0), axis=-1) + rk
    n_active = jnp.maximum(ends[-1] // tm, 1)
    tile_start = jnp.minimum(jnp.arange(n_tiles, dtype=jnp.int32), n_active - 1) * tm
    tile_expert = jnp.minimum(jnp.sum((ends[None, :] <= tile_start[:, None]).astype(jnp.int32), axis=1),
                              N_EXPERTS - 1)
    last_tile = jnp.where(padded > 0, ends - tm, -1)
    tail = ends[-1] + jnp.arange(N_EXPERTS, dtype=jnp.int32) * tm
    fill_rows = jnp.concatenate([last_tile, jnp.where(tail < n_rows, tail, -1)]).astype(jnp.int32)
    xs = _dispatch(h, pos[0], pos[1], fill_rows, n_rows, tm_tok, tm)
    ys = _moe_experts(xs, tile_expert, n_active.reshape(1).astype(jnp.int32), w_gate, w_up, w_down, layer, tm)
    return _combine_ln(h, ys, pos[0], pos[1], gw[0].reshape(t, 1), gw[1].reshape(t, 1), ln_g, ln_b, tm_tok)


def kernel(x, ab_w_in, ab_lambda_re, ab_lambda_im, ab_log_dt, ab_b_re, ab_b_im, ab_c_re, ab_c_im, ab_d,
           ab_w_glu, ab_b_glu, ab_w_out, c_w_in, c_w_group, c_scale, c_w_out, ln_g, ln_b, router_w,
           router_b, moe_w_gate, moe_w_up, moe_w_down):
    bsz, seq, d = x.shape
    t = bsz * seq
    tm = min(ROW_TILE, t)
    assert t % tm == 0 and seq % tm == 0 and seq % S5_CHUNK == 0
    row = lambda a: a.reshape(1, -1)

    rw = jnp.zeros((d, LANES), F32).at[:, :N_EXPERTS].set(router_w)
    rw_hi = rw.astype(BF16)
    rw_lo = (rw - rw_hi.astype(F32)).astype(BF16)
    rb = jnp.zeros((1, LANES), F32).at[0, :N_EXPERTS].set(router_b)
    router = (rw_hi, rw_lo, rb)

    h = x.reshape(t, d)

    wg_b, wu_b, wd_b = moe_w_gate.astype(BF16), moe_w_up.astype(BF16), moe_w_down.astype(BF16)
    u, qkv = _in_proj_ab(h, ab_w_in[0].astype(BF16), min(2 * ROW_TILE, t))
    chunks = seq // S5_CHUNK
    nsteps = max(int(math.log2(chunks)), 0)
    assert (1 << nsteps) == chunks
    ops = _s5_operands(ab_lambda_re[0], ab_lambda_im[0], ab_log_dt[0], ab_b_re[0], ab_b_im[0],
                       ab_c_re[0], ab_c_im[0], S5_CHUNK, nsteps)
    y_ssm = _s5_scan(u, ops, bsz, seq)
    y_a = _s5_glu(y_ssm, u, ab_d[0], ab_w_glu[0], ab_b_glu[0], tm)
    y_b = _sb_attention(qkv.reshape(bsz, seq, 3 * SB_WIDTH), bsz, seq).reshape(t, SB_WIDTH)
    w_out = ab_w_out[0].astype(BF16)
    h, e, gw, rk, cnt = _proj_ln_route([y_a, y_b], [w_out[:S5_WIDTH], w_out[S5_WIDTH:]], h,
                                       row(ln_g[0, 0]), row(ln_b[0, 0]), router, tm)
    h = _moe_block(h, e, gw, rk, cnt, wg_b, wu_b, wd_b, 0, row(ln_g[0, 1]), row(ln_b[0, 1]), tm)

    pooled = _pool_in(h, c_w_in[0].astype(BF16), seq, tm)
    h, e, gw, rk, cnt = _proj_ln_route([pooled], [c_w_out[0].astype(BF16)], h,
                                       row(ln_g[1, 0]), row(ln_b[1, 0]), router, tm,
                                       grouped=(c_w_group[0].astype(BF16), row(c_scale[0])))
    h = _moe_block(h, e, gw, rk, cnt, wg_b, wu_b, wd_b, 1, row(ln_g[1, 1]), row(ln_b[1, 1]), tm)
    return h.reshape(bsz, seq, d)
```

```python
import functools
import math

import jax
import jax.numpy as jnp
from jax import lax
from jax.experimental import pallas as pl
from jax.experimental.pallas import tpu as pltpu

S5_WIDTH = 512
S5_GROUP = 16
S5_GROUPS = 32
S5_STATE = 64
SB_HEAD_DIM = 128
SB_HEADS = 12
SB_WIDTH = SB_HEADS * SB_HEAD_DIM
POOL_WINDOWS = (2, 4, 8, 16)
POOL_GROUP_WIDTH = 512
N_EXPERTS = 16
N_EXPERT_GROUPS = 4
EXPERTS_PER_GROUP = 4
DEPTH = 2
ALPHA = (2.0 * DEPTH) ** 0.25
LN_EPS = 1e-5

LANES = 128
VMEM_LIMIT = 56 * 1024 * 1024

S5_CHUNK = 32
ATT_TQ = 128
ATT_HEADS_PER_STEP = 2
ATT_TILES_PER_ITER = 2
ATT_LOG_UNDERFLOW = -88.0
MOE_TM = 256
MOE_W_GROUPS = 8
ROW_TILE = 512

BF16 = jnp.bfloat16
F32 = jnp.float32


def _params(sem, vmem=VMEM_LIMIT):
    return pltpu.CompilerParams(dimension_semantics=sem, vmem_limit_bytes=vmem)


def _in_proj_ab_kernel(x_ref, w_ref, u_ref, qkv_ref, xb_ref):
    j = pl.program_id(1)

    @pl.when(j == 0)
    def _():
        xb_ref[...] = x_ref[...].astype(BF16)

    acc = jnp.dot(xb_ref[...], w_ref[...], preferred_element_type=F32)

    @pl.when(j == 0)
    def _():
        u_ref[...] = acc

    @pl.when(j > 0)
    def _():
        qkv_ref[...] = acc.astype(qkv_ref.dtype)


def _in_proj_ab(x, w, tm):
    m, k = x.shape
    tn = S5_WIDTH
    n_blocks = w.shape[1] // tn
    return pl.pallas_call(
        _in_proj_ab_kernel,
        grid=(m // tm, n_blocks),
        in_specs=[pl.BlockSpec((tm, k), lambda i, j: (i, 0)),
                  pl.BlockSpec((k, tn), lambda i, j: (0, j))],
        out_specs=[pl.BlockSpec((tm, tn), lambda i, j: (i, 0)),
                   pl.BlockSpec((tm, tn), lambda i, j: (i, jnp.maximum(j - 1, 0)))],
        out_shape=[jax.ShapeDtypeStruct((m, tn), F32),
                   jax.ShapeDtypeStruct((m, w.shape[1] - tn), BF16)],
        scratch_shapes=[pltpu.VMEM((tm, k), BF16)],
        compiler_params=_params(("parallel", "arbitrary")),
        name="in_proj_ab",
    )(x, w)


def _bmm_f32_kernel(x_ref, y_ref, o_ref):
    o_ref[...] = jnp.dot(x_ref[...], y_ref[...], preferred_element_type=F32,
                         precision=lax.Precision.HIGHEST)


def _bmm_f32(x, y):
    g, m, k = x.shape
    n = y.shape[2]
    return pl.pallas_call(
        _bmm_f32_kernel,
        grid=(g,),
        in_specs=[pl.BlockSpec((None, m, k), lambda i: (i, 0, 0)),
                  pl.BlockSpec((None, k, n), lambda i: (i, 0, 0))],
        out_specs=pl.BlockSpec((None, m, n), lambda i: (i, 0, 0)),
        out_shape=jax.ShapeDtypeStruct((g, m, n), F32),
        compiler_params=_params(("parallel",)),
        name="s5_taps",
    )(x, y)


def _s5_kernel(u_ref, wt_ref, ws_ref, wo_ref, a1_ref, a2_ref, o_ref, *, chunks, nsteps):
    u = u_ref[...]
    y_intra = jnp.dot(u, wt_ref[...], preferred_element_type=F32)
    s = jnp.dot(u, ws_ref[...], preferred_element_type=F32)
    m = s.shape[0]
    c_idx = lax.broadcasted_iota(jnp.int32, s.shape, 0) % chunks
    half = s.shape[1] // 2
    for k in range(nsteps):
        d = 1 << k
        sh = jnp.where(c_idx >= d, pltpu.roll(s, d, 0), 0.0)
        s = s + a1_ref[k:k + 1, :] * sh + a2_ref[k:k + 1, :] * pltpu.roll(sh, half, 1)
    s_prev = jnp.where(c_idx >= 1, pltpu.roll(s, 1, 0), 0.0)
    y_inter = jnp.dot(s_prev.astype(BF16), wo_ref[...], preferred_element_type=F32)
    o_ref[...] = y_intra + y_inter


def _s5_operands(lam_re, lam_im, log_dt, b_re, b_im, c_re, c_im, chunk, nsteps):
    g, n = lam_re.shape
    h = S5_GROUP
    dt = jnp.exp(log_dt)[:, None]
    lr, li = lam_re, lam_im
    mag = jnp.exp(lr * dt)
    lbr = mag * jnp.cos(li * dt)
    lbi = mag * jnp.sin(li * dt)
    den = lr * lr + li * li
    coef_re = ((lbr - 1.0) * lr + lbi * li) / den
    coef_im = (lbi * lr - (lbr - 1.0) * li) / den
    bbar_re = coef_re[..., None] * b_re - coef_im[..., None] * b_im
    bbar_im = coef_re[..., None] * b_im + coef_im[..., None] * b_re

    def power(mult):
        mult = mult.astype(F32)[None, :, None]
        pm = jnp.exp(lr[:, None, :] * dt[:, None, :] * mult)
        ang = li[:, None, :] * dt[:, None, :] * mult
        return pm * jnp.cos(ang), pm * jnp.sin(ang)

    pr, pi = power(jnp.arange(chunk + 1))
    cre = jnp.swapaxes(c_re, 1, 2)
    cim = jnp.swapaxes(c_im, 1, 2)
    cb_re = cre[:, :, :, None] * bbar_re[:, :, None, :] - cim[:, :, :, None] * bbar_im[:, :, None, :]
    cb_im = cre[:, :, :, None] * bbar_im[:, :, None, :] + cim[:, :, :, None] * bbar_re[:, :, None, :]
    y_op = jnp.concatenate([cb_re, cb_im], axis=1).reshape(g, 2 * n, h * h)
    x_op = jnp.concatenate([pr[:, :chunk], -pi[:, :chunk]], axis=-1)
    taps = _bmm_f32(x_op, y_op).reshape(g, chunk, h, h)
    jj = jnp.arange(chunk)[:, None]
    tt = jnp.arange(chunk)[None, :]
    lag = jnp.clip(tt - jj, 0, chunk - 1)
    wt = jnp.where((tt >= jj)[None, :, :, None, None], taps[:, lag], 0.0)
    wt = wt.transpose(0, 1, 4, 2, 3).reshape(g, chunk * h, chunk * h)
    prr = pr[:, chunk - 1::-1][:, :chunk]
    pir = pi[:, chunk - 1::-1][:, :chunk]
    ws_re = prr[:, :, None, :] * jnp.swapaxes(bbar_re, 1, 2)[:, None] - pir[:, :, None, :] * jnp.swapaxes(bbar_im, 1, 2)[:, None]
    ws_im = prr[:, :, None, :] * jnp.swapaxes(bbar_im, 1, 2)[:, None] + pir[:, :, None, :] * jnp.swapaxes(bbar_re, 1, 2)[:, None]
    ws = jnp.concatenate([ws_re, ws_im], axis=-1).reshape(g, chunk * h, 2 * n)
    p1r = pr[:, 1:chunk + 1]
    p1i = pi[:, 1:chunk + 1]
    wo_re = c_re[:, None] * p1r[:, :, None, :] - c_im[:, None] * p1i[:, :, None, :]
    wo_im = -(c_re[:, None] * p1i[:, :, None, :] + c_im[:, None] * p1r[:, :, None, :])
    wo = jnp.concatenate([wo_re, wo_im], axis=-1)
    wo = wo.transpose(0, 3, 1, 2).reshape(g, 2 * n, chunk * h)
    sr, si = power(chunk * (2 ** jnp.arange(max(nsteps, 1))))
    a1 = jnp.concatenate([sr, sr], axis=-1)
    a2 = jnp.concatenate([-si, si], axis=-1)
    return wt.astype(BF16), ws.astype(BF16), wo.astype(BF16), a1, a2


def _s5_scan(u, ops, bsz, seq):
    wt, ws, wo, a1, a2 = ops
    g, h, chunk = S5_GROUPS, S5_GROUP, S5_CHUNK
    chunks = seq // chunk
    m = bsz * chunks
    nsteps = a1.shape[1] if chunks > 1 else 0
    up = u.reshape(bsz, chunks, chunk, g, h).transpose(3, 0, 1, 2, 4).reshape(g, m, chunk * h).astype(BF16)
    n2 = ws.shape[2]
    ns = a1.shape[1]
    yp = pl.pallas_call(
        functools.partial(_s5_kernel, chunks=chunks, nsteps=nsteps),
        grid=(g,),
        in_specs=[pl.BlockSpec((None, m, chunk * h), lambda i: (i, 0, 0)),
                  pl.BlockSpec((None, chunk * h, chunk * h), lambda i: (i, 0, 0)),
                  pl.BlockSpec((None, chunk * h, n2), lambda i: (i, 0, 0)),
                  pl.BlockSpec((None, n2, chunk * h), lambda i: (i, 0, 0)),
                  pl.BlockSpec((None, ns, n2), lambda i: (i, 0, 0)),
                  pl.BlockSpec((None, ns, n2), lambda i: (i, 0, 0))],
        out_specs=pl.BlockSpec((None, m, chunk * h), lambda i: (i, 0, 0)),
        out_shape=jax.ShapeDtypeStruct((g, m, chunk * h), F32),
        compiler_params=_params(("parallel",)),
        name="s5_scan",
    )(up, wt, ws, wo, a1, a2)
    return yp.reshape(g, bsz, chunks, chunk, h).transpose(1, 2, 3, 0, 4).reshape(bsz * seq, g * h)


def _s5_glu_kernel(y_ref, u_ref, d_ref, w_ref, b_ref, o_ref):
    y = y_ref[...] + d_ref[...] * u_ref[...]
    gl = jax.nn.gelu(y)
    gate = jnp.dot(gl.astype(BF16), w_ref[...], preferred_element_type=F32) + b_ref[...]
    o_ref[...] = (gl * jax.nn.sigmoid(gate)).astype(o_ref.dtype)


def _s5_glu(y, u, d, w_glu, b_glu, tm):
    t, c = y.shape
    return pl.pallas_call(
        _s5_glu_kernel,
        grid=(t // tm,),
        in_specs=[pl.BlockSpec((tm, c), lambda i: (i, 0)),
                  pl.BlockSpec((tm, c), lambda i: (i, 0)),
                  pl.BlockSpec((1, c), lambda i: (0, 0)),
                  pl.BlockSpec((c, c), lambda i: (0, 0)),
                  pl.BlockSpec((1, c), lambda i: (0, 0))],
        out_specs=pl.BlockSpec((tm, c), lambda i: (i, 0)),
        out_shape=jax.ShapeDtypeStruct((t, c), BF16),
        compiler_params=_params(("parallel",)),
        name="s5_glu",
    )(y, u, d.reshape(1, c), w_glu.astype(BF16), b_glu.reshape(1, c))


def _sb_steps(tiles, tri, scale):
    dh = SB_HEAD_DIM
    nk = tiles[0][1].shape[0]
    left = lax.broadcasted_iota(jnp.int32, (nk, 2 * dh), 1) < dh
    pre = []
    stack = []
    for q2, k2, v2, causal, _ in tiles:
        zero = jnp.zeros_like(k2)
        k_bd = jnp.concatenate([jnp.where(left, k2, zero), jnp.where(left, zero, k2)], axis=0)
        z2 = lax.dot_general(q2, k_bd, (((1,), (1,)), ((), ())), preferred_element_type=F32) * scale
        per_head = []
        for hh in range(2):
            z = z2[:, hh * nk:(hh + 1) * nk]
            sp = jnp.maximum(z, 0.0) + jnp.log(1.0 + jnp.exp(-jnp.abs(z)))
            log_keep = -sp if causal is None else jnp.where(causal, -sp, 0.0)
            hi = log_keep.astype(BF16)
            lo = (log_keep - hi.astype(F32)).astype(BF16)
            stack += [hi, lo]
            per_head.append((z - sp, log_keep))
        pre.append(per_head)
    tq = tiles[0][0].shape[0]
    sums = jnp.dot(jnp.concatenate(stack, axis=0), tri, preferred_element_type=F32)
    results = []
    for ti, (q2, k2, v2, causal, r_in) in enumerate(tiles):
        ws, rs = [], []
        for hh in range(2):
            log_beta, log_keep = pre[ti][hh]
            base = (ti * 2 + hh) * 2 * tq
            tail = sums[base:base + tq] + sums[base + tq:base + 2 * tq]
            if r_in is not None:
                tail = tail + r_in[hh]
            w = jnp.exp(log_beta + tail)
            if causal is not None:
                w = jnp.where(causal, w, 0.0)
            ws.append(w.astype(BF16))
            rs.append(jnp.sum(log_keep, axis=1, keepdims=True))
        zero = jnp.zeros_like(v2)
        v_bd = jnp.concatenate([jnp.where(left, v2, zero), jnp.where(left, zero, v2)], axis=0)
        out = jnp.dot(jnp.concatenate(ws, axis=1), v_bd, preferred_element_type=F32)
        results.append((out, rs))
    return results


def _sb_attn_kernel(q_ref, k_ref, v_ref, o_ref, acc_ref, r_ref, tri_ref, *, seq, tq, scale):
    dh = SB_HEAD_DIM
    win = 2 * tq
    row = lax.broadcasted_iota(jnp.int32, (win, win), 0)
    col = lax.broadcasted_iota(jnp.int32, (win, win), 1)
    tri_ref[...] = jnp.where(row > col, 1.0, 0.0).astype(BF16)
    assert ATT_HEADS_PER_STEP == 2

    def q_tiles(p, carry):
        first, where = [], []
        for ti in range(ATT_TILES_PER_ITER):
            i = p * ATT_TILES_PER_ITER + ti
            q0 = pl.multiple_of(i * tq, tq)
            kblk = jnp.maximum(i - 1, 0)
            k0 = pl.multiple_of(kblk * tq, tq)
            q_pos = q0 + lax.broadcasted_iota(jnp.int32, (tq, win), 0)
            causal = (k0 + lax.broadcasted_iota(jnp.int32, (tq, win), 1)) < q_pos
            first.append((q_ref[pl.ds(q0, tq), :], k_ref[pl.ds(k0, win), :], v_ref[pl.ds(k0, win), :], causal, None))
            where.append((q0, kblk))
        rmax = []
        for ti, (out, rs) in enumerate(_sb_steps(first, tri_ref[...], scale)):
            acc_ref[ti] = out
            r_ref[2 * ti] = rs[0]
            r_ref[2 * ti + 1] = rs[1]
            rmax.append(jnp.maximum(jnp.max(rs[0]), jnp.max(rs[1])))
        for ti, (q0, kblk) in enumerate(where):
            def cond(c):
                j, rm = c
                return jnp.logical_and(j >= 0, rm > ATT_LOG_UNDERFLOW)

            def body(c, ti=ti, q0=q0):
                j, _ = c
                kj = pl.multiple_of(j * tq, tq)
                r_old = [r_ref[2 * ti], r_ref[2 * ti + 1]]
                tile = (q_ref[pl.ds(q0, tq), :], k_ref[pl.ds(kj, tq), :], v_ref[pl.ds(kj, tq), :], None, r_old)
                (out, rs), = _sb_steps([tile], tri_ref[:tq, :tq], scale)
                acc_ref[ti] += out
                r_new = [r_old[0] + rs[0], r_old[1] + rs[1]]
                r_ref[2 * ti] = r_new[0]
                r_ref[2 * ti + 1] = r_new[1]
                return j - 1, jnp.maximum(jnp.max(r_new[0]), jnp.max(r_new[1]))

            lax.while_loop(cond, body, (kblk - 1, rmax[ti]))
            o_ref[pl.ds(q0, tq), :] = acc_ref[ti].astype(o_ref.dtype)
        return carry

    lax.fori_loop(0, seq // (tq * ATT_TILES_PER_ITER), q_tiles, 0)


def _sb_attention(qkv, bsz, seq):
    tq = ATT_TQ
    assert seq % (tq * ATT_TILES_PER_ITER) == 0 and seq >= 2 * tq
    wcols = ATT_HEADS_PER_STEP * SB_HEAD_DIM
    nhp = SB_WIDTH // wcols
    kern = functools.partial(_sb_attn_kernel, seq=seq, tq=tq, scale=1.0 / math.sqrt(SB_HEAD_DIM))
    return pl.pallas_call(
        kern,
        grid=(bsz, nhp),
        in_specs=[pl.BlockSpec((None, seq, wcols), lambda b, h: (b, 0, h)),
                  pl.BlockSpec((None, seq, wcols), lambda b, h: (b, 0, nhp + h)),
                  pl.BlockSpec((None, seq, wcols), lambda b, h: (b, 0, 2 * nhp + h))],
        out_specs=pl.BlockSpec((None, seq, wcols), lambda b, h: (b, 0, h)),
        out_shape=jax.ShapeDtypeStruct((bsz, seq, SB_WIDTH), BF16),
        scratch_shapes=[pltpu.VMEM((ATT_TILES_PER_ITER, tq, wcols), F32),
                        pltpu.VMEM((ATT_HEADS_PER_STEP * ATT_TILES_PER_ITER, tq, 1), F32),
                        pltpu.VMEM((2 * tq, 2 * tq), BF16)],
        compiler_params=_params(("parallel", "parallel")),
        name="sb_attention",
    )(qkv, qkv, qkv)


def _pool_in_kernel(x_ref, w_ref, o_ref, halo_ref, *, seq, tm):
    i = pl.program_id(0)
    hmax = POOL_WINDOWS[-1]
    acc = jnp.dot(x_ref[...].astype(BF16), w_ref[...], preferred_element_type=F32)
    t0 = (i * tm) % seq

    @pl.when(t0 == 0)
    def _():
        halo_ref[...] = jnp.zeros_like(halo_ref)

    ext = jnp.concatenate([halo_ref[...], acc], axis=0)
    t_idx = t0 + lax.broadcasted_iota(jnp.int32, (tm, POOL_GROUP_WIDTH), 0)
    for gi, win in enumerate(POOL_WINDOWS):
        cs = slice(gi * POOL_GROUP_WIDTH, (gi + 1) * POOL_GROUP_WIDTH)
        e = ext[:, cs]
        span = 1
        while span < win:
            e = e + pltpu.roll(e, span, 0)
            span *= 2
        wsum = e[hmax:, :]
        cnt = jnp.minimum(t_idx + 1, win).astype(F32)
        o_ref[:, cs] = (wsum / cnt - acc[:, cs]).astype(o_ref.dtype)
    halo_ref[...] = acc[tm - hmax:, :]


def _pool_in(x, w, seq, tm):
    t, k = x.shape
    n = w.shape[1]
    return pl.pallas_call(
        functools.partial(_pool_in_kernel, seq=seq, tm=tm),
        grid=(t // tm,),
        in_specs=[pl.BlockSpec((tm, k), lambda i: (i, 0)),
                  pl.BlockSpec((k, n), lambda i: (0, 0))],
        out_specs=pl.BlockSpec((tm, n), lambda i: (i, 0)),
        out_shape=jax.ShapeDtypeStruct((t, n), BF16),
        scratch_shapes=[pltpu.VMEM((POOL_WINDOWS[-1], n), F32)],
        compiler_params=_params(("arbitrary",)),
        name="pool_in",
    )(x, w)


def _layer_norm(x, g, b):
    mu = jnp.mean(x, axis=-1, keepdims=True)
    xc = x - mu
    var = jnp.mean(xc * xc, axis=-1, keepdims=True)
    return xc * lax.rsqrt(var + LN_EPS) * g + b


def _route(h, rwh_ref, rwl_ref, rb_ref, base_ref, tm):
    h_hi = h.astype(BF16)
    h_lo = (h - h_hi.astype(F32)).astype(BF16)
    rwh = rwh_ref[...]
    logits = (jnp.dot(h_hi, rwh, preferred_element_type=F32)
              + jnp.dot(h_lo, rwh, preferred_element_type=F32)
              + jnp.dot(h_hi, rwl_ref[...], preferred_element_type=F32)) + rb_ref[...]
    lt = logits.T[:N_EXPERTS, :]
    mx = jnp.max(lt, axis=0, keepdims=True)
    ex = jnp.exp(lt - mx)
    probs = ex / jnp.sum(ex, axis=0, keepdims=True)
    p = [probs[e:e + 1, :] for e in range(N_EXPERTS)]

    def first_max(vals):
        best, idx = vals[0], jnp.zeros_like(vals[0], dtype=jnp.int32)
        for n in range(1, len(vals)):
            upd = vals[n] > best
            idx = jnp.where(upd, n, idx)
            best = jnp.where(upd, vals[n], best)
        return best, idx

    scores = []
    for gi in range(N_EXPERT_GROUPS):
        p0, p1, p2, p3 = p[4 * gi:4 * gi + 4]
        a, b = jnp.maximum(p0, p1), jnp.minimum(p0, p1)
        c, d = jnp.maximum(p2, p3), jnp.minimum(p2, p3)
        scores.append(jnp.maximum(a, c) + jnp.maximum(jnp.minimum(a, c), jnp.maximum(b, d)))
    _, g_sel = first_max(scores)
    sel = []
    for n in range(EXPERTS_PER_GROUP):
        v = p[n]
        for gi in range(1, N_EXPERT_GROUPS):
            v = jnp.where(g_sel == gi, p[4 * gi + n], v)
        sel.append(v)
    v1, i1 = first_max(sel)
    v2, i2 = first_max([jnp.where(i1 == n, -1.0, sel[n]) for n in range(EXPERTS_PER_GROUP)])
    den = v1 + v2
    e1 = g_sel * EXPERTS_PER_GROUP + i1
    e2 = g_sel * EXPERTS_PER_GROUP + i2
    e_iota = lax.broadcasted_iota(jnp.int32, (N_EXPERTS, tm), 0)
    oh1 = e_iota == e1
    oh2 = e_iota == e2
    oh = jnp.where(jnp.logical_or(oh1, oh2), 1.0, 0.0)
    r_i = lax.broadcasted_iota(jnp.int32, (tm, tm), 0)
    c_i = lax.broadcasted_iota(jnp.int32, (tm, tm), 1)
    before = jnp.where(r_i < c_i, 1.0, 0.0).astype(BF16)
    cnt = jnp.dot(oh.astype(BF16), before, preferred_element_type=F32) + base_ref[...]
    rank1 = jnp.sum(jnp.where(oh1, cnt, 0.0), axis=0, keepdims=True)
    rank2 = jnp.sum(jnp.where(oh2, cnt, 0.0), axis=0, keepdims=True)
    base_ref[...] += jnp.sum(oh, axis=1, keepdims=True)
    e_out = jnp.concatenate([e1, e2], axis=0)
    w_out = jnp.concatenate([v1 / den, v2 / den], axis=0)
    r_out = jnp.concatenate([rank1, rank2], axis=0).astype(jnp.int32)
    return e_out, w_out, r_out


def _proj_ln_kernel(*refs, n_x, grouped, tm):
    xs = refs[:n_x]
    ws = refs[n_x:2 * n_x]
    pos = 2 * n_x
    if grouped:
        wg_ref, sc_ref = refs[pos:pos + 2]
        pos += 2
    res_ref, g_ref, b_ref, rwh_ref, rwl_ref, rb_ref = refs[pos:pos + 6]
    h_ref, e_ref, gw_ref, rk_ref, cnt_ref, base_ref = refs[pos + 6:pos + 12]

    @pl.when(pl.program_id(0) == 0)
    def _():
        base_ref[...] = jnp.zeros_like(base_ref)

    if grouped:
        x = xs[0][...]
        parts = []
        for gi in range(len(POOL_WINDOWS)):
            cs = slice(gi * POOL_GROUP_WIDTH, (gi + 1) * POOL_GROUP_WIDTH)
            parts.append(jnp.dot(x[:, cs], wg_ref[gi], preferred_element_type=F32))
        y = (jnp.concatenate(parts, axis=1) * sc_ref[...]).astype(BF16)
        mix = jnp.dot(y, ws[0][...], preferred_element_type=F32)
    else:
        mix = jnp.dot(xs[0][...], ws[0][...], preferred_element_type=F32)
        for n in range(1, n_x):
            mix = mix + jnp.dot(xs[n][...], ws[n][...], preferred_element_type=F32)
    h = _layer_norm(ALPHA * res_ref[...] + mix, g_ref[...], b_ref[...])
    h_ref[...] = h
    e_out, w_out, r_out = _route(h, rwh_ref, rwl_ref, rb_ref, base_ref, tm)
    e_ref[...] = e_out
    gw_ref[...] = w_out
    rk_ref[...] = r_out
    cnt_ref[...] = jnp.broadcast_to(base_ref[...], cnt_ref.shape)


def _proj_ln_route(xs, ws, res, ln_g, ln_b, router, tm, grouped=None):
    t, d = res.shape
    rwh, rwl, rb = router
    n_x = len(xs)
    full = lambda a: pl.BlockSpec(a.shape, lambda i, nd=a.ndim: (0,) * nd)
    in_specs = [pl.BlockSpec((tm, x.shape[1]), lambda i: (i, 0)) for x in xs]
    in_specs += [full(w) for w in ws]
    args = list(xs) + list(ws)
    if grouped is not None:
        in_specs += [full(a) for a in grouped]
        args += list(grouped)
    in_specs += [pl.BlockSpec((tm, d), lambda i: (i, 0)), full(ln_g), full(ln_b), full(rwh), full(rwl), full(rb)]
    args += [res, ln_g, ln_b, rwh, rwl, rb]
    pair = lambda dt: jax.ShapeDtypeStruct((2, t), dt)
    out = pl.pallas_call(
        functools.partial(_proj_ln_kernel, n_x=n_x, grouped=grouped is not None, tm=tm),
        grid=(t // tm,),
        in_specs=in_specs,
        out_specs=[pl.BlockSpec((tm, d), lambda i: (i, 0)),
                   pl.BlockSpec((2, tm), lambda i: (0, i)),
                   pl.BlockSpec((2, tm), lambda i: (0, i)),
                   pl.BlockSpec((2, tm), lambda i: (0, i)),
                   pl.BlockSpec((N_EXPERTS, LANES), lambda i: (0, 0))],
        out_shape=[jax.ShapeDtypeStruct((t, d), F32), pair(jnp.int32), pair(F32), pair(jnp.int32),
                   jax.ShapeDtypeStruct((N_EXPERTS, LANES), F32)],
        scratch_shapes=[pltpu.VMEM((N_EXPERTS, 1), F32)],
        compiler_params=_params(("arbitrary",)),
        name="proj_ln_route",
    )(*args)
    h, e, gw, rk, cnt = out
    return h, e, gw, rk, cnt[:, 0]


def _dispatch_kernel(p1_ref, p2_ref, fill_ref, h_ref, xs_ref, zero_ref, sem, *, tm, n_fill):
    base = pl.program_id(0) * tm

    @pl.when(pl.program_id(0) == 0)
    def _():
        zero_ref[...] = jnp.zeros_like(zero_ref)
        n_zero = zero_ref.shape[0]
        fill = lambda k: pltpu.make_async_copy(
            zero_ref, xs_ref.at[pl.ds(pl.multiple_of(jnp.maximum(fill_ref[k], 0), n_zero), n_zero)], sem.at[2])
        for k in range(n_fill):
            pl.when(fill_ref[k] >= 0)(lambda k=k: fill(k).start())
        for k in range(n_fill):
            pl.when(fill_ref[k] >= 0)(lambda k=k: fill(k).wait())

    def issue(r, carry):
        src = h_ref.at[pl.ds(r, 1)]
        pltpu.make_async_copy(src, xs_ref.at[pl.ds(p1_ref[base + r], 1)], sem.at[0]).start()
        pltpu.make_async_copy(src, xs_ref.at[pl.ds(p2_ref[base + r], 1)], sem.at[1]).start()
        return carry

    lax.fori_loop(0, tm, issue, 0)
    pltpu.make_async_copy(h_ref, xs_ref.at[pl.ds(0, tm)], sem.at[0]).wait()
    pltpu.make_async_copy(h_ref, xs_ref.at[pl.ds(0, tm)], sem.at[1]).wait()


def _dispatch(h, pos1, pos2, fill_rows, n_rows, tm, tm_fill):
    t, d = h.shape
    return pl.pallas_call(
        functools.partial(_dispatch_kernel, tm=tm, n_fill=fill_rows.shape[0]),
        grid_spec=pltpu.PrefetchScalarGridSpec(
            num_scalar_prefetch=3,
            grid=(t // tm,),
            in_specs=[pl.BlockSpec((tm, d), lambda i, p1, p2, fr: (i, 0))],
            out_specs=pl.BlockSpec(memory_space=pl.ANY),
            scratch_shapes=[pltpu.VMEM((tm_fill, d), h.dtype), pltpu.SemaphoreType.DMA((3,))],
        ),
        out_shape=jax.ShapeDtypeStruct((n_rows, d), h.dtype),
        compiler_params=_params(("arbitrary",)),
        name="moe_dispatch",
    )(pos1, pos2, fill_rows, h)


def _moe_kernel(te_ref, slot_ref, nxt_ref, lo_ref, hi_ref, na_ref, x_ref, wg_hbm, wu_hbm, wd_hbm, o_ref,
                wg_buf, wu_buf, wd_buf, st_g, st_u, st_d, sem, *, layer):
    i = pl.program_id(0)
    rg = wg_buf.shape[1] // MOE_W_GROUPS
    rd = wd_buf.shape[1] // MOE_W_GROUPS

    def copies(e, c, s):
        r0g = pl.multiple_of(c * rg, rg)
        r0d = pl.multiple_of(c * rd, rd)
        return (pltpu.make_async_copy(wg_hbm.at[layer, e, pl.ds(r0g, rg)], st_g.at[s], sem.at[s, 0]),
                pltpu.make_async_copy(wu_hbm.at[layer, e, pl.ds(r0g, rg)], st_u.at[s], sem.at[s, 1]),
                pltpu.make_async_copy(wd_hbm.at[layer, e, pl.ds(r0d, rd)], st_d.at[s], sem.at[s, 2]))

    def start(e, c):
        for cp in copies(e, c, c % 2):
            cp.start()

    def finish(e, c, w_slot):
        s = c % 2
        for cp in copies(e, c, s):
            cp.wait()
        wg_buf[w_slot, pl.ds(pl.multiple_of(c * rg, rg), rg), :] = st_g[s].astype(BF16)
        wu_buf[w_slot, pl.ds(pl.multiple_of(c * rg, rg), rg), :] = st_u[s].astype(BF16)
        wd_buf[w_slot, pl.ds(pl.multiple_of(c * rd, rd), rd), :] = st_d[s].astype(BF16)

    def fetch(e, lo, hi, w_slot):
        def step(c, carry):
            @pl.when(c + 1 < hi)
            def _():
                start(e, c + 1)
            finish(e, c, w_slot)
            return carry
        lax.fori_loop(lo, hi, step, 0)

    @pl.when(i == 0)
    def _():
        start(te_ref[0], 0)
        fetch(te_ref[0], 0, MOE_W_GROUPS, slot_ref[0])

    @pl.when(i < na_ref[0])
    def _():
        lo, hi, nxt, w_slot = lo_ref[i], hi_ref[i], nxt_ref[i], slot_ref[i]

        @pl.when(hi > lo)
        def _():
            start(nxt, lo)

        x = x_ref[...].astype(BF16)
        gate = jnp.dot(x, wg_buf[w_slot], preferred_element_type=F32)
        up = jnp.dot(x, wu_buf[w_slot], preferred_element_type=F32)
        hid = (jax.nn.silu(gate) * up).astype(BF16)
        o_ref[...] = jnp.dot(hid, wd_buf[w_slot], preferred_element_type=F32)
        fetch(nxt, lo, hi, 1 - w_slot)

    @pl.when(i >= na_ref[0])
    def _():
        o_ref[...] = jnp.zeros_like(o_ref)


def _moe_experts(xs, tables, n_active, w_gate, w_up, w_down, layer, tm):
    p, d = xs.shape
    f = w_gate.shape[3]
    row_blk = lambda i, te, sl, nx, lo, hi, na: (jnp.minimum(i, na[0] - 1), 0)
    hbm = pl.BlockSpec(memory_space=pl.ANY)
    return pl.pallas_call(
        functools.partial(_moe_kernel, layer=layer),
        grid_spec=pltpu.PrefetchScalarGridSpec(
            num_scalar_prefetch=6,
            grid=(p // tm,),
            in_specs=[pl.BlockSpec((tm, d), row_blk), hbm, hbm, hbm],
            out_specs=pl.BlockSpec((tm, d), lambda i, te, sl, nx, lo, hi, na: (i, 0)),
            scratch_shapes=[pltpu.VMEM((2, d, f), BF16), pltpu.VMEM((2, d, f), BF16), pltpu.VMEM((2, f, d), BF16),
                            pltpu.VMEM((2, d // MOE_W_GROUPS, f), F32), pltpu.VMEM((2, d // MOE_W_GROUPS, f), F32),
                            pltpu.VMEM((2, f // MOE_W_GROUPS, d), F32), pltpu.SemaphoreType.DMA((2, 3))],
        ),
        out_shape=jax.ShapeDtypeStruct((p, d), F32),
        compiler_params=_params(("arbitrary",)),
        name="moe_experts",
    )(*tables, n_active, xs, w_gate, w_up, w_down)


def _combine_ln_kernel(p1_ref, p2_ref, h_ref, g1_ref, g2_ref, lg_ref, lb_ref, ys_ref, o_ref,
                       buf1, buf2, sem, *, tm):
    base = pl.program_id(0) * tm

    def issue(r, carry):
        pltpu.make_async_copy(ys_ref.at[pl.ds(p1_ref[base + r], 1)], buf1.at[pl.ds(r, 1)], sem.at[0]).start()
        pltpu.make_async_copy(ys_ref.at[pl.ds(p2_ref[base + r], 1)], buf2.at[pl.ds(r, 1)], sem.at[1]).start()
        return carry

    lax.fori_loop(0, tm, issue, 0)
    pltpu.make_async_copy(ys_ref.at[pl.ds(0, tm)], buf1, sem.at[0]).wait()
    pltpu.make_async_copy(ys_ref.at[pl.ds(0, tm)], buf2, sem.at[1]).wait()
    ffn = g1_ref[...] * buf1[...] + g2_ref[...] * buf2[...]
    o_ref[...] = _layer_norm(ALPHA * h_ref[...] + ffn, lg_ref[...], lb_ref[...])


def _combine_ln(h, ys, pos1, pos2, g1, g2, ln_g, ln_b, tm):
    t, d = h.shape
    row = lambda i, p1, p2: (i, 0)
    const = lambda i, p1, p2: (0, 0)
    return pl.pallas_call(
        functools.partial(_combine_ln_kernel, tm=tm),
        grid_spec=pltpu.PrefetchScalarGridSpec(
            num_scalar_prefetch=2,
            grid=(t // tm,),
            in_specs=[pl.BlockSpec((tm, d), row),
                      pl.BlockSpec((tm, 1), row),
                      pl.BlockSpec((tm, 1), row),
                      pl.BlockSpec((1, d), const),
                      pl.BlockSpec((1, d), const),
                      pl.BlockSpec(memory_space=pl.ANY)],
            out_specs=pl.BlockSpec((tm, d), row),
            scratch_shapes=[pltpu.VMEM((tm, d), F32), pltpu.VMEM((tm, d), F32),
                            pltpu.SemaphoreType.DMA((2,))],
        ),
        out_shape=jax.ShapeDtypeStruct((t, d), F32),
        compiler_params=_params(("arbitrary",)),
        name="moe_combine_ln",
    )(pos1, pos2, h, g1, g2, ln_g, ln_b, ys)


def _moe_block(h, e, gw, rk, counts, w_gate, w_up, w_down, layer, ln_g, ln_b, tm_tok):
    t, d = h.shape
    tm = MOE_TM
    n_rows = 2 * t + N_EXPERTS * tm
    n_tiles = n_rows // tm
    cnt = counts.astype(jnp.int32)
    padded = ((cnt + tm - 1) // tm) * tm
    ends = jnp.cumsum(padded)
    offs = ends - padded
    e_ids = jnp.arange(N_EXPERTS, dtype=jnp.int32)
    pos = jnp.sum(jnp.where(e[..., None] == e_ids, offs, 0), axis=-1) + rk
    n_active = jnp.maximum(ends[-1] // tm, 1)
    tile_start = jnp.minimum(jnp.arange(n_tiles, dtype=jnp.int32), n_active - 1) * tm
    tile_expert = jnp.minimum(jnp.sum((ends[None, :] <= tile_start[:, None]).astype(jnp.int32), axis=1),
                              N_EXPERTS - 1)
    last_tile = jnp.where(padded > 0, ends - tm, -1)
    tail = ends[-1] + jnp.arange(N_EXPERTS, dtype=jnp.int32) * tm
    fill_rows = jnp.concatenate([last_tile, jnp.where(tail < n_rows, tail, -1)]).astype(jnp.int32)
    xs = _dispatch(h, pos[0], pos[1], fill_rows, n_rows, tm_tok, tm)
    idx = jnp.arange(n_tiles, dtype=jnp.int32)
    is_first = jnp.concatenate([jnp.ones((1,), bool), tile_expert[1:] != tile_expert[:-1]])
    run_id = jnp.cumsum(is_first.astype(jnp.int32)) - 1
    run_start = lax.cummax(jnp.where(is_first, idx, 0))
    same_run = run_id[:, None] == run_id[None, :]
    run_len = jnp.sum((same_run & (idx[None, :] < n_active)).astype(jnp.int32), axis=1)
    nxt_tile = run_start + run_len
    nxt = jnp.sum(jnp.where(idx[None, :] == nxt_tile[:, None], tile_expert[None, :], 0), axis=1)
    has_next = nxt_tile < n_active
    k_in_run = idx - run_start
    lo = jnp.where(has_next, (MOE_W_GROUPS * k_in_run) // jnp.maximum(run_len, 1), 0)
    hi = jnp.where(has_next, (MOE_W_GROUPS * (k_in_run + 1)) // jnp.maximum(run_len, 1), 0)
    tables = [a.astype(jnp.int32) for a in (tile_expert, run_id % 2, jnp.where(has_next, nxt, 0), lo, hi)]
    ys = _moe_experts(xs, tables, n_active.reshape(1).astype(jnp.int32), w_gate, w_up, w_down, layer, tm)
    return _combine_ln(h, ys, pos[0], pos[1], gw[0].reshape(t, 1), gw[1].reshape(t, 1), ln_g, ln_b, tm_tok)


def kernel(x, ab_w_in, ab_lambda_re, ab_lambda_im, ab_log_dt, ab_b_re, ab_b_im, ab_c_re, ab_c_im, ab_d,
           ab_w_glu, ab_b_glu, ab_w_out, c_w_in, c_w_group, c_scale, c_w_out, ln_g, ln_b, router_w,
           router_b, moe_w_gate, moe_w_up, moe_w_down):
    bsz, seq, d = x.shape
    t = bsz * seq
    tm = min(ROW_TILE, t)
    assert t % tm == 0 and seq % tm == 0 and seq % S5_CHUNK == 0
    row = lambda a: a.reshape(1, -1)

    rw = jnp.zeros((d, LANES), F32).at[:, :N_EXPERTS].set(router_w)
    rw_hi = rw.astype(BF16)
    rw_lo = (rw - rw_hi.astype(F32)).astype(BF16)
    rb = jnp.zeros((1, LANES), F32).at[0, :N_EXPERTS].set(router_b)
    router = (rw_hi, rw_lo, rb)

    h = x.reshape(t, d)

    wg_b, wu_b, wd_b = moe_w_gate, moe_w_up, moe_w_down
    u, qkv = _in_proj_ab(h, ab_w_in[0].astype(BF16), min(2 * ROW_TILE, t))
    chunks = seq // S5_CHUNK
    nsteps = max(int(math.log2(chunks)), 0)
    assert (1 << nsteps) == chunks
    ops = _s5_operands(ab_lambda_re[0], ab_lambda_im[0], ab_log_dt[0], ab_b_re[0], ab_b_im[0],
                       ab_c_re[0], ab_c_im[0], S5_CHUNK, nsteps)
    y_ssm = _s5_scan(u, ops, bsz, seq)
    y_a = _s5_glu(y_ssm, u, ab_d[0], ab_w_glu[0], ab_b_glu[0], tm)
    y_b = _sb_attention(qkv.reshape(bsz, seq, 3 * SB_WIDTH), bsz, seq).reshape(t, SB_WIDTH)
    w_out = ab_w_out[0].astype(BF16)
    h, e, gw, rk, cnt = _proj_ln_route([y_a, y_b], [w_out[:S5_WIDTH], w_out[S5_WIDTH:]], h,
                                       row(ln_g[0, 0]), row(ln_b[0, 0]), router, tm)
    h = _moe_block(h, e, gw, rk, cnt, wg_b, wu_b, wd_b, 0, row(ln_g[0, 1]), row(ln_b[0, 1]), tm)

    pooled = _pool_in(h, c_w_in[0].astype(BF16), seq, tm)
    h, e, gw, rk, cnt = _proj_ln_route([pooled], [c_w_out[0].astype(BF16)], h,
                                       row(ln_g[1, 0]), row(ln_b[1, 0]), router, tm,
                                       grouped=(c_w_group[0].astype(BF16), row(c_scale[0])))
    h = _moe_block(h, e, gw, rk, cnt, wg_b, wu_b, wd_b, 1, row(ln_g[1, 1]), row(ln_b[1, 1]), tm)
    return h.reshape(bsz, seq, d)
```

```python
import functools
import math

import jax
import jax.numpy as jnp
from jax import lax
from jax.experimental import pallas as pl
from jax.experimental.pallas import tpu as pltpu

S5_WIDTH = 512
S5_GROUP = 16
S5_GROUPS = 32
S5_STATE = 64
SB_HEAD_DIM = 128
SB_HEADS = 12
SB_WIDTH = SB_HEADS * SB_HEAD_DIM
POOL_WINDOWS = (2, 4, 8, 16)
POOL_GROUP_WIDTH = 512
N_EXPERTS = 16
N_EXPERT_GROUPS = 4
EXPERTS_PER_GROUP = 4
DEPTH = 2
ALPHA = (2.0 * DEPTH) ** 0.25
LN_EPS = 1e-5

LANES = 128
VMEM_LIMIT = 56 * 1024 * 1024

S5_CHUNK = 32
ATT_TQ = 128
ATT_HEADS_PER_STEP = 2
ATT_TILES_PER_ITER = 2
ATT_LOG_UNDERFLOW = -88.0
MOE_TM = 256
MOE_W_GROUPS = 8
ROW_TILE = 512

BF16 = jnp.bfloat16
F32 = jnp.float32


def _params(sem, vmem=VMEM_LIMIT):
    return pltpu.CompilerParams(dimension_semantics=sem, vmem_limit_bytes=vmem)


def _in_proj_ab_kernel(x_ref, w_ref, u_ref, qkv_ref, xb_ref):
    j = pl.program_id(1)

    @pl.when(j == 0)
    def _():
        xb_ref[...] = x_ref[...].astype(BF16)

    acc = jnp.dot(xb_ref[...], w_ref[...], preferred_element_type=F32)

    @pl.when(j == 0)
    def _():
        u_ref[...] = acc

    @pl.when(j > 0)
    def _():
        qkv_ref[...] = acc.astype(qkv_ref.dtype)


def _in_proj_ab(x, w, tm):
    m, k = x.shape
    tn = S5_WIDTH
    n_blocks = w.shape[1] // tn
    return pl.pallas_call(
        _in_proj_ab_kernel,
        grid=(m // tm, n_blocks),
        in_specs=[pl.BlockSpec((tm, k), lambda i, j: (i, 0)),
                  pl.BlockSpec((k, tn), lambda i, j: (0, j))],
        out_specs=[pl.BlockSpec((tm, tn), lambda i, j: (i, 0)),
                   pl.BlockSpec((tm, tn), lambda i, j: (i, jnp.maximum(j - 1, 0)))],
        out_shape=[jax.ShapeDtypeStruct((m, tn), F32),
                   jax.ShapeDtypeStruct((m, w.shape[1] - tn), BF16)],
        scratch_shapes=[pltpu.VMEM((tm, k), BF16)],
        compiler_params=_params(("parallel", "arbitrary")),
        name="in_proj_ab",
    )(x, w)


def _bmm_f32_kernel(x_ref, y_ref, o_ref):
    o_ref[...] = jnp.dot(x_ref[...], y_ref[...], preferred_element_type=F32,
                         precision=lax.Precision.HIGHEST)


def _bmm_f32(x, y):
    g, m, k = x.shape
    n = y.shape[2]
    return pl.pallas_call(
        _bmm_f32_kernel,
        grid=(g,),
        in_specs=[pl.BlockSpec((None, m, k), lambda i: (i, 0, 0)),
                  pl.BlockSpec((None, k, n), lambda i: (i, 0, 0))],
        out_specs=pl.BlockSpec((None, m, n), lambda i: (i, 0, 0)),
        out_shape=jax.ShapeDtypeStruct((g, m, n), F32),
        compiler_params=_params(("parallel",)),
        name="s5_taps",
    )(x, y)


def _s5_kernel(u_ref, wt_ref, ws_ref, wo_ref, a1_ref, a2_ref, o_ref, *, chunks, nsteps):
    u = u_ref[...]
    y_intra = jnp.dot(u, wt_ref[...], preferred_element_type=F32)
    s = jnp.dot(u, ws_ref[...], preferred_element_type=F32)
    m = s.shape[0]
    c_idx = lax.broadcasted_iota(jnp.int32, s.shape, 0) % chunks
    half = s.shape[1] // 2
    for k in range(nsteps):
        d = 1 << k
        sh = jnp.where(c_idx >= d, pltpu.roll(s, d, 0), 0.0)
        s = s + a1_ref[k:k + 1, :] * sh + a2_ref[k:k + 1, :] * pltpu.roll(sh, half, 1)
    s_prev = jnp.where(c_idx >= 1, pltpu.roll(s, 1, 0), 0.0)
    y_inter = jnp.dot(s_prev.astype(BF16), wo_ref[...], preferred_element_type=F32)
    o_ref[...] = y_intra + y_inter


def _s5_operands(lam_re, lam_im, log_dt, b_re, b_im, c_re, c_im, chunk, nsteps):
    g, n = lam_re.shape
    h = S5_GROUP
    dt = jnp.exp(log_dt)[:, None]
    lr, li = lam_re, lam_im
    mag = jnp.exp(lr * dt)
    lbr = mag * jnp.cos(li * dt)
    lbi = mag * jnp.sin(li * dt)
    den = lr * lr + li * li
    coef_re = ((lbr - 1.0) * lr + lbi * li) / den
    coef_im = (lbi * lr - (lbr - 1.0) * li) / den
    bbar_re = coef_re[..., None] * b_re - coef_im[..., None] * b_im
    bbar_im = coef_re[..., None] * b_im + coef_im[..., None] * b_re

    def power(mult):
        mult = mult.astype(F32)[None, :, None]
        pm = jnp.exp(lr[:, None, :] * dt[:, None, :] * mult)
        ang = li[:, None, :] * dt[:, None, :] * mult
        return pm * jnp.cos(ang), pm * jnp.sin(ang)

    pr, pi = power(jnp.arange(chunk + 1))
    cre = jnp.swapaxes(c_re, 1, 2)
    cim = jnp.swapaxes(c_im, 1, 2)
    cb_re = cre[:, :, :, None] * bbar_re[:, :, None, :] - cim[:, :, :, None] * bbar_im[:, :, None, :]
    cb_im = cre[:, :, :, None] * bbar_im[:, :, None, :] + cim[:, :, :, None] * bbar_re[:, :, None, :]
    y_op = jnp.concatenate([cb_re, cb_im], axis=1).reshape(g, 2 * n, h * h)
    x_op = jnp.concatenate([pr[:, :chunk], -pi[:, :chunk]], axis=-1)
    taps = _bmm_f32(x_op, y_op).reshape(g, chunk, h, h)
    jj = jnp.arange(chunk)[:, None]
    tt = jnp.arange(chunk)[None, :]
    lag = jnp.clip(tt - jj, 0, chunk - 1)
    wt = jnp.where((tt >= jj)[None, :, :, None, None], taps[:, lag], 0.0)
    wt = wt.transpose(0, 1, 4, 2, 3).reshape(g, chunk * h, chunk * h)
    prr = pr[:, chunk - 1::-1][:, :chunk]
    pir = pi[:, chunk - 1::-1][:, :chunk]
    ws_re = prr[:, :, None, :] * jnp.swapaxes(bbar_re, 1, 2)[:, None] - pir[:, :, None, :] * jnp.swapaxes(bbar_im, 1, 2)[:, None]
    ws_im = prr[:, :, None, :] * jnp.swapaxes(bbar_im, 1, 2)[:, None] + pir[:, :, None, :] * jnp.swapaxes(bbar_re, 1, 2)[:, None]
    ws = jnp.concatenate([ws_re, ws_im], axis=-1).reshape(g, chunk * h, 2 * n)
    p1r = pr[:, 1:chunk + 1]
    p1i = pi[:, 1:chunk + 1]
    wo_re = c_re[:, None] * p1r[:, :, None, :] - c_im[:, None] * p1i[:, :, None, :]
    wo_im = -(c_re[:, None] * p1i[:, :, None, :] + c_im[:, None] * p1r[:, :, None, :])
    wo = jnp.concatenate([wo_re, wo_im], axis=-1)
    wo = wo.transpose(0, 3, 1, 2).reshape(g, 2 * n, chunk * h)
    sr, si = power(chunk * (2 ** jnp.arange(max(nsteps, 1))))
    a1 = jnp.concatenate([sr, sr], axis=-1)
    a2 = jnp.concatenate([-si, si], axis=-1)
    return wt.astype(BF16), ws.astype(BF16), wo.astype(BF16), a1, a2


def _s5_scan(u, ops, bsz, seq):
    wt, ws, wo, a1, a2 = ops
    g, h, chunk = S5_GROUPS, S5_GROUP, S5_CHUNK
    chunks = seq // chunk
    m = bsz * chunks
    nsteps = a1.shape[1] if chunks > 1 else 0
    up = u.reshape(bsz, chunks, chunk, g, h).transpose(3, 0, 1, 2, 4).reshape(g, m, chunk * h).astype(BF16)
    n2 = ws.shape[2]
    ns = a1.shape[1]
    yp = pl.pallas_call(
        functools.partial(_s5_kernel, chunks=chunks, nsteps=nsteps),
        grid=(g,),
        in_specs=[pl.BlockSpec((None, m, chunk * h), lambda i: (i, 0, 0)),
                  pl.BlockSpec((None, chunk * h, chunk * h), lambda i: (i, 0, 0)),
                  pl.BlockSpec((None, chunk * h, n2), lambda i: (i, 0, 0)),
                  pl.BlockSpec((None, n2, chunk * h), lambda i: (i, 0, 0)),
                  pl.BlockSpec((None, ns, n2), lambda i: (i, 0, 0)),
                  pl.BlockSpec((None, ns, n2), lambda i: (i, 0, 0))],
        out_specs=pl.BlockSpec((None, m, chunk * h), lambda i: (i, 0, 0)),
        out_shape=jax.ShapeDtypeStruct((g, m, chunk * h), F32),
        compiler_params=_params(("parallel",)),
        name="s5_scan",
    )(up, wt, ws, wo, a1, a2)
    return yp.reshape(g, bsz, chunks, chunk, h).transpose(1, 2, 3, 0, 4).reshape(bsz * seq, g * h)


def _s5_glu_kernel(y_ref, u_ref, d_ref, w_ref, b_ref, o_ref):
    y = y_ref[...] + d_ref[...] * u_ref[...]
    gl = jax.nn.gelu(y)
    gate = jnp.dot(gl.astype(BF16), w_ref[...], preferred_element_type=F32) + b_ref[...]
    o_ref[...] = (gl * jax.nn.sigmoid(gate)).astype(o_ref.dtype)


def _s5_glu(y, u, d, w_glu, b_glu, tm):
    t, c = y.shape
    return pl.pallas_call(
        _s5_glu_kernel,
        grid=(t // tm,),
        in_specs=[pl.BlockSpec((tm, c), lambda i: (i, 0)),
                  pl.BlockSpec((tm, c), lambda i: (i, 0)),
                  pl.BlockSpec((1, c), lambda i: (0, 0)),
                  pl.BlockSpec((c, c), lambda i: (0, 0)),
                  pl.BlockSpec((1, c), lambda i: (0, 0))],
        out_specs=pl.BlockSpec((tm, c), lambda i: (i, 0)),
        out_shape=jax.ShapeDtypeStruct((t, c), BF16),
        compiler_params=_params(("parallel",)),
        name="s5_glu",
    )(y, u, d.reshape(1, c), w_glu.astype(BF16), b_glu.reshape(1, c))


def _sb_steps(tiles, tri, scale):
    dh = SB_HEAD_DIM
    nk = tiles[0][1].shape[0]
    left = lax.broadcasted_iota(jnp.int32, (nk, 2 * dh), 1) < dh
    pre = []
    stack = []
    for q2, k2, v2, causal, _ in tiles:
        zero = jnp.zeros_like(k2)
        k_bd = jnp.concatenate([jnp.where(left, k2, zero), jnp.where(left, zero, k2)], axis=0)
        z2 = lax.dot_general(q2, k_bd, (((1,), (1,)), ((), ())), preferred_element_type=F32) * scale
        per_head = []
        for hh in range(2):
            z = z2[:, hh * nk:(hh + 1) * nk]
            sp = jnp.maximum(z, 0.0) + jnp.log(1.0 + jnp.exp(-jnp.abs(z)))
            log_keep = -sp if causal is None else jnp.where(causal, -sp, 0.0)
            hi = log_keep.astype(BF16)
            lo = (log_keep - hi.astype(F32)).astype(BF16)
            stack += [hi, lo]
            per_head.append((z - sp, log_keep))
        pre.append(per_head)
    tq = tiles[0][0].shape[0]
    sums = jnp.dot(jnp.concatenate(stack, axis=0), tri, preferred_element_type=F32)
    results = []
    for ti, (q2, k2, v2, causal, r_in) in enumerate(tiles):
        ws, rs = [], []
        for hh in range(2):
            log_beta, log_keep = pre[ti][hh]
            base = (ti * 2 + hh) * 2 * tq
            tail = sums[base:base + tq] + sums[base + tq:base + 2 * tq]
            if r_in is not None:
                tail = tail + r_in[hh]
            w = jnp.exp(log_beta + tail)
            if causal is not None:
                w = jnp.where(causal, w, 0.0)
            ws.append(w.astype(BF16))
            rs.append(jnp.sum(log_keep, axis=1, keepdims=True))
        zero = jnp.zeros_like(v2)
        v_bd = jnp.concatenate([jnp.where(left, v2, zero), jnp.where(left, zero, v2)], axis=0)
        out = jnp.dot(jnp.concatenate(ws, axis=1), v_bd, preferred_element_type=F32)
        results.append((out, rs))
    return results


def _sb_attn_kernel(q_ref, k_ref, v_ref, o_ref, acc_ref, r_ref, tri_ref, *, seq, tq, scale):
    dh = SB_HEAD_DIM
    win = 2 * tq
    row = lax.broadcasted_iota(jnp.int32, (win, win), 0)
    col = lax.broadcasted_iota(jnp.int32, (win, win), 1)
    tri_ref[...] = jnp.where(row > col, 1.0, 0.0).astype(BF16)
    assert ATT_HEADS_PER_STEP == 2

    def q_tiles(p, carry):
        first, where = [], []
        for ti in range(ATT_TILES_PER_ITER):
            i = p * ATT_TILES_PER_ITER + ti
            q0 = pl.multiple_of(i * tq, tq)
            kblk = jnp.maximum(i - 1, 0)
            k0 = pl.multiple_of(kblk * tq, tq)
            q_pos = q0 + lax.broadcasted_iota(jnp.int32, (tq, win), 0)
            causal = (k0 + lax.broadcasted_iota(jnp.int32, (tq, win), 1)) < q_pos
            first.append((q_ref[pl.ds(q0, tq), :], k_ref[pl.ds(k0, win), :], v_ref[pl.ds(k0, win), :], causal, None))
            where.append((q0, kblk))
        rmax = []
        for ti, (out, rs) in enumerate(_sb_steps(first, tri_ref[...], scale)):
            acc_ref[ti] = out
            r_ref[2 * ti] = rs[0]
            r_ref[2 * ti + 1] = rs[1]
            rmax.append(jnp.maximum(jnp.max(rs[0]), jnp.max(rs[1])))
        for ti, (q0, kblk) in enumerate(where):
            def cond(c):
                j, rm = c
                return jnp.logical_and(j >= 0, rm > ATT_LOG_UNDERFLOW)

            def body(c, ti=ti, q0=q0):
                j, _ = c
                kj = pl.multiple_of(j * tq, tq)
                r_old = [r_ref[2 * ti], r_ref[2 * ti + 1]]
                tile = (q_ref[pl.ds(q0, tq), :], k_ref[pl.ds(kj, tq), :], v_ref[pl.ds(kj, tq), :], None, r_old)
                (out, rs), = _sb_steps([tile], tri_ref[:tq, :tq], scale)
                acc_ref[ti] += out
                r_new = [r_old[0] + rs[0], r_old[1] + rs[1]]
                r_ref[2 * ti] = r_new[0]
                r_ref[2 * ti + 1] = r_new[1]
                return j - 1, jnp.maximum(jnp.max(r_new[0]), jnp.max(r_new[1]))

            lax.while_loop(cond, body, (kblk - 1, rmax[ti]))
            o_ref[pl.ds(q0, tq), :] = acc_ref[ti].astype(o_ref.dtype)
        return carry

    lax.fori_loop(0, seq // (tq * ATT_TILES_PER_ITER), q_tiles, 0)


def _sb_attention(qkv, bsz, seq):
    tq = ATT_TQ
    assert seq % (tq * ATT_TILES_PER_ITER) == 0 and seq >= 2 * tq
    wcols = ATT_HEADS_PER_STEP * SB_HEAD_DIM
    nhp = SB_WIDTH // wcols
    kern = functools.partial(_sb_attn_kernel, seq=seq, tq=tq, scale=1.0 / math.sqrt(SB_HEAD_DIM))
    return pl.pallas_call(
        kern,
        grid=(bsz, nhp),
        in_specs=[pl.BlockSpec((None, seq, wcols), lambda b, h: (b, 0, h)),
                  pl.BlockSpec((None, seq, wcols), lambda b, h: (b, 0, nhp + h)),
                  pl.BlockSpec((None, seq, wcols), lambda b, h: (b, 0, 2 * nhp + h))],
        out_specs=pl.BlockSpec((None, seq, wcols), lambda b, h: (b, 0, h)),
        out_shape=jax.ShapeDtypeStruct((bsz, seq, SB_WIDTH), BF16),
        scratch_shapes=[pltpu.VMEM((ATT_TILES_PER_ITER, tq, wcols), F32),
                        pltpu.VMEM((ATT_HEADS_PER_STEP * ATT_TILES_PER_ITER, tq, 1), F32),
                        pltpu.VMEM((2 * tq, 2 * tq), BF16)],
        compiler_params=_params(("parallel", "parallel")),
        name="sb_attention",
    )(qkv, qkv, qkv)


def _pool_in_kernel(x_ref, w_ref, o_ref, halo_ref, *, seq, tm):
    i = pl.program_id(0)
    hmax = POOL_WINDOWS[-1]
    acc = jnp.dot(x_ref[...].astype(BF16), w_ref[...], preferred_element_type=F32)
    t0 = (i * tm) % seq

    @pl.when(t0 == 0)
    def _():
        halo_ref[...] = jnp.zeros_like(halo_ref)

    ext = jnp.concatenate([halo_ref[...], acc], axis=0)
    t_idx = t0 + lax.broadcasted_iota(jnp.int32, (tm, POOL_GROUP_WIDTH), 0)
    for gi, win in enumerate(POOL_WINDOWS):
        cs = slice(gi * POOL_GROUP_WIDTH, (gi + 1) * POOL_GROUP_WIDTH)
        e = ext[:, cs]
        span = 1
        while span < win:
            e = e + pltpu.roll(e, span, 0)
            span *= 2
        wsum = e[hmax:, :]
        cnt = jnp.minimum(t_idx + 1, win).astype(F32)
        o_ref[:, cs] = (wsum / cnt - acc[:, cs]).astype(o_ref.dtype)
    halo_ref[...] = acc[tm - hmax:, :]


def _pool_in(x, w, seq, tm):
    t, k = x.shape
    n = w.shape[1]
    return pl.pallas_call(
        functools.partial(_pool_in_kernel, seq=seq, tm=tm),
        grid=(t // tm,),
        in_specs=[pl.BlockSpec((tm, k), lambda i: (i, 0)),
                  pl.BlockSpec((k, n), lambda i: (0, 0))],
        out_specs=pl.BlockSpec((tm, n), lambda i: (i, 0)),
        out_shape=jax.ShapeDtypeStruct((t, n), BF16),
        scratch_shapes=[pltpu.VMEM((POOL_WINDOWS[-1], n), F32)],
        compiler_params=_params(("arbitrary",)),
        name="pool_in",
    )(x, w)


def _layer_norm(x, g, b):
    mu = jnp.mean(x, axis=-1, keepdims=True)
    xc = x - mu
    var = jnp.mean(xc * xc, axis=-1, keepdims=True)
    return xc * lax.rsqrt(var + LN_EPS) * g + b


def _route(h, rwh_ref, rwl_ref, rb_ref, base_ref, tm):
    h_hi = h.astype(BF16)
    h_lo = (h - h_hi.astype(F32)).astype(BF16)
    rwh = rwh_ref[...]
    logits = (jnp.dot(h_hi, rwh, preferred_element_type=F32)
              + jnp.dot(h_lo, rwh, preferred_element_type=F32)
              + jnp.dot(h_hi, rwl_ref[...], preferred_element_type=F32)) + rb_ref[...]
    lt = logits.T[:N_EXPERTS, :]
    mx = jnp.max(lt, axis=0, keepdims=True)
    ex = jnp.exp(lt - mx)
    probs = ex / jnp.sum(ex, axis=0, keepdims=True)
    p = [probs[e:e + 1, :] for e in range(N_EXPERTS)]

    def first_max(vals):
        best, idx = vals[0], jnp.zeros_like(vals[0], dtype=jnp.int32)
        for n in range(1, len(vals)):
            upd = vals[n] > best
            idx = jnp.where(upd, n, idx)
            best = jnp.where(upd, vals[n], best)
        return best, idx

    scores = []
    for gi in range(N_EXPERT_GROUPS):
        p0, p1, p2, p3 = p[4 * gi:4 * gi + 4]
        a, b = jnp.maximum(p0, p1), jnp.minimum(p0, p1)
        c, d = jnp.maximum(p2, p3), jnp.minimum(p2, p3)
        scores.append(jnp.maximum(a, c) + jnp.maximum(jnp.minimum(a, c), jnp.maximum(b, d)))
    _, g_sel = first_max(scores)
    sel = []
    for n in range(EXPERTS_PER_GROUP):
        v = p[n]
        for gi in range(1, N_EXPERT_GROUPS):
            v = jnp.where(g_sel == gi, p[4 * gi + n], v)
        sel.append(v)
    v1, i1 = first_max(sel)
    v2, i2 = first_max([jnp.where(i1 == n, -1.0, sel[n]) for n in range(EXPERTS_PER_GROUP)])
    den = v1 + v2
    e1 = g_sel * EXPERTS_PER_GROUP + i1
    e2 = g_sel * EXPERTS_PER_GROUP + i2
    e_iota = lax.broadcasted_iota(jnp.int32, (N_EXPERTS, tm), 0)
    oh1 = e_iota == e1
    oh2 = e_iota == e2
    oh = jnp.where(jnp.logical_or(oh1, oh2), 1.0, 0.0)
    r_i = lax.broadcasted_iota(jnp.int32, (tm, tm), 0)
    c_i = lax.broadcasted_iota(jnp.int32, (tm, tm), 1)
    before = jnp.where(r_i < c_i, 1.0, 0.0).astype(BF16)
    cnt = jnp.dot(oh.astype(BF16), before, preferred_element_type=F32) + base_ref[...]
    rank1 = jnp.sum(jnp.where(oh1, cnt, 0.0), axis=0, keepdims=True)
    rank2 = jnp.sum(jnp.where(oh2, cnt, 0.0), axis=0, keepdims=True)
    base_ref[...] += jnp.sum(oh, axis=1, keepdims=True)
    e_out = jnp.concatenate([e1, e2], axis=0)
    w_out = jnp.concatenate([v1 / den, v2 / den], axis=0)
    r_out = jnp.concatenate([rank1, rank2], axis=0).astype(jnp.int32)
    return e_out, w_out, r_out


def _proj_ln_kernel(*refs, n_x, grouped, tm):
    xs = refs[:n_x]
    ws = refs[n_x:2 * n_x]
    pos = 2 * n_x
    if grouped:
        wg_ref, sc_ref = refs[pos:pos + 2]
        pos += 2
    res_ref, g_ref, b_ref, rwh_ref, rwl_ref, rb_ref = refs[pos:pos + 6]
    h_ref, e_ref, gw_ref, rk_ref, cnt_ref, base_ref = refs[pos + 6:pos + 12]

    @pl.when(pl.program_id(0) == 0)
    def _():
        base_ref[...] = jnp.zeros_like(base_ref)

    if grouped:
        x = xs[0][...]
        parts = []
        for gi in range(len(POOL_WINDOWS)):
            cs = slice(gi * POOL_GROUP_WIDTH, (gi + 1) * POOL_GROUP_WIDTH)
            parts.append(jnp.dot(x[:, cs], wg_ref[gi], preferred_element_type=F32))
        y = (jnp.concatenate(parts, axis=1) * sc_ref[...]).astype(BF16)
        mix = jnp.dot(y, ws[0][...], preferred_element_type=F32)
    else:
        mix = jnp.dot(xs[0][...], ws[0][...], preferred_element_type=F32)
        for n in range(1, n_x):
            mix = mix + jnp.dot(xs[n][...], ws[n][...], preferred_element_type=F32)
    h = _layer_norm(ALPHA * res_ref[...] + mix, g_ref[...], b_ref[...])
    h_ref[...] = h
    e_out, w_out, r_out = _route(h, rwh_ref, rwl_ref, rb_ref, base_ref, tm)
    e_ref[...] = e_out
    gw_ref[...] = w_out
    rk_ref[...] = r_out
    cnt_ref[...] = jnp.broadcast_to(base_ref[...], cnt_ref.shape)


def _proj_ln_route(xs, ws, res, ln_g, ln_b, router, tm, grouped=None):
    t, d = res.shape
    rwh, rwl, rb = router
    n_x = len(xs)
    full = lambda a: pl.BlockSpec(a.shape, lambda i, nd=a.ndim: (0,) * nd)
    in_specs = [pl.BlockSpec((tm, x.shape[1]), lambda i: (i, 0)) for x in xs]
    in_specs += [full(w) for w in ws]
    args = list(xs) + list(ws)
    if grouped is not None:
        in_specs += [full(a) for a in grouped]
        args += list(grouped)
    in_specs += [pl.BlockSpec((tm, d), lambda i: (i, 0)), full(ln_g), full(ln_b), full(rwh), full(rwl), full(rb)]
    args += [res, ln_g, ln_b, rwh, rwl, rb]
    pair = lambda dt: jax.ShapeDtypeStruct((2, t), dt)
    out = pl.pallas_call(
        functools.partial(_proj_ln_kernel, n_x=n_x, grouped=grouped is not None, tm=tm),
        grid=(t // tm,),
        in_specs=in_specs,
        out_specs=[pl.BlockSpec((tm, d), lambda i: (i, 0)),
                   pl.BlockSpec((2, tm), lambda i: (0, i)),
                   pl.BlockSpec((2, tm), lambda i: (0, i)),
                   pl.BlockSpec((2, tm), lambda i: (0, i)),
                   pl.BlockSpec((N_EXPERTS, LANES), lambda i: (0, 0))],
        out_shape=[jax.ShapeDtypeStruct((t, d), F32), pair(jnp.int32), pair(F32), pair(jnp.int32),
                   jax.ShapeDtypeStruct((N_EXPERTS, LANES), F32)],
        scratch_shapes=[pltpu.VMEM((N_EXPERTS, 1), F32)],
        compiler_params=_params(("arbitrary",)),
        name="proj_ln_route",
    )(*args)
    h, e, gw, rk, cnt = out
    return h, e, gw, rk, cnt[:, 0]


def _dispatch_kernel(p1_ref, p2_ref, fill_ref, h_ref, xs_ref, zero_ref, sem, *, tm, n_fill):
    base = pl.program_id(0) * tm

    @pl.when(pl.program_id(0) == 0)
    def _():
        zero_ref[...] = jnp.zeros_like(zero_ref)
        n_zero = zero_ref.shape[0]
        fill = lambda k: pltpu.make_async_copy(
            zero_ref, xs_ref.at[pl.ds(pl.multiple_of(jnp.maximum(fill_ref[k], 0), n_zero), n_zero)], sem.at[2])
        for k in range(n_fill):
            pl.when(fill_ref[k] >= 0)(lambda k=k: fill(k).start())
        for k in range(n_fill):
            pl.when(fill_ref[k] >= 0)(lambda k=k: fill(k).wait())

    def issue(r, carry):
        src = h_ref.at[pl.ds(r, 1)]
        pltpu.make_async_copy(src, xs_ref.at[pl.ds(p1_ref[base + r], 1)], sem.at[0]).start()
        pltpu.make_async_copy(src, xs_ref.at[pl.ds(p2_ref[base + r], 1)], sem.at[1]).start()
        return carry

    lax.fori_loop(0, tm, issue, 0)
    pltpu.make_async_copy(h_ref, xs_ref.at[pl.ds(0, tm)], sem.at[0]).wait()
    pltpu.make_async_copy(h_ref, xs_ref.at[pl.ds(0, tm)], sem.at[1]).wait()


def _dispatch(h, pos1, pos2, fill_rows, n_rows, tm, tm_fill):
    t, d = h.shape
    return pl.pallas_call(
        functools.partial(_dispatch_kernel, tm=tm, n_fill=fill_rows.shape[0]),
        grid_spec=pltpu.PrefetchScalarGridSpec(
            num_scalar_prefetch=3,
            grid=(t // tm,),
            in_specs=[pl.BlockSpec((tm, d), lambda i, p1, p2, fr: (i, 0))],
            out_specs=pl.BlockSpec(memory_space=pl.ANY),
            scratch_shapes=[pltpu.VMEM((tm_fill, d), h.dtype), pltpu.SemaphoreType.DMA((3,))],
        ),
        out_shape=jax.ShapeDtypeStruct((n_rows, d), h.dtype),
        compiler_params=_params(("arbitrary",)),
        name="moe_dispatch",
    )(pos1, pos2, fill_rows, h)


def _moe_kernel(te_ref, slot_ref, nxt_ref, lo_ref, hi_ref, na_ref, x_ref, wg_hbm, wu_hbm, wd_hbm, o_ref,
                wg_buf, wu_buf, wd_buf, st_g, st_u, st_d, sem, *, layer):
    i = pl.program_id(0)
    rg = wg_buf.shape[1] // MOE_W_GROUPS
    rd = wd_buf.shape[1] // MOE_W_GROUPS

    def copies(e, c, s):
        r0g = pl.multiple_of(c * rg, rg)
        r0d = pl.multiple_of(c * rd, rd)
        return (pltpu.make_async_copy(wg_hbm.at[layer, e, pl.ds(r0g, rg)], st_g.at[s], sem.at[s, 0]),
                pltpu.make_async_copy(wu_hbm.at[layer, e, pl.ds(r0g, rg)], st_u.at[s], sem.at[s, 1]),
                pltpu.make_async_copy(wd_hbm.at[layer, e, pl.ds(r0d, rd)], st_d.at[s], sem.at[s, 2]))

    def start(e, c):
        for cp in copies(e, c, c % 2):
            cp.start()

    def finish(e, c, w_slot):
        s = c % 2
        for cp in copies(e, c, s):
            cp.wait()
        wg_buf[w_slot, pl.ds(pl.multiple_of(c * rg, rg), rg), :] = st_g[s].astype(BF16)
        wu_buf[w_slot, pl.ds(pl.multiple_of(c * rg, rg), rg), :] = st_u[s].astype(BF16)
        wd_buf[w_slot, pl.ds(pl.multiple_of(c * rd, rd), rd), :] = st_d[s].astype(BF16)

    def fetch(e, lo, hi, w_slot):
        def step(c, carry):
            @pl.when(c + 1 < hi)
            def _():
                start(e, c + 1)
            finish(e, c, w_slot)
            return carry
        lax.fori_loop(lo, hi, step, 0)

    @pl.when(i == 0)
    def _():
        start(te_ref[0], 0)
        fetch(te_ref[0], 0, MOE_W_GROUPS, slot_ref[0])

    @pl.when(i < na_ref[0])
    def _():
        lo, hi, nxt, w_slot = lo_ref[i], hi_ref[i], nxt_ref[i], slot_ref[i]

        def issue_first(_, carry):
            start(nxt, lo)
            return carry
        lax.fori_loop(0, (hi > lo).astype(jnp.int32), issue_first, 0)

        x = x_ref[...].astype(BF16)
        gate = jnp.dot(x, wg_buf[w_slot], preferred_element_type=F32)
        up = jnp.dot(x, wu_buf[w_slot], preferred_element_type=F32)
        hid = (jax.nn.silu(gate) * up).astype(BF16)
        o_ref[...] = jnp.dot(hid, wd_buf[w_slot], preferred_element_type=F32)
        fetch(nxt, lo, hi, 1 - w_slot)

    @pl.when(i >= na_ref[0])
    def _():
        o_ref[...] = jnp.zeros_like(o_ref)


def _moe_experts(xs, tables, n_active, w_gate, w_up, w_down, layer, tm):
    p, d = xs.shape
    f = w_gate.shape[3]
    row_blk = lambda i, te, sl, nx, lo, hi, na: (jnp.minimum(i, na[0] - 1), 0)
    hbm = pl.BlockSpec(memory_space=pl.ANY)
    return pl.pallas_call(
        functools.partial(_moe_kernel, layer=layer),
        grid_spec=pltpu.PrefetchScalarGridSpec(
            num_scalar_prefetch=6,
            grid=(p // tm,),
            in_specs=[pl.BlockSpec((tm, d), row_blk), hbm, hbm, hbm],
            out_specs=pl.BlockSpec((tm, d), lambda i, te, sl, nx, lo, hi, na: (i, 0)),
            scratch_shapes=[pltpu.VMEM((2, d, f), BF16), pltpu.VMEM((2, d, f), BF16), pltpu.VMEM((2, f, d), BF16),
                            pltpu.VMEM((2, d // MOE_W_GROUPS, f), F32), pltpu.VMEM((2, d // MOE_W_GROUPS, f), F32),
                            pltpu.VMEM((2, f // MOE_W_GROUPS, d), F32), pltpu.SemaphoreType.DMA((2, 3))],
        ),
        out_shape=jax.ShapeDtypeStruct((p, d), F32),
        compiler_params=_params(("arbitrary",)),
        name="moe_experts",
    )(*tables, n_active, xs, w_gate, w_up, w_down)


def _combine_ln_kernel(p1_ref, p2_ref, h_ref, g1_ref, g2_ref, lg_ref, lb_ref, ys_ref, o_ref,
                       buf1, buf2, sem, *, tm):
    base = pl.program_id(0) * tm

    def issue(r, carry):
        pltpu.make_async_copy(ys_ref.at[pl.ds(p1_ref[base + r], 1)], buf1.at[pl.ds(r, 1)], sem.at[0]).start()
        pltpu.make_async_copy(ys_ref.at[pl.ds(p2_ref[base + r], 1)], buf2.at[pl.ds(r, 1)], sem.at[1]).start()
        return carry

    lax.fori_loop(0, tm, issue, 0)
    pltpu.make_async_copy(ys_ref.at[pl.ds(0, tm)], buf1, sem.at[0]).wait()
    pltpu.make_async_copy(ys_ref.at[pl.ds(0, tm)], buf2, sem.at[1]).wait()
    ffn = g1_ref[...] * buf1[...] + g2_ref[...] * buf2[...]
    o_ref[...] = _layer_norm(ALPHA * h_ref[...] + ffn, lg_ref[...], lb_ref[...])


def _combine_ln(h, ys, pos1, pos2, g1, g2, ln_g, ln_b, tm):
    t, d = h.shape
    row = lambda i, p1, p2: (i, 0)
    const = lambda i, p1, p2: (0, 0)
    return pl.pallas_call(
        functools.partial(_combine_ln_kernel, tm=tm),
        grid_spec=pltpu.PrefetchScalarGridSpec(
            num_scalar_prefetch=2,
            grid=(t // tm,),
            in_specs=[pl.BlockSpec((tm, d), row),
                      pl.BlockSpec((tm, 1), row),
                      pl.BlockSpec((tm, 1), row),
                      pl.BlockSpec((1, d), const),
                      pl.BlockSpec((1, d), const),
                      pl.BlockSpec(memory_space=pl.ANY)],
            out_specs=pl.BlockSpec((tm, d), row),
            scratch_shapes=[pltpu.VMEM((tm, d), F32), pltpu.VMEM((tm, d), F32),
                            pltpu.SemaphoreType.DMA((2,))],
        ),
        out_shape=jax.ShapeDtypeStruct((t, d), F32),
        compiler_params=_params(("arbitrary",)),
        name="moe_combine_ln",
    )(pos1, pos2, h, g1, g2, ln_g, ln_b, ys)


def _moe_block(h, e, gw, rk, counts, w_gate, w_up, w_down, layer, ln_g, ln_b, tm_tok):
    t, d = h.shape
    tm = MOE_TM
    n_rows = 2 * t + N_EXPERTS * tm
    n_tiles = n_rows // tm
    cnt = counts.astype(jnp.int32)
    padded = ((cnt + tm - 1) // tm) * tm
    ends = jnp.cumsum(padded)
    offs = ends - padded
    e_ids = jnp.arange(N_EXPERTS, dtype=jnp.int32)
    pos = jnp.sum(jnp.where(e[..., None] == e_ids, offs, 0), axis=-1) + rk
    n_active = jnp.maximum(ends[-1] // tm, 1)
    tile_start = jnp.minimum(jnp.arange(n_tiles, dtype=jnp.int32), n_active - 1) * tm
    tile_expert = jnp.minimum(jnp.sum((ends[None, :] <= tile_start[:, None]).astype(jnp.int32), axis=1),
                              N_EXPERTS - 1)
    last_tile = jnp.where(padded > 0, ends - tm, -1)
    tail = ends[-1] + jnp.arange(N_EXPERTS, dtype=jnp.int32) * tm
    fill_rows = jnp.concatenate([last_tile, jnp.where(tail < n_rows, tail, -1)]).astype(jnp.int32)
    xs = _dispatch(h, pos[0], pos[1], fill_rows, n_rows, tm_tok, tm)
    idx = jnp.arange(n_tiles, dtype=jnp.int32)
    is_first = jnp.concatenate([jnp.ones((1,), bool), tile_expert[1:] != tile_expert[:-1]])
    run_id = jnp.cumsum(is_first.astype(jnp.int32)) - 1
    run_start = lax.cummax(jnp.where(is_first, idx, 0))
    same_run = run_id[:, None] == run_id[None, :]
    run_len = jnp.sum((same_run & (idx[None, :] < n_active)).astype(jnp.int32), axis=1)
    nxt_tile = run_start + run_len
    nxt = jnp.sum(jnp.where(idx[None, :] == nxt_tile[:, None], tile_expert[None, :], 0), axis=1)
    has_next = nxt_tile < n_active
    k_in_run = idx - run_start
    lo = jnp.where(has_next, (MOE_W_GROUPS * k_in_run) // jnp.maximum(run_len, 1), 0)
    hi = jnp.where(has_next, (MOE_W_GROUPS * (k_in_run + 1)) // jnp.maximum(run_len, 1), 0)
    tables = [a.astype(jnp.int32) for a in (tile_expert, run_id % 2, jnp.where(has_next, nxt, 0), lo, hi)]
    ys = _moe_experts(xs, tables, n_active.reshape(1).astype(jnp.int32), w_gate, w_up, w_down, layer, tm)
    return _combine_ln(h, ys, pos[0], pos[1], gw[0].reshape(t, 1), gw[1].reshape(t, 1), ln_g, ln_b, tm_tok)


def kernel(x, ab_w_in, ab_lambda_re, ab_lambda_im, ab_log_dt, ab_b_re, ab_b_im, ab_c_re, ab_c_im, ab_d,
           ab_w_glu, ab_b_glu, ab_w_out, c_w_in, c_w_group, c_scale, c_w_out, ln_g, ln_b, router_w,
           router_b, moe_w_gate, moe_w_up, moe_w_down):
    bsz, seq, d = x.shape
    t = bsz * seq
    tm = min(ROW_TILE, t)
    assert t % tm == 0 and seq % tm == 0 and seq % S5_CHUNK == 0
    row = lambda a: a.reshape(1, -1)

    rw = jnp.zeros((d, LANES), F32).at[:, :N_EXPERTS].set(router_w)
    rw_hi = rw.astype(BF16)
    rw_lo = (rw - rw_hi.astype(F32)).astype(BF16)
    rb = jnp.zeros((1, LANES), F32).at[0, :N_EXPERTS].set(router_b)
    router = (rw_hi, rw_lo, rb)

    h = x.reshape(t, d)

    wg_b, wu_b, wd_b = moe_w_gate, moe_w_up, moe_w_down
    u, qkv = _in_proj_ab(h, ab_w_in[0].astype(BF16), min(2 * ROW_TILE, t))
    chunks = seq // S5_CHUNK
    nsteps = max(int(math.log2(chunks)), 0)
    assert (1 << nsteps) == chunks
    ops = _s5_operands(ab_lambda_re[0], ab_lambda_im[0], ab_log_dt[0], ab_b_re[0], ab_b_im[0],
                       ab_c_re[0], ab_c_im[0], S5_CHUNK, nsteps)
    y_ssm = _s5_scan(u, ops, bsz, seq)
    y_a = _s5_glu(y_ssm, u, ab_d[0], ab_w_glu[0], ab_b_glu[0], tm)
    y_b = _sb_attention(qkv.reshape(bsz, seq, 3 * SB_WIDTH), bsz, seq).reshape(t, SB_WIDTH)
    w_out = ab_w_out[0].astype(BF16)
    h, e, gw, rk, cnt = _proj_ln_route([y_a, y_b], [w_out[:S5_WIDTH], w_out[S5_WIDTH:]], h,
                                       row(ln_g[0, 0]), row(ln_b[0, 0]), router, tm)
    h = _moe_block(h, e, gw, rk, cnt, wg_b, wu_b, wd_b, 0, row(ln_g[0, 1]), row(ln_b[0, 1]), tm)

    pooled = _pool_in(h, c_w_in[0].astype(BF16), seq, tm)
    h, e, gw, rk, cnt = _proj_ln_route([pooled], [c_w_out[0].astype(BF16)], h,
                                       row(ln_g[1, 0]), row(ln_b[1, 0]), router, tm,
                                       grouped=(c_w_group[0].astype(BF16), row(c_scale[0])))
    h = _moe_block(h, e, gw, rk, cnt, wg_b, wu_b, wd_b, 1, row(ln_g[1, 1]), row(ln_b[1, 1]), tm)
    return h.reshape(bsz, seq, d)
```

```python
import functools
import math

import jax
import jax.numpy as jnp
from jax import lax
from jax.experimental import pallas as pl
from jax.experimental.pallas import tpu as pltpu

S5_WIDTH = 512
S5_GROUP = 16
S5_GROUPS = 32
S5_STATE = 64
SB_HEAD_DIM = 128
SB_HEADS = 12
SB_WIDTH = SB_HEADS * SB_HEAD_DIM
POOL_WINDOWS = (2, 4, 8, 16)
POOL_GROUP_WIDTH = 512
N_EXPERTS = 16
N_EXPERT_GROUPS = 4
EXPERTS_PER_GROUP = 4
DEPTH = 2
ALPHA = (2.0 * DEPTH) ** 0.25
LN_EPS = 1e-5

LANES = 128
VMEM_LIMIT = 56 * 1024 * 1024

S5_CHUNK = 32
ATT_TQ = 128
ATT_HEADS_PER_STEP = 2
ATT_TILES_PER_ITER = 2
ATT_LOG_UNDERFLOW = -88.0
MOE_TM = 256
MOE_W_GROUPS = 8
ROW_TILE = 512

BF16 = jnp.bfloat16
F32 = jnp.float32


def _params(sem, vmem=VMEM_LIMIT):
    return pltpu.CompilerParams(dimension_semantics=sem, vmem_limit_bytes=vmem)


def _piece_transpose(blocks):
    v = list(blocks)
    n = len(v)
    assert n * S5_GROUP == LANES
    piece = lax.broadcasted_iota(jnp.int32, v[0].shape, 1) // S5_GROUP
    d = n // 2
    while d >= 1:
        upper = (piece & d) != 0
        for q in range(n):
            if q & d == 0:
                a, b = v[q], v[q + d]
                v[q] = jnp.where(upper, pltpu.roll(b, d * S5_GROUP, 1), a)
                v[q + d] = jnp.where(upper, b, pltpu.roll(a, LANES - d * S5_GROUP, 1))
        d //= 2
    return v


def _in_proj_ab_kernel(x_ref, w_ref, u_ref, up_ref, qkv_ref, xb_ref, us_ref):
    j = pl.program_id(1)

    @pl.when(j == 0)
    def _():
        xb_ref[...] = x_ref[...].astype(BF16)

    acc = jnp.dot(xb_ref[...], w_ref[...], preferred_element_type=F32)

    @pl.when(j == 0)
    def _():
        u_ref[...] = acc
        chunk = S5_CHUNK
        n_chunks = acc.shape[0] // chunk
        per = LANES // S5_GROUP
        for g1 in range(S5_WIDTH // LANES):
            us_ref[g1] = acc[:, g1 * LANES:(g1 + 1) * LANES]
        for s1 in range(chunk // per):
            for g1 in range(S5_WIDTH // LANES):
                rows = [us_ref[g1, pl.ds(s1 * per + s2, n_chunks, stride=chunk), :] for s2 in range(per)]
                regrouped = _piece_transpose(rows)
                for g2 in range(per):
                    up_ref[g1 * per + g2, :, s1 * LANES:(s1 + 1) * LANES] = regrouped[g2].astype(up_ref.dtype)

    @pl.when(j > 0)
    def _():
        qkv_ref[...] = acc.astype(qkv_ref.dtype)


def _in_proj_ab(x, w, tm):
    m, k = x.shape
    tn = S5_WIDTH
    n_blocks = w.shape[1] // tn
    return pl.pallas_call(
        _in_proj_ab_kernel,
        grid=(m // tm, n_blocks),
        in_specs=[pl.BlockSpec((tm, k), lambda i, j: (i, 0)),
                  pl.BlockSpec((k, tn), lambda i, j: (0, j))],
        out_specs=[pl.BlockSpec((tm, tn), lambda i, j: (i, 0)),
                   pl.BlockSpec((S5_GROUPS, tm // S5_CHUNK, tn), lambda i, j: (0, i, 0)),
                   pl.BlockSpec((tm, tn), lambda i, j: (i, jnp.maximum(j - 1, 0)))],
        out_shape=[jax.ShapeDtypeStruct((m, tn), F32),
                   jax.ShapeDtypeStruct((S5_GROUPS, m // S5_CHUNK, tn), BF16),
                   jax.ShapeDtypeStruct((m, w.shape[1] - tn), BF16)],
        scratch_shapes=[pltpu.VMEM((tm, k), BF16), pltpu.VMEM((tn // LANES, tm, LANES), F32)],
        compiler_params=_params(("parallel", "arbitrary")),
        name="in_proj_ab",
    )(x, w)


def _bmm_f32_kernel(x_ref, y_ref, o_ref):
    o_ref[...] = jnp.dot(x_ref[...], y_ref[...], preferred_element_type=F32,
                         precision=lax.Precision.HIGHEST)


def _bmm_f32(x, y):
    g, m, k = x.shape
    n = y.shape[2]
    return pl.pallas_call(
        _bmm_f32_kernel,
        grid=(g,),
        in_specs=[pl.BlockSpec((None, m, k), lambda i: (i, 0, 0)),
                  pl.BlockSpec((None, k, n), lambda i: (i, 0, 0))],
        out_specs=pl.BlockSpec((None, m, n), lambda i: (i, 0, 0)),
        out_shape=jax.ShapeDtypeStruct((g, m, n), F32),
        compiler_params=_params(("parallel",)),
        name="s5_taps",
    )(x, y)


def _s5_kernel(u_ref, r0_ref, ws_ref, wo_ref, a1_ref, a2_ref, o_ref, wt_ref, *, chunks, nsteps):
    h = S5_GROUP
    r0 = r0_ref[...]
    lane = lax.broadcasted_iota(jnp.int32, r0.shape, 1)
    for j in range(r0.shape[1] // h):
        blk = r0 if j == 0 else jnp.where(lane >= j * h, pltpu.roll(r0, j * h, 1), 0.0)
        wt_ref[j * h:(j + 1) * h, :] = blk.astype(BF16)
    u = u_ref[...]
    y_intra = jnp.dot(u, wt_ref[...], preferred_element_type=F32)
    s = jnp.dot(u, ws_ref[...], preferred_element_type=F32)
    m = s.shape[0]
    c_idx = lax.broadcasted_iota(jnp.int32, s.shape, 0) % chunks
    half = s.shape[1] // 2
    for k in range(nsteps):
        d = 1 << k
        sh = jnp.where(c_idx >= d, pltpu.roll(s, d, 0), 0.0)
        s = s + a1_ref[k:k + 1, :] * sh + a2_ref[k:k + 1, :] * pltpu.roll(sh, half, 1)
    s_prev = jnp.where(c_idx >= 1, pltpu.roll(s, 1, 0), 0.0)
    y_inter = jnp.dot(s_prev.astype(BF16), wo_ref[...], preferred_element_type=F32)
    o_ref[...] = y_intra + y_inter


def _s5_operands(lam_re, lam_im, log_dt, b_re, b_im, c_re, c_im, chunk, nsteps):
    g, n = lam_re.shape
    h = S5_GROUP
    dt = jnp.exp(log_dt)[:, None]
    lr, li = lam_re, lam_im
    mag = jnp.exp(lr * dt)
    lbr = mag * jnp.cos(li * dt)
    lbi = mag * jnp.sin(li * dt)
    den = lr * lr + li * li
    coef_re = ((lbr - 1.0) * lr + lbi * li) / den
    coef_im = (lbi * lr - (lbr - 1.0) * li) / den
    bbar_re = coef_re[..., None] * b_re - coef_im[..., None] * b_im
    bbar_im = coef_re[..., None] * b_im + coef_im[..., None] * b_re

    def power(mult):
        mult = mult.astype(F32)[None, :, None]
        pm = jnp.exp(lr[:, None, :] * dt[:, None, :] * mult)
        ang = li[:, None, :] * dt[:, None, :] * mult
        return pm * jnp.cos(ang), pm * jnp.sin(ang)

    pr, pi = power(jnp.arange(chunk + 1))
    cre = jnp.swapaxes(c_re, 1, 2)
    cim = jnp.swapaxes(c_im, 1, 2)
    cb_re = cre[:, :, :, None] * bbar_re[:, :, None, :] - cim[:, :, :, None] * bbar_im[:, :, None, :]
    cb_im = cre[:, :, :, None] * bbar_im[:, :, None, :] + cim[:, :, :, None] * bbar_re[:, :, None, :]
    y_op = jnp.concatenate([cb_re, cb_im], axis=1).reshape(g, 2 * n, h * h)
    x_op = jnp.concatenate([pr[:, :chunk], -pi[:, :chunk]], axis=-1)
    taps = _bmm_f32(x_op, y_op).reshape(g, chunk, h, h)
    r0 = taps.transpose(0, 3, 1, 2).reshape(g, h, chunk * h)
    prr = pr[:, chunk - 1::-1][:, :chunk]
    pir = pi[:, chunk - 1::-1][:, :chunk]
    ws_re = prr[:, :, None, :] * jnp.swapaxes(bbar_re, 1, 2)[:, None] - pir[:, :, None, :] * jnp.swapaxes(bbar_im, 1, 2)[:, None]
    ws_im = prr[:, :, None, :] * jnp.swapaxes(bbar_im, 1, 2)[:, None] + pir[:, :, None, :] * jnp.swapaxes(bbar_re, 1, 2)[:, None]
    ws = jnp.concatenate([ws_re, ws_im], axis=-1).reshape(g, chunk * h, 2 * n)
    p1r = pr[:, 1:chunk + 1]
    p1i = pi[:, 1:chunk + 1]
    wo_re = c_re[:, None] * p1r[:, :, None, :] - c_im[:, None] * p1i[:, :, None, :]
    wo_im = -(c_re[:, None] * p1i[:, :, None, :] + c_im[:, None] * p1r[:, :, None, :])
    wo = jnp.concatenate([wo_re, wo_im], axis=-1)
    wo = wo.transpose(0, 3, 1, 2).reshape(g, 2 * n, chunk * h)
    sr, si = power(chunk * (2 ** jnp.arange(max(nsteps, 1))))
    a1 = jnp.concatenate([sr, sr], axis=-1)
    a2 = jnp.concatenate([-si, si], axis=-1)
    return r0, ws.astype(BF16), wo.astype(BF16), a1, a2


def _s5_scan(up, ops, bsz, seq):
    r0, ws, wo, a1, a2 = ops
    g, h, chunk = S5_GROUPS, S5_GROUP, S5_CHUNK
    chunks = seq // chunk
    m = bsz * chunks
    nsteps = a1.shape[1] if chunks > 1 else 0
    n2 = ws.shape[2]
    ns = a1.shape[1]
    return pl.pallas_call(
        functools.partial(_s5_kernel, chunks=chunks, nsteps=nsteps),
        grid=(g,),
        in_specs=[pl.BlockSpec((None, m, chunk * h), lambda i: (i, 0, 0)),
                  pl.BlockSpec((None, h, chunk * h), lambda i: (i, 0, 0)),
                  pl.BlockSpec((None, chunk * h, n2), lambda i: (i, 0, 0)),
                  pl.BlockSpec((None, n2, chunk * h), lambda i: (i, 0, 0)),
                  pl.BlockSpec((None, ns, n2), lambda i: (i, 0, 0)),
                  pl.BlockSpec((None, ns, n2), lambda i: (i, 0, 0))],
        out_specs=pl.BlockSpec((None, m, chunk * h), lambda i: (i, 0, 0)),
        out_shape=jax.ShapeDtypeStruct((g, m, chunk * h), F32),
        scratch_shapes=[pltpu.VMEM((chunk * h, chunk * h), BF16)],
        compiler_params=_params(("parallel",)),
        name="s5_scan",
    )(up, r0, ws, wo, a1, a2)


def _s5_glu_kernel(yp_ref, u_ref, d_ref, w_ref, b_ref, o_ref, y_ref):
    chunk = S5_CHUNK
    n_chunks = yp_ref.shape[1]
    per = LANES // S5_GROUP
    for s1 in range(chunk // per):
        for g1 in range(S5_WIDTH // LANES):
            by_step = _piece_transpose([yp_ref[g1 * per + g2, :, s1 * LANES:(s1 + 1) * LANES] for g2 in range(per)])
            for s2 in range(per):
                y_ref[g1, pl.ds(s1 * per + s2, n_chunks, stride=chunk), :] = by_step[s2]
    y_ssm = jnp.concatenate([y_ref[g1] for g1 in range(S5_WIDTH // LANES)], axis=1)
    y = y_ssm + d_ref[...] * u_ref[...]
    gl = jax.nn.gelu(y)
    gate = jnp.dot(gl.astype(BF16), w_ref[...], preferred_element_type=F32) + b_ref[...]
    o_ref[...] = (gl * jax.nn.sigmoid(gate)).astype(o_ref.dtype)


def _s5_glu(yp, u, d, w_glu, b_glu, tm):
    t, c = u.shape
    return pl.pallas_call(
        _s5_glu_kernel,
        grid=(t // tm,),
        in_specs=[pl.BlockSpec((S5_GROUPS, tm // S5_CHUNK, c), lambda i: (0, i, 0)),
                  pl.BlockSpec((tm, c), lambda i: (i, 0)),
                  pl.BlockSpec((1, c), lambda i: (0, 0)),
                  pl.BlockSpec((c, c), lambda i: (0, 0)),
                  pl.BlockSpec((1, c), lambda i: (0, 0))],
        out_specs=pl.BlockSpec((tm, c), lambda i: (i, 0)),
        out_shape=jax.ShapeDtypeStruct((t, c), BF16),
        scratch_shapes=[pltpu.VMEM((c // LANES, tm, LANES), F32)],
        compiler_params=_params(("parallel",)),
        name="s5_glu",
    )(yp, u, d.reshape(1, c), w_glu.astype(BF16), b_glu.reshape(1, c))


def _sb_steps(tiles, tri, scale):
    dh = SB_HEAD_DIM
    nk = tiles[0][1].shape[0]
    left = lax.broadcasted_iota(jnp.int32, (nk, 2 * dh), 1) < dh
    pre = []
    stack = []
    for q2, k2, v2, causal, _ in tiles:
        zero = jnp.zeros_like(k2)
        k_bd = jnp.concatenate([jnp.where(left, k2, zero), jnp.where(left, zero, k2)], axis=0)
        z2 = lax.dot_general(q2, k_bd, (((1,), (1,)), ((), ())), preferred_element_type=F32) * scale
        per_head = []
        for hh in range(2):
            z = z2[:, hh * nk:(hh + 1) * nk]
            sp = jnp.maximum(z, 0.0) + jnp.log(1.0 + jnp.exp(-jnp.abs(z)))
            log_keep = -sp if causal is None else jnp.where(causal, -sp, 0.0)
            hi = log_keep.astype(BF16)
            lo = (log_keep - hi.astype(F32)).astype(BF16)
            stack += [hi, lo]
            per_head.append((z - sp, log_keep))
        pre.append(per_head)
    tq = tiles[0][0].shape[0]
    sums = jnp.dot(jnp.concatenate(stack, axis=0), tri, preferred_element_type=F32)
    results = []
    for ti, (q2, k2, v2, causal, r_in) in enumerate(tiles):
        ws, rs = [], []
        for hh in range(2):
            log_beta, log_keep = pre[ti][hh]
            base = (ti * 2 + hh) * 2 * tq
            tail = sums[base:base + tq] + sums[base + tq:base + 2 * tq]
            if r_in is not None:
                tail = tail + r_in[hh]
            w = jnp.exp(log_beta + tail)
            if causal is not None:
                w = jnp.where(causal, w, 0.0)
            ws.append(w.astype(BF16))
            rs.append(jnp.sum(log_keep, axis=1, keepdims=True))
        zero = jnp.zeros_like(v2)
        v_bd = jnp.concatenate([jnp.where(left, v2, zero), jnp.where(left, zero, v2)], axis=0)
        out = jnp.dot(jnp.concatenate(ws, axis=1), v_bd, preferred_element_type=F32)
        results.append((out, rs))
    return results


def _sb_attn_kernel(q_ref, k_ref, v_ref, o_ref, acc_ref, r_ref, tri_ref, *, seq, tq, scale):
    dh = SB_HEAD_DIM
    win = 2 * tq
    row = lax.broadcasted_iota(jnp.int32, (win, win), 0)
    col = lax.broadcasted_iota(jnp.int32, (win, win), 1)
    tri_ref[...] = jnp.where(row > col, 1.0, 0.0).astype(BF16)
    assert ATT_HEADS_PER_STEP == 2

    def q_tiles(p, carry):
        first, where = [], []
        for ti in range(ATT_TILES_PER_ITER):
            i = p * ATT_TILES_PER_ITER + ti
            q0 = pl.multiple_of(i * tq, tq)
            kblk = jnp.maximum(i - 1, 0)
            k0 = pl.multiple_of(kblk * tq, tq)
            q_pos = q0 + lax.broadcasted_iota(jnp.int32, (tq, win), 0)
            causal = (k0 + lax.broadcasted_iota(jnp.int32, (tq, win), 1)) < q_pos
            first.append((q_ref[pl.ds(q0, tq), :], k_ref[pl.ds(k0, win), :], v_ref[pl.ds(k0, win), :], causal, None))
            where.append((q0, kblk))
        rmax = []
        for ti, (out, rs) in enumerate(_sb_steps(first, tri_ref[...], scale)):
            acc_ref[ti] = out
            r_ref[2 * ti] = rs[0]
            r_ref[2 * ti + 1] = rs[1]
            rmax.append(jnp.maximum(jnp.max(rs[0]), jnp.max(rs[1])))
        for ti, (q0, kblk) in enumerate(where):
            def cond(c):
                j, rm = c
                return jnp.logical_and(j >= 0, rm > ATT_LOG_UNDERFLOW)

            def body(c, ti=ti, q0=q0):
                j, _ = c
                kj = pl.multiple_of(j * tq, tq)
                r_old = [r_ref[2 * ti], r_ref[2 * ti + 1]]
                tile = (q_ref[pl.ds(q0, tq), :], k_ref[pl.ds(kj, tq), :], v_ref[pl.ds(kj, tq), :], None, r_old)
                (out, rs), = _sb_steps([tile], tri_ref[:tq, :tq], scale)
                acc_ref[ti] += out
                r_new = [r_old[0] + rs[0], r_old[1] + rs[1]]
                r_ref[2 * ti] = r_new[0]
                r_ref[2 * ti + 1] = r_new[1]
                return j - 1, jnp.maximum(jnp.max(r_new[0]), jnp.max(r_new[1]))

            lax.while_loop(cond, body, (kblk - 1, rmax[ti]))
            o_ref[pl.ds(q0, tq), :] = acc_ref[ti].astype(o_ref.dtype)
        return carry

    lax.fori_loop(0, seq // (tq * ATT_TILES_PER_ITER), q_tiles, 0)


def _sb_attention(qkv, bsz, seq):
    tq = ATT_TQ
    assert seq % (tq * ATT_TILES_PER_ITER) == 0 and seq >= 2 * tq
    wcols = ATT_HEADS_PER_STEP * SB_HEAD_DIM
    nhp = SB_WIDTH // wcols
    kern = functools.partial(_sb_attn_kernel, seq=seq, tq=tq, scale=1.0 / math.sqrt(SB_HEAD_DIM))
    return pl.pallas_call(
        kern,
        grid=(bsz, nhp),
        in_specs=[pl.BlockSpec((None, seq, wcols), lambda b, h: (b, 0, h)),
                  pl.BlockSpec((None, seq, wcols), lambda b, h: (b, 0, nhp + h)),
                  pl.BlockSpec((None, seq, wcols), lambda b, h: (b, 0, 2 * nhp + h))],
        out_specs=pl.BlockSpec((None, seq, wcols), lambda b, h: (b, 0, h)),
        out_shape=jax.ShapeDtypeStruct((bsz, seq, SB_WIDTH), BF16),
        scratch_shapes=[pltpu.VMEM((ATT_TILES_PER_ITER, tq, wcols), F32),
                        pltpu.VMEM((ATT_HEADS_PER_STEP * ATT_TILES_PER_ITER, tq, 1), F32),
                        pltpu.VMEM((2 * tq, 2 * tq), BF16)],
        compiler_params=_params(("parallel", "parallel")),
        name="sb_attention",
    )(qkv, qkv, qkv)


def _pool_in_kernel(x_ref, w_ref, o_ref, halo_ref, *, seq, tm):
    i = pl.program_id(0)
    hmax = POOL_WINDOWS[-1]
    acc = jnp.dot(x_ref[...].astype(BF16), w_ref[...], preferred_element_type=F32)
    t0 = (i * tm) % seq

    @pl.when(t0 == 0)
    def _():
        halo_ref[...] = jnp.zeros_like(halo_ref)

    ext = jnp.concatenate([halo_ref[...], acc], axis=0)
    t_idx = t0 + lax.broadcasted_iota(jnp.int32, (tm, POOL_GROUP_WIDTH), 0)
    for gi, win in enumerate(POOL_WINDOWS):
        cs = slice(gi * POOL_GROUP_WIDTH, (gi + 1) * POOL_GROUP_WIDTH)
        e = ext[:, cs]
        span = 1
        while span < win:
            e = e + pltpu.roll(e, span, 0)
            span *= 2
        wsum = e[hmax:, :]
        cnt = jnp.minimum(t_idx + 1, win).astype(F32)
        o_ref[:, cs] = (wsum / cnt - acc[:, cs]).astype(o_ref.dtype)
    halo_ref[...] = acc[tm - hmax:, :]


def _pool_in(x, w, seq, tm):
    t, k = x.shape
    n = w.shape[1]
    return pl.pallas_call(
        functools.partial(_pool_in_kernel, seq=seq, tm=tm),
        grid=(t // tm,),
        in_specs=[pl.BlockSpec((tm, k), lambda i: (i, 0)),
                  pl.BlockSpec((k, n), lambda i: (0, 0))],
        out_specs=pl.BlockSpec((tm, n), lambda i: (i, 0)),
        out_shape=jax.ShapeDtypeStruct((t, n), BF16),
        scratch_shapes=[pltpu.VMEM((POOL_WINDOWS[-1], n), F32)],
        compiler_params=_params(("arbitrary",)),
        name="pool_in",
    )(x, w)


def _layer_norm(x, g, b):
    mu = jnp.mean(x, axis=-1, keepdims=True)
    xc = x - mu
    var = jnp.mean(xc * xc, axis=-1, keepdims=True)
    return xc * lax.rsqrt(var + LN_EPS) * g + b


def _route(h, rwh_ref, rwl_ref, rb_ref, base_ref, tm):
    h_hi = h.astype(BF16)
    h_lo = (h - h_hi.astype(F32)).astype(BF16)
    rwh = rwh_ref[...]
    logits = (jnp.dot(h_hi, rwh, preferred_element_type=F32)
              + jnp.dot(h_lo, rwh, preferred_element_type=F32)
              + jnp.dot(h_hi, rwl_ref[...], preferred_element_type=F32)) + rb_ref[...]
    lt = logits.T[:N_EXPERTS, :]
    mx = jnp.max(lt, axis=0, keepdims=True)
    ex = jnp.exp(lt - mx)
    probs = ex / jnp.sum(ex, axis=0, keepdims=True)
    p = [probs[e:e + 1, :] for e in range(N_EXPERTS)]

    def first_max(vals):
        best, idx = vals[0], jnp.zeros_like(vals[0], dtype=jnp.int32)
        for n in range(1, len(vals)):
            upd = vals[n] > best
            idx = jnp.where(upd, n, idx)
            best = jnp.where(upd, vals[n], best)
        return best, idx

    scores = []
    for gi in range(N_EXPERT_GROUPS):
        p0, p1, p2, p3 = p[4 * gi:4 * gi + 4]
        a, b = jnp.maximum(p0, p1), jnp.minimum(p0, p1)
        c, d = jnp.maximum(p2, p3), jnp.minimum(p2, p3)
        scores.append(jnp.maximum(a, c) + jnp.maximum(jnp.minimum(a, c), jnp.maximum(b, d)))
    _, g_sel = first_max(scores)
    sel = []
    for n in range(EXPERTS_PER_GROUP):
        v = p[n]
        for gi in range(1, N_EXPERT_GROUPS):
            v = jnp.where(g_sel == gi, p[4 * gi + n], v)
        sel.append(v)
    v1, i1 = first_max(sel)
    v2, i2 = first_max([jnp.where(i1 == n, -1.0, sel[n]) for n in range(EXPERTS_PER_GROUP)])
    den = v1 + v2
    e1 = g_sel * EXPERTS_PER_GROUP + i1
    e2 = g_sel * EXPERTS_PER_GROUP + i2
    e_iota = lax.broadcasted_iota(jnp.int32, (N_EXPERTS, tm), 0)
    oh1 = e_iota == e1
    oh2 = e_iota == e2
    oh = jnp.where(jnp.logical_or(oh1, oh2), 1.0, 0.0)
    r_i = lax.broadcasted_iota(jnp.int32, (tm, tm), 0)
    c_i = lax.broadcasted_iota(jnp.int32, (tm, tm), 1)
    before = jnp.where(r_i < c_i, 1.0, 0.0).astype(BF16)
    cnt = jnp.dot(oh.astype(BF16), before, preferred_element_type=F32) + base_ref[...]
    rank1 = jnp.sum(jnp.where(oh1, cnt, 0.0), axis=0, keepdims=True)
    rank2 = jnp.sum(jnp.where(oh2, cnt, 0.0), axis=0, keepdims=True)
    base_ref[...] += jnp.sum(oh, axis=1, keepdims=True)
    e_out = jnp.concatenate([e1, e2], axis=0)
    w_out = jnp.concatenate([v1 / den, v2 / den], axis=0)
    r_out = jnp.concatenate([rank1, rank2], axis=0).astype(jnp.int32)
    return e_out, w_out, r_out


def _proj_ln_kernel(*refs, n_x, grouped, tm):
    xs = refs[:n_x]
    ws = refs[n_x:2 * n_x]
    pos = 2 * n_x
    if grouped:
        wg_ref, sc_ref = refs[pos:pos + 2]
        pos += 2
    res_ref, g_ref, b_ref, rwh_ref, rwl_ref, rb_ref = refs[pos:pos + 6]
    h_ref, e_ref, gw_ref, rk_ref, cnt_ref, base_ref = refs[pos + 6:pos + 12]

    @pl.when(pl.program_id(0) == 0)
    def _():
        base_ref[...] = jnp.zeros_like(base_ref)

    if grouped:
        x = xs[0][...]
        parts = []
        for gi in range(len(POOL_WINDOWS)):
            cs = slice(gi * POOL_GROUP_WIDTH, (gi + 1) * POOL_GROUP_WIDTH)
            parts.append(jnp.dot(x[:, cs], wg_ref[gi], preferred_element_type=F32))
        y = (jnp.concatenate(parts, axis=1) * sc_ref[...]).astype(BF16)
        mix = jnp.dot(y, ws[0][...], preferred_element_type=F32)
    else:
        mix = jnp.dot(xs[0][...], ws[0][...], preferred_element_type=F32)
        for n in range(1, n_x):
            mix = mix + jnp.dot(xs[n][...], ws[n][...], preferred_element_type=F32)
    h = _layer_norm(ALPHA * res_ref[...] + mix, g_ref[...], b_ref[...])
    h_ref[...] = h
    e_out, w_out, r_out = _route(h, rwh_ref, rwl_ref, rb_ref, base_ref, tm)
    e_ref[...] = e_out
    gw_ref[...] = w_out
    rk_ref[...] = r_out
    cnt_ref[...] = jnp.broadcast_to(base_ref[...], cnt_ref.shape)


def _proj_ln_route(xs, ws, res, ln_g, ln_b, router, tm, grouped=None):
    t, d = res.shape
    rwh, rwl, rb = router
    n_x = len(xs)
    full = lambda a: pl.BlockSpec(a.shape, lambda i, nd=a.ndim: (0,) * nd)
    in_specs = [pl.BlockSpec((tm, x.shape[1]), lambda i: (i, 0)) for x in xs]
    in_specs += [full(w) for w in ws]
    args = list(xs) + list(ws)
    if grouped is not None:
        in_specs += [full(a) for a in grouped]
        args += list(grouped)
    in_specs += [pl.BlockSpec((tm, d), lambda i: (i, 0)), full(ln_g), full(ln_b), full(rwh), full(rwl), full(rb)]
    args += [res, ln_g, ln_b, rwh, rwl, rb]
    pair = lambda dt: jax.ShapeDtypeStruct((2, t), dt)
    out = pl.pallas_call(
        functools.partial(_proj_ln_kernel, n_x=n_x, grouped=grouped is not None, tm=tm),
        grid=(t // tm,),
        in_specs=in_specs,
        out_specs=[pl.BlockSpec((tm, d), lambda i: (i, 0)),
                   pl.BlockSpec((2, tm), lambda i: (0, i)),
                   pl.BlockSpec((2, tm), lambda i: (0, i)),
                   pl.BlockSpec((2, tm), lambda i: (0, i)),
                   pl.BlockSpec((N_EXPERTS, LANES), lambda i: (0, 0))],
        out_shape=[jax.ShapeDtypeStruct((t, d), F32), pair(jnp.int32), pair(F32), pair(jnp.int32),
                   jax.ShapeDtypeStruct((N_EXPERTS, LANES), F32)],
        scratch_shapes=[pltpu.VMEM((N_EXPERTS, 1), F32)],
        compiler_params=_params(("arbitrary",)),
        name="proj_ln_route",
    )(*args)
    h, e, gw, rk, cnt = out
    return h, e, gw, rk, cnt[:, 0]


def _dispatch_kernel(p1_ref, p2_ref, fill_ref, h_ref, xs_ref, zero_ref, sem, *, tm, n_fill):
    base = pl.program_id(0) * tm

    @pl.when(pl.program_id(0) == 0)
    def _():
        zero_ref[...] = jnp.zeros_like(zero_ref)
        n_zero = zero_ref.shape[0]
        fill = lambda k: pltpu.make_async_copy(
            zero_ref, xs_ref.at[pl.ds(pl.multiple_of(jnp.maximum(fill_ref[k], 0), n_zero), n_zero)], sem.at[2])
        for k in range(n_fill):
            pl.when(fill_ref[k] >= 0)(lambda k=k: fill(k).start())
        for k in range(n_fill):
            pl.when(fill_ref[k] >= 0)(lambda k=k: fill(k).wait())

    def issue(r, carry):
        src = h_ref.at[pl.ds(r, 1)]
        pltpu.make_async_copy(src, xs_ref.at[pl.ds(p1_ref[base + r], 1)], sem.at[0]).start()
        pltpu.make_async_copy(src, xs_ref.at[pl.ds(p2_ref[base + r], 1)], sem.at[1]).start()
        return carry

    lax.fori_loop(0, tm, issue, 0)
    pltpu.make_async_copy(h_ref, xs_ref.at[pl.ds(0, tm)], sem.at[0]).wait()
    pltpu.make_async_copy(h_ref, xs_ref.at[pl.ds(0, tm)], sem.at[1]).wait()


def _dispatch(h, pos1, pos2, fill_rows, n_rows, tm, tm_fill):
    t, d = h.shape
    return pl.pallas_call(
        functools.partial(_dispatch_kernel, tm=tm, n_fill=fill_rows.shape[0]),
        grid_spec=pltpu.PrefetchScalarGridSpec(
            num_scalar_prefetch=3,
            grid=(t // tm,),
            in_specs=[pl.BlockSpec((tm, d), lambda i, p1, p2, fr: (i, 0))],
            out_specs=pl.BlockSpec(memory_space=pl.ANY),
            scratch_shapes=[pltpu.VMEM((tm_fill, d), h.dtype), pltpu.SemaphoreType.DMA((3,))],
        ),
        out_shape=jax.ShapeDtypeStruct((n_rows, d), h.dtype),
        compiler_params=_params(("arbitrary",)),
        name="moe_dispatch",
    )(pos1, pos2, fill_rows, h)


def _moe_kernel(te_ref, slot_ref, nxt_ref, lo_ref, hi_ref, na_ref, x_ref, wg_hbm, wu_hbm, wd_hbm, o_ref,
                wg_buf, wu_buf, wd_buf, st_g, st_u, st_d, sem, *, layer):
    i = pl.program_id(0)
    rg = wg_buf.shape[1] // MOE_W_GROUPS
    rd = wd_buf.shape[1] // MOE_W_GROUPS

    def copies(e, c, s):
        r0g = pl.multiple_of(c * rg, rg)
        r0d = pl.multiple_of(c * rd, rd)
        return (pltpu.make_async_copy(wg_hbm.at[layer, e, pl.ds(r0g, rg)], st_g.at[s], sem.at[s, 0]),
                pltpu.make_async_copy(wu_hbm.at[layer, e, pl.ds(r0g, rg)], st_u.at[s], sem.at[s, 1]),
                pltpu.make_async_copy(wd_hbm.at[layer, e, pl.ds(r0d, rd)], st_d.at[s], sem.at[s, 2]))

    def start(e, c):
        for cp in copies(e, c, c % 2):
            cp.start()

    def finish(e, c, w_slot):
        s = c % 2
        for cp in copies(e, c, s):
            cp.wait()
        wg_buf[w_slot, pl.ds(pl.multiple_of(c * rg, rg), rg), :] = st_g[s].astype(BF16)
        wu_buf[w_slot, pl.ds(pl.multiple_of(c * rg, rg), rg), :] = st_u[s].astype(BF16)
        wd_buf[w_slot, pl.ds(pl.multiple_of(c * rd, rd), rd), :] = st_d[s].astype(BF16)

    def fetch(e, lo, hi, w_slot):
        def step(c, carry):
            @pl.when(c + 1 < hi)
            def _():
                start(e, c + 1)
            finish(e, c, w_slot)
            return carry
        lax.fori_loop(lo, hi, step, 0)

    @pl.when(i == 0)
    def _():
        start(te_ref[0], 0)
        fetch(te_ref[0], 0, MOE_W_GROUPS, slot_ref[0])

    @pl.when(i < na_ref[0])
    def _():
        lo, hi, nxt, w_slot = lo_ref[i], hi_ref[i], nxt_ref[i], slot_ref[i]

        def issue_first(_, carry):
            start(nxt, lo)
            return carry
        lax.fori_loop(0, (hi > lo).astype(jnp.int32), issue_first, 0)

        x = x_ref[...].astype(BF16)
        gate = jnp.dot(x, wg_buf[w_slot], preferred_element_type=F32)
        up = jnp.dot(x, wu_buf[w_slot], preferred_element_type=F32)
        hid = (jax.nn.silu(gate) * up).astype(BF16)
        o_ref[...] = jnp.dot(hid, wd_buf[w_slot], preferred_element_type=F32)
        fetch(nxt, lo, hi, 1 - w_slot)

    @pl.when(i >= na_ref[0])
    def _():
        o_ref[...] = jnp.zeros_like(o_ref)


def _moe_experts(xs, tables, n_active, w_gate, w_up, w_down, layer, tm):
    p, d = xs.shape
    f = w_gate.shape[3]
    row_blk = lambda i, te, sl, nx, lo, hi, na: (jnp.minimum(i, na[0] - 1), 0)
    hbm = pl.BlockSpec(memory_space=pl.ANY)
    return pl.pallas_call(
        functools.partial(_moe_kernel, layer=layer),
        grid_spec=pltpu.PrefetchScalarGridSpec(
            num_scalar_prefetch=6,
            grid=(p // tm,),
            in_specs=[pl.BlockSpec((tm, d), row_blk), hbm, hbm, hbm],
            out_specs=pl.BlockSpec((tm, d), lambda i, te, sl, nx, lo, hi, na: (i, 0)),
            scratch_shapes=[pltpu.VMEM((2, d, f), BF16), pltpu.VMEM((2, d, f), BF16), pltpu.VMEM((2, f, d), BF16),
                            pltpu.VMEM((2, d // MOE_W_GROUPS, f), F32), pltpu.VMEM((2, d // MOE_W_GROUPS, f), F32),
                            pltpu.VMEM((2, f // MOE_W_GROUPS, d), F32), pltpu.SemaphoreType.DMA((2, 3))],
        ),
        out_shape=jax.ShapeDtypeStruct((p, d), F32),
        compiler_params=_params(("arbitrary",)),
        name="moe_experts",
    )(*tables, n_active, xs, w_gate, w_up, w_down)


def _combine_ln_kernel(p1_ref, p2_ref, h_ref, g1_ref, g2_ref, lg_ref, lb_ref, ys_ref, o_ref,
                       buf1, buf2, sem, *, tm):
    base = pl.program_id(0) * tm

    def issue(r, carry):
        pltpu.make_async_copy(ys_ref.at[pl.ds(p1_ref[base + r], 1)], buf1.at[pl.ds(r, 1)], sem.at[0]).start()
        pltpu.make_async_copy(ys_ref.at[pl.ds(p2_ref[base + r], 1)], buf2.at[pl.ds(r, 1)], sem.at[1]).start()
        return carry

    lax.fori_loop(0, tm, issue, 0)
    pltpu.make_async_copy(ys_ref.at[pl.ds(0, tm)], buf1, sem.at[0]).wait()
    pltpu.make_async_copy(ys_ref.at[pl.ds(0, tm)], buf2, sem.at[1]).wait()
    ffn = g1_ref[...] * buf1[...] + g2_ref[...] * buf2[...]
    o_ref[...] = _layer_norm(ALPHA * h_ref[...] + ffn, lg_ref[...], lb_ref[...])


def _combine_ln(h, ys, pos1, pos2, g1, g2, ln_g, ln_b, tm):
    t, d = h.shape
    row = lambda i, p1, p2: (i, 0)
    const = lambda i, p1, p2: (0, 0)
    return pl.pallas_call(
        functools.partial(_combine_ln_kernel, tm=tm),
        grid_spec=pltpu.PrefetchScalarGridSpec(
            num_scalar_prefetch=2,
            grid=(t // tm,),
            in_specs=[pl.BlockSpec((tm, d), row),
                      pl.BlockSpec((tm, 1), row),
                      pl.BlockSpec((tm, 1), row),
                      pl.BlockSpec((1, d), const),
                      pl.BlockSpec((1, d), const),
                      pl.BlockSpec(memory_space=pl.ANY)],
            out_specs=pl.BlockSpec((tm, d), row),
            scratch_shapes=[pltpu.VMEM((tm, d), F32), pltpu.VMEM((tm, d), F32),
                            pltpu.SemaphoreType.DMA((2,))],
        ),
        out_shape=jax.ShapeDtypeStruct((t, d), F32),
        compiler_params=_params(("arbitrary",)),
        name="moe_combine_ln",
    )(pos1, pos2, h, g1, g2, ln_g, ln_b, ys)


def _moe_block(h, e, gw, rk, counts, w_gate, w_up, w_down, layer, ln_g, ln_b, tm_tok):
    t, d = h.shape
    tm = MOE_TM
    n_rows = 2 * t + N_EXPERTS * tm
    n_tiles = n_rows // tm
    cnt = counts.astype(jnp.int32)
    padded = ((cnt + tm - 1) // tm) * tm
    ends = jnp.cumsum(padded)
    offs = ends - padded
    e_ids = jnp.arange(N_EXPERTS, dtype=jnp.int32)
    pos = jnp.sum(jnp.where(e[..., None] == e_ids, offs, 0), axis=-1) + rk
    n_active = jnp.maximum(ends[-1] // tm, 1)
    tile_start = jnp.minimum(jnp.arange(n_tiles, dtype=jnp.int32), n_active - 1) * tm
    tile_expert = jnp.minimum(jnp.sum((ends[None, :] <= tile_start[:, None]).astype(jnp.int32), axis=1),
                              N_EXPERTS - 1)
    last_tile = jnp.where(padded > 0, ends - tm, -1)
    tail = ends[-1] + jnp.arange(N_EXPERTS, dtype=jnp.int32) * tm
    fill_rows = jnp.concatenate([last_tile, jnp.where(tail < n_rows, tail, -1)]).astype(jnp.int32)
    xs = _dispatch(h, pos[0], pos[1], fill_rows, n_rows, tm_tok, tm)
    idx = jnp.arange(n_tiles, dtype=jnp.int32)
    is_first = jnp.concatenate([jnp.ones((1,), bool), tile_expert[1:] != tile_expert[:-1]])
    run_id = jnp.cumsum(is_first.astype(jnp.int32)) - 1
    run_start = lax.cummax(jnp.where(is_first, idx, 0))
    same_run = run_id[:, None] == run_id[None, :]
    run_len = jnp.sum((same_run & (idx[None, :] < n_active)).astype(jnp.int32), axis=1)
    nxt_tile = run_start + run_len
    nxt = jnp.sum(jnp.where(idx[None, :] == nxt_tile[:, None], tile_expert[None, :], 0), axis=1)
    has_next = nxt_tile < n_active
    k_in_run = idx - run_start
    lo = jnp.where(has_next, (MOE_W_GROUPS * k_in_run) // jnp.maximum(run_len, 1), 0)
    hi = jnp.where(has_next, (MOE_W_GROUPS * (k_in_run + 1)) // jnp.maximum(run_len, 1), 0)
    tables = [a.astype(jnp.int32) for a in (tile_expert, run_id % 2, jnp.where(has_next, nxt, 0), lo, hi)]
    ys = _moe_experts(xs, tables, n_active.reshape(1).astype(jnp.int32), w_gate, w_up, w_down, layer, tm)
    return _combine_ln(h, ys, pos[0], pos[1], gw[0].reshape(t, 1), gw[1].reshape(t, 1), ln_g, ln_b, tm_tok)


def kernel(x, ab_w_in, ab_lambda_re, ab_lambda_im, ab_log_dt, ab_b_re, ab_b_im, ab_c_re, ab_c_im, ab_d,
           ab_w_glu, ab_b_glu, ab_w_out, c_w_in, c_w_group, c_scale, c_w_out, ln_g, ln_b, router_w,
           router_b, moe_w_gate, moe_w_up, moe_w_down):
    bsz, seq, d = x.shape
    t = bsz * seq
    tm = min(ROW_TILE, t)
    assert t % tm == 0 and seq % tm == 0 and seq % S5_CHUNK == 0
    row = lambda a: a.reshape(1, -1)

    rw = jnp.zeros((d, LANES), F32).at[:, :N_EXPERTS].set(router_w)
    rw_hi = rw.astype(BF16)
    rw_lo = (rw - rw_hi.astype(F32)).astype(BF16)
    rb = jnp.zeros((1, LANES), F32).at[0, :N_EXPERTS].set(router_b)
    router = (rw_hi, rw_lo, rb)

    h = x.reshape(t, d)

    wg_b, wu_b, wd_b = moe_w_gate, moe_w_up, moe_w_down
    u, u_chunked, qkv = _in_proj_ab(h, ab_w_in[0].astype(BF16), min(2 * ROW_TILE, t))
    chunks = seq // S5_CHUNK
    nsteps = max(int(math.log2(chunks)), 0)
    assert (1 << nsteps) == chunks
    ops = _s5_operands(ab_lambda_re[0], ab_lambda_im[0], ab_log_dt[0], ab_b_re[0], ab_b_im[0],
                       ab_c_re[0], ab_c_im[0], S5_CHUNK, nsteps)
    y_ssm = _s5_scan(u_chunked, ops, bsz, seq)
    y_a = _s5_glu(y_ssm, u, ab_d[0], ab_w_glu[0], ab_b_glu[0], tm)
    y_b = _sb_attention(qkv.reshape(bsz, seq, 3 * SB_WIDTH), bsz, seq).reshape(t, SB_WIDTH)
    w_out = ab_w_out[0].astype(BF16)
    h, e, gw, rk, cnt = _proj_ln_route([y_a, y_b], [w_out[:S5_WIDTH], w_out[S5_WIDTH:]], h,
                                       row(ln_g[0, 0]), row(ln_b[0, 0]), router, tm)
    h = _moe_block(h, e, gw, rk, cnt, wg_b, wu_b, wd_b, 0, row(ln_g[0, 1]), row(ln_b[0, 1]), tm)

    pooled = _pool_in(h, c_w_in[0].astype(BF16), seq, tm)
    h, e, gw, rk, cnt = _proj_ln_route([pooled], [c_w_out[0].astype(BF16)], h,
                                       row(ln_g[1, 0]), row(ln_b[1, 0]), router, tm,
                                       grouped=(c_w_group[0].astype(BF16), row(c_scale[0])))
    h = _moe_block(h, e, gw, rk, cnt, wg_b, wu_b, wd_b, 1, row(ln_g[1, 1]), row(ln_b[1, 1]), tm)
    return h.reshape(bsz, seq, d)
```

```python
import functools
import math

import jax
import jax.numpy as jnp
from jax import lax
from jax.experimental import pallas as pl
from jax.experimental.pallas import tpu as pltpu

S5_WIDTH = 512
S5_GROUP = 16
S5_GROUPS = 32
S5_STATE = 64
SB_HEAD_DIM = 128
SB_HEADS = 12
SB_WIDTH = SB_HEADS * SB_HEAD_DIM
POOL_WINDOWS = (2, 4, 8, 16)
POOL_GROUP_WIDTH = 512
N_EXPERTS = 16
N_EXPERT_GROUPS = 4
EXPERTS_PER_GROUP = 4
DEPTH = 2
ALPHA = (2.0 * DEPTH) ** 0.25
LN_EPS = 1e-5

LANES = 128
VMEM_LIMIT = 56 * 1024 * 1024

S5_CHUNK = 32
ATT_TQ = 128
ATT_HEADS_PER_STEP = 2
ATT_TILES_PER_ITER = 2
ATT_LOG_UNDERFLOW = -88.0
MOE_TM = 256
MOE_W_GROUPS = 8
ROW_TILE = 512
PROJ_SLABS = 2
DMA_ISSUE_UNROLL = 8

BF16 = jnp.bfloat16
F32 = jnp.float32


def _params(sem, vmem=VMEM_LIMIT):
    return pltpu.CompilerParams(dimension_semantics=sem, vmem_limit_bytes=vmem)


def _piece_transpose(blocks):
    v = list(blocks)
    n = len(v)
    assert n * S5_GROUP == LANES
    piece = lax.broadcasted_iota(jnp.int32, v[0].shape, 1) // S5_GROUP
    d = n // 2
    while d >= 1:
        upper = (piece & d) != 0
        for q in range(n):
            if q & d == 0:
                a, b = v[q], v[q + d]
                v[q] = jnp.where(upper, pltpu.roll(b, d * S5_GROUP, 1), a)
                v[q + d] = jnp.where(upper, b, pltpu.roll(a, LANES - d * S5_GROUP, 1))
        d //= 2
    return v


def _in_proj_ab_kernel(x_ref, w_ref, u_ref, up_ref, qkv_ref, xb_ref, us_ref):
    j = pl.program_id(1)

    @pl.when(j == 0)
    def _():
        xb_ref[...] = x_ref[...].astype(BF16)

    acc = jnp.dot(xb_ref[...], w_ref[...], preferred_element_type=F32)

    @pl.when(j == 0)
    def _():
        u_ref[...] = acc
        chunk = S5_CHUNK
        n_chunks = acc.shape[0] // chunk
        per = LANES // S5_GROUP
        for g1 in range(S5_WIDTH // LANES):
            us_ref[g1] = acc[:, g1 * LANES:(g1 + 1) * LANES]
        for s1 in range(chunk // per):
            for g1 in range(S5_WIDTH // LANES):
                rows = [us_ref[g1, pl.ds(s1 * per + s2, n_chunks, stride=chunk), :] for s2 in range(per)]
                regrouped = _piece_transpose(rows)
                for g2 in range(per):
                    up_ref[g1 * per + g2, :, s1 * LANES:(s1 + 1) * LANES] = regrouped[g2].astype(up_ref.dtype)

    @pl.when(j > 0)
    def _():
        qkv_ref[...] = acc.astype(qkv_ref.dtype)


def _in_proj_ab(x, w, tm):
    m, k = x.shape
    tn = S5_WIDTH
    n_blocks = w.shape[1] // tn
    return pl.pallas_call(
        _in_proj_ab_kernel,
        grid=(m // tm, n_blocks),
        in_specs=[pl.BlockSpec((tm, k), lambda i, j: (i, 0)),
                  pl.BlockSpec((k, tn), lambda i, j: (0, j))],
        out_specs=[pl.BlockSpec((tm, tn), lambda i, j: (i, 0)),
                   pl.BlockSpec((S5_GROUPS, tm // S5_CHUNK, tn), lambda i, j: (0, i, 0)),
                   pl.BlockSpec((tm, tn), lambda i, j: (i, jnp.maximum(j - 1, 0)))],
        out_shape=[jax.ShapeDtypeStruct((m, tn), F32),
                   jax.ShapeDtypeStruct((S5_GROUPS, m // S5_CHUNK, tn), BF16),
                   jax.ShapeDtypeStruct((m, w.shape[1] - tn), BF16)],
        scratch_shapes=[pltpu.VMEM((tm, k), BF16), pltpu.VMEM((tn // LANES, tm, LANES), F32)],
        compiler_params=_params(("parallel", "arbitrary")),
        name="in_proj_ab",
    )(x, w)


def _bmm_f32_kernel(x_ref, y_ref, o_ref):
    o_ref[...] = jnp.dot(x_ref[...], y_ref[...], preferred_element_type=F32,
                         precision=lax.Precision.HIGHEST)


def _bmm_f32(x, y):
    g, m, k = x.shape
    n = y.shape[2]
    return pl.pallas_call(
        _bmm_f32_kernel,
        grid=(g,),
        in_specs=[pl.BlockSpec((None, m, k), lambda i: (i, 0, 0)),
                  pl.BlockSpec((None, k, n), lambda i: (i, 0, 0))],
        out_specs=pl.BlockSpec((None, m, n), lambda i: (i, 0, 0)),
        out_shape=jax.ShapeDtypeStruct((g, m, n), F32),
        compiler_params=_params(("parallel",)),
        name="s5_taps",
    )(x, y)


def _s5_kernel(u_ref, r0_ref, ws_ref, wo_ref, a1_ref, a2_ref, o_ref, wt_ref, *, chunks, nsteps):
    h = S5_GROUP
    r0 = r0_ref[...]
    lane = lax.broadcasted_iota(jnp.int32, r0.shape, 1)
    for j in range(r0.shape[1] // h):
        blk = r0 if j == 0 else jnp.where(lane >= j * h, pltpu.roll(r0, j * h, 1), 0.0)
        wt_ref[j * h:(j + 1) * h, :] = blk.astype(BF16)
    u = u_ref[...]
    y_intra = jnp.dot(u, wt_ref[...], preferred_element_type=F32)
    s = jnp.dot(u, ws_ref[...], preferred_element_type=F32)
    m = s.shape[0]
    c_idx = lax.broadcasted_iota(jnp.int32, s.shape, 0) % chunks
    half = s.shape[1] // 2
    for k in range(nsteps):
        d = 1 << k
        sh = jnp.where(c_idx >= d, pltpu.roll(s, d, 0), 0.0)
        s = s + a1_ref[k:k + 1, :] * sh + a2_ref[k:k + 1, :] * pltpu.roll(sh, half, 1)
    s_prev = jnp.where(c_idx >= 1, pltpu.roll(s, 1, 0), 0.0)
    y_inter = jnp.dot(s_prev.astype(BF16), wo_ref[...], preferred_element_type=F32)
    o_ref[...] = y_intra + y_inter


def _s5_operands(lam_re, lam_im, log_dt, b_re, b_im, c_re, c_im, chunk, nsteps):
    g, n = lam_re.shape
    h = S5_GROUP
    dt = jnp.exp(log_dt)[:, None]
    lr, li = lam_re, lam_im
    mag = jnp.exp(lr * dt)
    lbr = mag * jnp.cos(li * dt)
    lbi = mag * jnp.sin(li * dt)
    den = lr * lr + li * li
    coef_re = ((lbr - 1.0) * lr + lbi * li) / den
    coef_im = (lbi * lr - (lbr - 1.0) * li) / den
    bbar_re = coef_re[..., None] * b_re - coef_im[..., None] * b_im
    bbar_im = coef_re[..., None] * b_im + coef_im[..., None] * b_re

    def power(mult):
        mult = mult.astype(F32)[None, :, None]
        pm = jnp.exp(lr[:, None, :] * dt[:, None, :] * mult)
        ang = li[:, None, :] * dt[:, None, :] * mult
        return pm * jnp.cos(ang), pm * jnp.sin(ang)

    pr, pi = power(jnp.arange(chunk + 1))
    cre = jnp.swapaxes(c_re, 1, 2)
    cim = jnp.swapaxes(c_im, 1, 2)
    cb_re = cre[:, :, :, None] * bbar_re[:, :, None, :] - cim[:, :, :, None] * bbar_im[:, :, None, :]
    cb_im = cre[:, :, :, None] * bbar_im[:, :, None, :] + cim[:, :, :, None] * bbar_re[:, :, None, :]
    y_op = jnp.concatenate([cb_re, cb_im], axis=1).reshape(g, 2 * n, h * h)
    x_op = jnp.concatenate([pr[:, :chunk], -pi[:, :chunk]], axis=-1)
    taps = _bmm_f32(x_op, y_op).reshape(g, chunk, h, h)
    r0 = taps.transpose(0, 3, 1, 2).reshape(g, h, chunk * h)
    prr = pr[:, chunk - 1::-1][:, :chunk]
    pir = pi[:, chunk - 1::-1][:, :chunk]
    ws_re = prr[:, :, None, :] * jnp.swapaxes(bbar_re, 1, 2)[:, None] - pir[:, :, None, :] * jnp.swapaxes(bbar_im, 1, 2)[:, None]
    ws_im = prr[:, :, None, :] * jnp.swapaxes(bbar_im, 1, 2)[:, None] + pir[:, :, None, :] * jnp.swapaxes(bbar_re, 1, 2)[:, None]
    ws = jnp.concatenate([ws_re, ws_im], axis=-1).reshape(g, chunk * h, 2 * n)
    p1r = pr[:, 1:chunk + 1]
    p1i = pi[:, 1:chunk + 1]
    wo_re = c_re[:, None] * p1r[:, :, None, :] - c_im[:, None] * p1i[:, :, None, :]
    wo_im = -(c_re[:, None] * p1i[:, :, None, :] + c_im[:, None] * p1r[:, :, None, :])
    wo = jnp.concatenate([wo_re, wo_im], axis=-1)
    wo = wo.transpose(0, 3, 1, 2).reshape(g, 2 * n, chunk * h)
    sr, si = power(chunk * (2 ** jnp.arange(max(nsteps, 1))))
    a1 = jnp.concatenate([sr, sr], axis=-1)
    a2 = jnp.concatenate([-si, si], axis=-1)
    return r0, ws.astype(BF16), wo.astype(BF16), a1, a2


def _s5_scan(up, ops, bsz, seq):
    r0, ws, wo, a1, a2 = ops
    g, h, chunk = S5_GROUPS, S5_GROUP, S5_CHUNK
    chunks = seq // chunk
    m = bsz * chunks
    nsteps = a1.shape[1] if chunks > 1 else 0
    n2 = ws.shape[2]
    ns = a1.shape[1]
    return pl.pallas_call(
        functools.partial(_s5_kernel, chunks=chunks, nsteps=nsteps),
        grid=(g,),
        in_specs=[pl.BlockSpec((None, m, chunk * h), lambda i: (i, 0, 0)),
                  pl.BlockSpec((None, h, chunk * h), lambda i: (i, 0, 0)),
                  pl.BlockSpec((None, chunk * h, n2), lambda i: (i, 0, 0)),
                  pl.BlockSpec((None, n2, chunk * h), lambda i: (i, 0, 0)),
                  pl.BlockSpec((None, ns, n2), lambda i: (i, 0, 0)),
                  pl.BlockSpec((None, ns, n2), lambda i: (i, 0, 0))],
        out_specs=pl.BlockSpec((None, m, chunk * h), lambda i: (i, 0, 0)),
        out_shape=jax.ShapeDtypeStruct((g, m, chunk * h), F32),
        scratch_shapes=[pltpu.VMEM((chunk * h, chunk * h), BF16)],
        compiler_params=_params(("parallel",)),
        name="s5_scan",
    )(up, r0, ws, wo, a1, a2)


def _s5_glu_kernel(yp_ref, u_ref, d_ref, w_ref, b_ref, o_ref, y_ref):
    chunk = S5_CHUNK
    n_chunks = yp_ref.shape[1]
    per = LANES // S5_GROUP
    for s1 in range(chunk // per):
        for g1 in range(S5_WIDTH // LANES):
            by_step = _piece_transpose([yp_ref[g1 * per + g2, :, s1 * LANES:(s1 + 1) * LANES] for g2 in range(per)])
            for s2 in range(per):
                y_ref[g1, pl.ds(s1 * per + s2, n_chunks, stride=chunk), :] = by_step[s2]
    y_ssm = jnp.concatenate([y_ref[g1] for g1 in range(S5_WIDTH // LANES)], axis=1)
    y = y_ssm + d_ref[...] * u_ref[...]
    gl = jax.nn.gelu(y)
    gate = jnp.dot(gl.astype(BF16), w_ref[...], preferred_element_type=F32) + b_ref[...]
    o_ref[...] = (gl * jax.nn.sigmoid(gate)).astype(o_ref.dtype)


def _s5_glu(yp, u, d, w_glu, b_glu, tm):
    t, c = u.shape
    return pl.pallas_call(
        _s5_glu_kernel,
        grid=(t // tm,),
        in_specs=[pl.BlockSpec((S5_GROUPS, tm // S5_CHUNK, c), lambda i: (0, i, 0)),
                  pl.BlockSpec((tm, c), lambda i: (i, 0)),
                  pl.BlockSpec((1, c), lambda i: (0, 0)),
                  pl.BlockSpec((c, c), lambda i: (0, 0)),
                  pl.BlockSpec((1, c), lambda i: (0, 0))],
        out_specs=pl.BlockSpec((tm, c), lambda i: (i, 0)),
        out_shape=jax.ShapeDtypeStruct((t, c), BF16),
        scratch_shapes=[pltpu.VMEM((c // LANES, tm, LANES), F32)],
        compiler_params=_params(("parallel",)),
        name="s5_glu",
    )(yp, u, d.reshape(1, c), w_glu.astype(BF16), b_glu.reshape(1, c))


def _sb_steps(tiles, tri, scale):
    dh = SB_HEAD_DIM
    nk = tiles[0][1].shape[0]
    left = lax.broadcasted_iota(jnp.int32, (nk, 2 * dh), 1) < dh
    pre = []
    stack = []
    for q2, k2, v2, causal, _ in tiles:
        zero = jnp.zeros_like(k2)
        k_bd = jnp.concatenate([jnp.where(left, k2, zero), jnp.where(left, zero, k2)], axis=0)
        z2 = lax.dot_general(q2, k_bd, (((1,), (1,)), ((), ())), preferred_element_type=F32) * scale
        per_head = []
        for hh in range(2):
            z = z2[:, hh * nk:(hh + 1) * nk]
            sp = jnp.maximum(z, 0.0) + jnp.log(1.0 + jnp.exp(-jnp.abs(z)))
            log_keep = -sp if causal is None else jnp.where(causal, -sp, 0.0)
            hi = log_keep.astype(BF16)
            lo = (log_keep - hi.astype(F32)).astype(BF16)
            stack += [hi, lo]
            per_head.append((z - sp, log_keep))
        pre.append(per_head)
    tq = tiles[0][0].shape[0]
    sums = jnp.dot(jnp.concatenate(stack, axis=0), tri, preferred_element_type=F32)
    results = []
    for ti, (q2, k2, v2, causal, r_in) in enumerate(tiles):
        ws, rs = [], []
        for hh in range(2):
            log_beta, log_keep = pre[ti][hh]
            base = (ti * 2 + hh) * 2 * tq
            tail = sums[base:base + tq] + sums[base + tq:base + 2 * tq]
            if r_in is not None:
                tail = tail + r_in[hh]
            w = jnp.exp(log_beta + tail)
            if causal is not None:
                w = jnp.where(causal, w, 0.0)
            ws.append(w.astype(BF16))
            rs.append(jnp.sum(log_keep, axis=1, keepdims=True))
        zero = jnp.zeros_like(v2)
        v_bd = jnp.concatenate([jnp.where(left, v2, zero), jnp.where(left, zero, v2)], axis=0)
        out = jnp.dot(jnp.concatenate(ws, axis=1), v_bd, preferred_element_type=F32)
        results.append((out, rs))
    return results


def _sb_attn_kernel(q_ref, k_ref, v_ref, o_ref, acc_ref, r_ref, tri_ref, *, seq, tq, scale):
    dh = SB_HEAD_DIM
    win = 2 * tq
    row = lax.broadcasted_iota(jnp.int32, (win, win), 0)
    col = lax.broadcasted_iota(jnp.int32, (win, win), 1)
    tri_ref[...] = jnp.where(row > col, 1.0, 0.0).astype(BF16)
    assert ATT_HEADS_PER_STEP == 2

    def q_tiles(p, carry):
        first, where = [], []
        for ti in range(ATT_TILES_PER_ITER):
            i = p * ATT_TILES_PER_ITER + ti
            q0 = pl.multiple_of(i * tq, tq)
            kblk = jnp.maximum(i - 1, 0)
            k0 = pl.multiple_of(kblk * tq, tq)
            q_pos = q0 + lax.broadcasted_iota(jnp.int32, (tq, win), 0)
            causal = (k0 + lax.broadcasted_iota(jnp.int32, (tq, win), 1)) < q_pos
            first.append((q_ref[pl.ds(q0, tq), :], k_ref[pl.ds(k0, win), :], v_ref[pl.ds(k0, win), :], causal, None))
            where.append((q0, kblk))
        rmax = []
        for ti, (out, rs) in enumerate(_sb_steps(first, tri_ref[...], scale)):
            acc_ref[ti] = out
            r_ref[2 * ti] = rs[0]
            r_ref[2 * ti + 1] = rs[1]
            rmax.append(jnp.maximum(jnp.max(rs[0]), jnp.max(rs[1])))
        for ti, (q0, kblk) in enumerate(where):
            def cond(c):
                j, rm = c
                return jnp.logical_and(j >= 0, rm > ATT_LOG_UNDERFLOW)

            def body(c, ti=ti, q0=q0):
                j, _ = c
                kj = pl.multiple_of(j * tq, tq)
                r_old = [r_ref[2 * ti], r_ref[2 * ti + 1]]
                tile = (q_ref[pl.ds(q0, tq), :], k_ref[pl.ds(kj, tq), :], v_ref[pl.ds(kj, tq), :], None, r_old)
                (out, rs), = _sb_steps([tile], tri_ref[:tq, :tq], scale)
                acc_ref[ti] += out
                r_new = [r_old[0] + rs[0], r_old[1] + rs[1]]
                r_ref[2 * ti] = r_new[0]
                r_ref[2 * ti + 1] = r_new[1]
                return j - 1, jnp.maximum(jnp.max(r_new[0]), jnp.max(r_new[1]))

            lax.while_loop(cond, body, (kblk - 1, rmax[ti]))
            o_ref[pl.ds(q0, tq), :] = acc_ref[ti].astype(o_ref.dtype)
        return carry

    lax.fori_loop(0, seq // (tq * ATT_TILES_PER_ITER), q_tiles, 0)


def _sb_attention(qkv, bsz, seq):
    tq = ATT_TQ
    assert seq % (tq * ATT_TILES_PER_ITER) == 0 and seq >= 2 * tq
    wcols = ATT_HEADS_PER_STEP * SB_HEAD_DIM
    nhp = SB_WIDTH // wcols
    kern = functools.partial(_sb_attn_kernel, seq=seq, tq=tq, scale=1.0 / math.sqrt(SB_HEAD_DIM))
    return pl.pallas_call(
        kern,
        grid=(bsz, nhp),
        in_specs=[pl.BlockSpec((None, seq, wcols), lambda b, h: (b, 0, h)),
                  pl.BlockSpec((None, seq, wcols), lambda b, h: (b, 0, nhp + h)),
                  pl.BlockSpec((None, seq, wcols), lambda b, h: (b, 0, 2 * nhp + h))],
        out_specs=pl.BlockSpec((None, seq, wcols), lambda b, h: (b, 0, h)),
        out_shape=jax.ShapeDtypeStruct((bsz, seq, SB_WIDTH), BF16),
        scratch_shapes=[pltpu.VMEM((ATT_TILES_PER_ITER, tq, wcols), F32),
                        pltpu.VMEM((ATT_HEADS_PER_STEP * ATT_TILES_PER_ITER, tq, 1), F32),
                        pltpu.VMEM((2 * tq, 2 * tq), BF16)],
        compiler_params=_params(("parallel", "parallel")),
        name="sb_attention",
    )(qkv, qkv, qkv)


def _pool_in_kernel(x_ref, w_ref, o_ref, halo_ref, *, seq, tm):
    i = pl.program_id(0)
    hmax = POOL_WINDOWS[-1]
    acc = jnp.dot(x_ref[...].astype(BF16), w_ref[...], preferred_element_type=F32)
    t0 = (i * tm) % seq

    @pl.when(t0 == 0)
    def _():
        halo_ref[...] = jnp.zeros_like(halo_ref)

    ext = jnp.concatenate([halo_ref[...], acc], axis=0)
    t_idx = t0 + lax.broadcasted_iota(jnp.int32, (tm, POOL_GROUP_WIDTH), 0)
    for gi, win in enumerate(POOL_WINDOWS):
        cs = slice(gi * POOL_GROUP_WIDTH, (gi + 1) * POOL_GROUP_WIDTH)
        e = ext[:, cs]
        span = 1
        while span < win:
            e = e + pltpu.roll(e, span, 0)
            span *= 2
        wsum = e[hmax:, :]
        cnt = jnp.minimum(t_idx + 1, win).astype(F32)
        o_ref[:, cs] = (wsum / cnt - acc[:, cs]).astype(o_ref.dtype)
    halo_ref[...] = acc[tm - hmax:, :]


def _pool_in(x, w, seq, tm):
    t, k = x.shape
    n = w.shape[1]
    return pl.pallas_call(
        functools.partial(_pool_in_kernel, seq=seq, tm=tm),
        grid=(t // tm,),
        in_specs=[pl.BlockSpec((tm, k), lambda i: (i, 0)),
                  pl.BlockSpec((k, n), lambda i: (0, 0))],
        out_specs=pl.BlockSpec((tm, n), lambda i: (i, 0)),
        out_shape=jax.ShapeDtypeStruct((t, n), BF16),
        scratch_shapes=[pltpu.VMEM((POOL_WINDOWS[-1], n), F32)],
        compiler_params=_params(("arbitrary",)),
        name="pool_in",
    )(x, w)


def _layer_norm(x, g, b):
    mu = jnp.mean(x, axis=-1, keepdims=True)
    xc = x - mu
    var = jnp.mean(xc * xc, axis=-1, keepdims=True)
    return xc * lax.rsqrt(var + LN_EPS) * g + b


def _route(h, rwh_ref, rb_ref, base_ref, tm):
    h_hi = h.astype(BF16)
    h_lo = (h - h_hi.astype(F32)).astype(BF16)
    both = jnp.dot(h_hi, rwh_ref[...], preferred_element_type=F32)
    logits = (both[:, :LANES] + both[:, LANES:]
              + jnp.dot(h_lo, rwh_ref[:, :LANES], preferred_element_type=F32)) + rb_ref[...]
    lt = logits.T[:N_EXPERTS, :]
    mx = jnp.max(lt, axis=0, keepdims=True)
    ex = jnp.exp(lt - mx)
    probs = ex / jnp.sum(ex, axis=0, keepdims=True)
    p = [probs[e:e + 1, :] for e in range(N_EXPERTS)]

    def first_max(vals):
        best, idx = vals[0], jnp.zeros_like(vals[0], dtype=jnp.int32)
        for n in range(1, len(vals)):
            upd = vals[n] > best
            idx = jnp.where(upd, n, idx)
            best = jnp.where(upd, vals[n], best)
        return best, idx

    scores = []
    for gi in range(N_EXPERT_GROUPS):
        p0, p1, p2, p3 = p[4 * gi:4 * gi + 4]
        a, b = jnp.maximum(p0, p1), jnp.minimum(p0, p1)
        c, d = jnp.maximum(p2, p3), jnp.minimum(p2, p3)
        scores.append(jnp.maximum(a, c) + jnp.maximum(jnp.minimum(a, c), jnp.maximum(b, d)))
    _, g_sel = first_max(scores)
    sel = []
    for n in range(EXPERTS_PER_GROUP):
        v = p[n]
        for gi in range(1, N_EXPERT_GROUPS):
            v = jnp.where(g_sel == gi, p[4 * gi + n], v)
        sel.append(v)
    v1, i1 = first_max(sel)
    v2, i2 = first_max([jnp.where(i1 == n, -1.0, sel[n]) for n in range(EXPERTS_PER_GROUP)])
    den = v1 + v2
    e1 = g_sel * EXPERTS_PER_GROUP + i1
    e2 = g_sel * EXPERTS_PER_GROUP + i2
    e_iota = lax.broadcasted_iota(jnp.int32, (N_EXPERTS, tm), 0)
    oh1 = e_iota == e1
    oh2 = e_iota == e2
    oh = jnp.where(jnp.logical_or(oh1, oh2), 1.0, 0.0)
    r_i = lax.broadcasted_iota(jnp.int32, (tm, tm), 0)
    c_i = lax.broadcasted_iota(jnp.int32, (tm, tm), 1)
    before = jnp.where(r_i < c_i, 1.0, 0.0).astype(BF16)
    cnt = jnp.dot(oh.astype(BF16), before, preferred_element_type=F32) + base_ref[...]
    rank1 = jnp.sum(jnp.where(oh1, cnt, 0.0), axis=0, keepdims=True)
    rank2 = jnp.sum(jnp.where(oh2, cnt, 0.0), axis=0, keepdims=True)
    base_ref[...] += jnp.sum(oh, axis=1, keepdims=True)
    e_out = jnp.concatenate([e1, e2], axis=0)
    w_out = jnp.concatenate([v1 / den, v2 / den], axis=0)
    r_out = jnp.concatenate([rank1, rank2], axis=0).astype(jnp.int32)
    return e_out, w_out, r_out


def _proj_ln_kernel(*refs, n_x, grouped, tm):
    xs = refs[:n_x]
    ws = refs[n_x:2 * n_x]
    pos = 2 * n_x
    if grouped:
        wg_ref, sc_ref = refs[pos:pos + 2]
        pos += 2
    res_ref, g_ref, b_ref, rwh_ref, rb_ref = refs[pos:pos + 5]
    h_ref, e_ref, gw_ref, rk_ref, cnt_ref, base_ref = refs[pos + 5:pos + 11]

    @pl.when(pl.program_id(0) == 0)
    def _():
        base_ref[...] = jnp.zeros_like(base_ref)

    sm = tm // PROJ_SLABS
    for sl in range(PROJ_SLABS):
        rs = slice(sl * sm, (sl + 1) * sm)
        if grouped:
            x = xs[0][rs, :]
            parts = []
            for gi in range(len(POOL_WINDOWS)):
                cs = slice(gi * POOL_GROUP_WIDTH, (gi + 1) * POOL_GROUP_WIDTH)
                parts.append(jnp.dot(x[:, cs], wg_ref[gi], preferred_element_type=F32))
            y = (jnp.concatenate(parts, axis=1) * sc_ref[...]).astype(BF16)
            mix = jnp.dot(y, ws[0][...], preferred_element_type=F32)
        else:
            mix = jnp.dot(xs[0][rs, :], ws[0][...], preferred_element_type=F32)
            for n in range(1, n_x):
                mix = mix + jnp.dot(xs[n][rs, :], ws[n][...], preferred_element_type=F32)
        h = _layer_norm(ALPHA * res_ref[rs, :] + mix, g_ref[...], b_ref[...])
        h_ref[rs, :] = h
        e_out, w_out, r_out = _route(h, rwh_ref, rb_ref, base_ref, sm)
        e_ref[:, rs] = e_out
        gw_ref[:, rs] = w_out
        rk_ref[:, rs] = r_out
    cnt_ref[...] = jnp.broadcast_to(base_ref[...], cnt_ref.shape)


def _proj_ln_route(xs, ws, res, ln_g, ln_b, router, tm, grouped=None):
    t, d = res.shape
    rwh, rb = router
    n_x = len(xs)
    full = lambda a: pl.BlockSpec(a.shape, lambda i, nd=a.ndim: (0,) * nd)
    in_specs = [pl.BlockSpec((tm, x.shape[1]), lambda i: (i, 0)) for x in xs]
    in_specs += [full(w) for w in ws]
    args = list(xs) + list(ws)
    if grouped is not None:
        in_specs += [full(a) for a in grouped]
        args += list(grouped)
    in_specs += [pl.BlockSpec((tm, d), lambda i: (i, 0)), full(ln_g), full(ln_b), full(rwh), full(rb)]
    args += [res, ln_g, ln_b, rwh, rb]
    pair = lambda dt: jax.ShapeDtypeStruct((2, t), dt)
    out = pl.pallas_call(
        functools.partial(_proj_ln_kernel, n_x=n_x, grouped=grouped is not None, tm=tm),
        grid=(t // tm,),
        in_specs=in_specs,
        out_specs=[pl.BlockSpec((tm, d), lambda i: (i, 0)),
                   pl.BlockSpec((2, tm), lambda i: (0, i)),
                   pl.BlockSpec((2, tm), lambda i: (0, i)),
                   pl.BlockSpec((2, tm), lambda i: (0, i)),
                   pl.BlockSpec((N_EXPERTS, LANES), lambda i: (0, 0))],
        out_shape=[jax.ShapeDtypeStruct((t, d), F32), pair(jnp.int32), pair(F32), pair(jnp.int32),
                   jax.ShapeDtypeStruct((N_EXPERTS, LANES), F32)],
        scratch_shapes=[pltpu.VMEM((N_EXPERTS, 1), F32)],
        compiler_params=_params(("arbitrary",)),
        name="proj_ln_route",
    )(*args)
    h, e, gw, rk, cnt = out
    return h, e, gw, rk, cnt[:, 0]


def _dispatch_kernel(p1_ref, p2_ref, fill_ref, h_ref, xs_ref, zero_ref, sem, *, tm, n_fill):
    base = pl.program_id(0) * tm

    @pl.when(pl.program_id(0) == 0)
    def _():
        zero_ref[...] = jnp.zeros_like(zero_ref)
        n_zero = zero_ref.shape[0]
        fill = lambda k: pltpu.make_async_copy(
            zero_ref, xs_ref.at[pl.ds(pl.multiple_of(jnp.maximum(fill_ref[k], 0), n_zero), n_zero)], sem.at[2])
        for k in range(n_fill):
            pl.when(fill_ref[k] >= 0)(lambda k=k: fill(k).start())
        for k in range(n_fill):
            pl.when(fill_ref[k] >= 0)(lambda k=k: fill(k).wait())

    def issue(r, carry):
        src = h_ref.at[pl.ds(r, 1)]
        pltpu.make_async_copy(src, xs_ref.at[pl.ds(p1_ref[base + r], 1)], sem.at[0]).start()
        pltpu.make_async_copy(src, xs_ref.at[pl.ds(p2_ref[base + r], 1)], sem.at[1]).start()
        return carry

    lax.fori_loop(0, tm, issue, 0, unroll=DMA_ISSUE_UNROLL)
    pltpu.make_async_copy(h_ref, xs_ref.at[pl.ds(0, tm)], sem.at[0]).wait()
    pltpu.make_async_copy(h_ref, xs_ref.at[pl.ds(0, tm)], sem.at[1]).wait()


def _dispatch(h, pos1, pos2, fill_rows, n_rows, tm, tm_fill):
    t, d = h.shape
    return pl.pallas_call(
        functools.partial(_dispatch_kernel, tm=tm, n_fill=fill_rows.shape[0]),
        grid_spec=pltpu.PrefetchScalarGridSpec(
            num_scalar_prefetch=3,
            grid=(t // tm,),
            in_specs=[pl.BlockSpec((tm, d), lambda i, p1, p2, fr: (i, 0))],
            out_specs=pl.BlockSpec(memory_space=pl.ANY),
            scratch_shapes=[pltpu.VMEM((tm_fill, d), h.dtype), pltpu.SemaphoreType.DMA((3,))],
        ),
        out_shape=jax.ShapeDtypeStruct((n_rows, d), h.dtype),
        compiler_params=_params(("arbitrary",)),
        name="moe_dispatch",
    )(pos1, pos2, fill_rows, h)


def _moe_kernel(te_ref, slot_ref, nxt_ref, lo_ref, hi_ref, na_ref, x_ref, wg_hbm, wu_hbm, wd_hbm, o_ref,
                wg_buf, wu_buf, wd_buf, st_g, st_u, st_d, sem, *, layer):
    i = pl.program_id(0)
    rg = wg_buf.shape[1] // MOE_W_GROUPS
    rd = wd_buf.shape[1] // MOE_W_GROUPS

    def copies(e, c, s):
        r0g = pl.multiple_of(c * rg, rg)
        r0d = pl.multiple_of(c * rd, rd)
        return (pltpu.make_async_copy(wg_hbm.at[layer, e, pl.ds(r0g, rg)], st_g.at[s], sem.at[s, 0]),
                pltpu.make_async_copy(wu_hbm.at[layer, e, pl.ds(r0g, rg)], st_u.at[s], sem.at[s, 1]),
                pltpu.make_async_copy(wd_hbm.at[layer, e, pl.ds(r0d, rd)], st_d.at[s], sem.at[s, 2]))

    def start(e, c):
        for cp in copies(e, c, c % 2):
            cp.start()

    def finish(e, c, w_slot):
        s = c % 2
        for cp in copies(e, c, s):
            cp.wait()
        wg_buf[w_slot, pl.ds(pl.multiple_of(c * rg, rg), rg), :] = st_g[s].astype(BF16)
        wu_buf[w_slot, pl.ds(pl.multiple_of(c * rg, rg), rg), :] = st_u[s].astype(BF16)
        wd_buf[w_slot, pl.ds(pl.multiple_of(c * rd, rd), rd), :] = st_d[s].astype(BF16)

    def fetch(e, lo, hi, w_slot):
        def step(c, carry):
            @pl.when(c + 1 < hi)
            def _():
                start(e, c + 1)
            finish(e, c, w_slot)
            return carry
        lax.fori_loop(lo, hi, step, 0)

    @pl.when(i == 0)
    def _():
        start(te_ref[0], 0)
        fetch(te_ref[0], 0, MOE_W_GROUPS, slot_ref[0])

    @pl.when(i < na_ref[0])
    def _():
        lo, hi, nxt, w_slot = lo_ref[i], hi_ref[i], nxt_ref[i], slot_ref[i]

        def issue_first(_, carry):
            start(nxt, lo)
            return carry
        lax.fori_loop(0, (hi > lo).astype(jnp.int32), issue_first, 0)

        x = x_ref[...].astype(BF16)
        gate = jnp.dot(x, wg_buf[w_slot], preferred_element_type=F32)
        up = jnp.dot(x, wu_buf[w_slot], preferred_element_type=F32)
        hid = (jax.nn.silu(gate) * up).astype(BF16)
        o_ref[...] = jnp.dot(hid, wd_buf[w_slot], preferred_element_type=F32)
        fetch(nxt, lo, hi, 1 - w_slot)

    @pl.when(i >= na_ref[0])
    def _():
        o_ref[...] = jnp.zeros_like(o_ref)


def _moe_experts(xs, tables, n_active, w_gate, w_up, w_down, layer, tm):
    p, d = xs.shape
    f = w_gate.shape[3]
    row_blk = lambda i, te, sl, nx, lo, hi, na: (jnp.minimum(i, na[0] - 1), 0)
    hbm = pl.BlockSpec(memory_space=pl.ANY)
    return pl.pallas_call(
        functools.partial(_moe_kernel, layer=layer),
        grid_spec=pltpu.PrefetchScalarGridSpec(
            num_scalar_prefetch=6,
            grid=(p // tm,),
            in_specs=[pl.BlockSpec((tm, d), row_blk), hbm, hbm, hbm],
            out_specs=pl.BlockSpec((tm, d), lambda i, te, sl, nx, lo, hi, na: (i, 0)),
            scratch_shapes=[pltpu.VMEM((2, d, f), BF16), pltpu.VMEM((2, d, f), BF16), pltpu.VMEM((2, f, d), BF16),
                            pltpu.VMEM((2, d // MOE_W_GROUPS, f), F32), pltpu.VMEM((2, d // MOE_W_GROUPS, f), F32),
                            pltpu.VMEM((2, f // MOE_W_GROUPS, d), F32), pltpu.SemaphoreType.DMA((2, 3))],
        ),
        out_shape=jax.ShapeDtypeStruct((p, d), F32),
        compiler_params=_params(("arbitrary",)),
        name="moe_experts",
    )(*tables, n_active, xs, w_gate, w_up, w_down)


def _combine_ln_kernel(p1_ref, p2_ref, h_ref, g1_ref, g2_ref, lg_ref, lb_ref, ys_ref, o_ref,
                       buf1, buf2, sem, *, tm):
    base = pl.program_id(0) * tm

    def issue(r, carry):
        pltpu.make_async_copy(ys_ref.at[pl.ds(p1_ref[base + r], 1)], buf1.at[pl.ds(r, 1)], sem.at[0]).start()
        pltpu.make_async_copy(ys_ref.at[pl.ds(p2_ref[base + r], 1)], buf2.at[pl.ds(r, 1)], sem.at[1]).start()
        return carry

    lax.fori_loop(0, tm, issue, 0, unroll=DMA_ISSUE_UNROLL)
    pltpu.make_async_copy(ys_ref.at[pl.ds(0, tm)], buf1, sem.at[0]).wait()
    pltpu.make_async_copy(ys_ref.at[pl.ds(0, tm)], buf2, sem.at[1]).wait()
    ffn = g1_ref[...] * buf1[...] + g2_ref[...] * buf2[...]
    o_ref[...] = _layer_norm(ALPHA * h_ref[...] + ffn, lg_ref[...], lb_ref[...])


def _combine_ln(h, ys, pos1, pos2, g1, g2, ln_g, ln_b, tm):
    t, d = h.shape
    row = lambda i, p1, p2: (i, 0)
    const = lambda i, p1, p2: (0, 0)
    return pl.pallas_call(
        functools.partial(_combine_ln_kernel, tm=tm),
        grid_spec=pltpu.PrefetchScalarGridSpec(
            num_scalar_prefetch=2,
            grid=(t // tm,),
            in_specs=[pl.BlockSpec((tm, d), row),
                      pl.BlockSpec((tm, 1), row),
                      pl.BlockSpec((tm, 1), row),
                      pl.BlockSpec((1, d), const),
                      pl.BlockSpec((1, d), const),
                      pl.BlockSpec(memory_space=pl.ANY)],
            out_specs=pl.BlockSpec((tm, d), row),
            scratch_shapes=[pltpu.VMEM((tm, d), F32), pltpu.VMEM((tm, d), F32),
                            pltpu.SemaphoreType.DMA((2,))],
        ),
        out_shape=jax.ShapeDtypeStruct((t, d), F32),
        compiler_params=_params(("arbitrary",)),
        name="moe_combine_ln",
    )(pos1, pos2, h, g1, g2, ln_g, ln_b, ys)


def _moe_block(h, e, gw, rk, counts, w_gate, w_up, w_down, layer, ln_g, ln_b, tm_tok):
    t, d = h.shape
    tm = MOE_TM
    n_rows = 2 * t + N_EXPERTS * tm
    n_tiles = n_rows // tm
    cnt = counts.astype(jnp.int32)
    padded = ((cnt + tm - 1) // tm) * tm
    ends = jnp.cumsum(padded)
    offs = ends - padded
    e_ids = jnp.arange(N_EXPERTS, dtype=jnp.int32)
    pos = jnp.sum(jnp.where(e[..., None] == e_ids, offs, 0), axis=-1) + rk
    n_active = jnp.maximum(ends[-1] // tm, 1)
    tile_start = jnp.minimum(jnp.arange(n_tiles, dtype=jnp.int32), n_active - 1) * tm
    tile_expert = jnp.minimum(jnp.sum((ends[None, :] <= tile_start[:, None]).astype(jnp.int32), axis=1),
                              N_EXPERTS - 1)
    last_tile = jnp.where(padded > 0, ends - tm, -1)
    tail = ends[-1] + jnp.arange(N_EXPERTS, dtype=jnp.int32) * tm
    fill_rows = jnp.concatenate([last_tile, jnp.where(tail < n_rows, tail, -1)]).astype(jnp.int32)
    xs = _dispatch(h, pos[0], pos[1], fill_rows, n_rows, tm_tok, tm)
    idx = jnp.arange(n_tiles, dtype=jnp.int32)
    is_first = jnp.concatenate([jnp.ones((1,), bool), tile_expert[1:] != tile_expert[:-1]])
    run_id = jnp.cumsum(is_first.astype(jnp.int32)) - 1
    run_start = lax.cummax(jnp.where(is_first, idx, 0))
    same_run = run_id[:, None] == run_id[None, :]
    run_len = jnp.sum((same_run & (idx[None, :] < n_active)).astype(jnp.int32), axis=1)
    nxt_tile = run_start + run_len
    nxt = jnp.sum(jnp.where(idx[None, :] == nxt_tile[:, None], tile_expert[None, :], 0), axis=1)
    has_next = nxt_tile < n_active
    k_in_run = idx - run_start
    lo = jnp.where(has_next, (MOE_W_GROUPS * k_in_run) // jnp.maximum(run_len, 1), 0)
    hi = jnp.where(has_next, (MOE_W_GROUPS * (k_in_run + 1)) // jnp.maximum(run_len, 1), 0)
    tables = [a.astype(jnp.int32) for a in (tile_expert, run_id % 2, jnp.where(has_next, nxt, 0), lo, hi)]
    ys = _moe_experts(xs, tables, n_active.reshape(1).astype(jnp.int32), w_gate, w_up, w_down, layer, tm)
    return _combine_ln(h, ys, pos[0], pos[1], gw[0].reshape(t, 1), gw[1].reshape(t, 1), ln_g, ln_b, tm_tok)


def kernel(x, ab_w_in, ab_lambda_re, ab_lambda_im, ab_log_dt, ab_b_re, ab_b_im, ab_c_re, ab_c_im, ab_d,
           ab_w_glu, ab_b_glu, ab_w_out, c_w_in, c_w_group, c_scale, c_w_out, ln_g, ln_b, router_w,
           router_b, moe_w_gate, moe_w_up, moe_w_down):
    bsz, seq, d = x.shape
    t = bsz * seq
    tm = min(ROW_TILE, t)
    assert t % tm == 0 and seq % tm == 0 and seq % S5_CHUNK == 0
    row = lambda a: a.reshape(1, -1)

    rw = jnp.zeros((d, LANES), F32).at[:, :N_EXPERTS].set(router_w)
    rw_hi = rw.astype(BF16)
    rw_lo = (rw - rw_hi.astype(F32)).astype(BF16)
    rb = jnp.zeros((1, LANES), F32).at[0, :N_EXPERTS].set(router_b)
    router = (jnp.concatenate([rw_hi, rw_lo], axis=1), rb)

    h = x.reshape(t, d)

    wg_b, wu_b, wd_b = moe_w_gate, moe_w_up, moe_w_down
    u, u_chunked, qkv = _in_proj_ab(h, ab_w_in[0].astype(BF16), min(2 * ROW_TILE, t))
    chunks = seq // S5_CHUNK
    nsteps = max(int(math.log2(chunks)), 0)
    assert (1 << nsteps) == chunks
    ops = _s5_operands(ab_lambda_re[0], ab_lambda_im[0], ab_log_dt[0], ab_b_re[0], ab_b_im[0],
                       ab_c_re[0], ab_c_im[0], S5_CHUNK, nsteps)
    y_ssm = _s5_scan(u_chunked, ops, bsz, seq)
    y_a = _s5_glu(y_ssm, u, ab_d[0], ab_w_glu[0], ab_b_glu[0], tm)
    y_b = _sb_attention(qkv.reshape(bsz, seq, 3 * SB_WIDTH), bsz, seq).reshape(t, SB_WIDTH)
    w_out = ab_w_out[0].astype(BF16)
    h, e, gw, rk, cnt = _proj_ln_route([y_a, y_b], [w_out[:S5_WIDTH], w_out[S5_WIDTH:]], h,
                                       row(ln_g[0, 0]), row(ln_b[0, 0]), router, tm)
    h = _moe_block(h, e, gw, rk, cnt, wg_b, wu_b, wd_b, 0, row(ln_g[0, 1]), row(ln_b[0, 1]), tm)

    pooled = _pool_in(h, c_w_in[0].astype(BF16), seq, tm)
    h, e, gw, rk, cnt = _proj_ln_route([pooled], [c_w_out[0].astype(BF16)], h,
                                       row(ln_g[1, 0]), row(ln_b[1, 0]), router, tm,
                                       grouped=(c_w_group[0].astype(BF16), row(c_scale[0])))
    h = _moe_block(h, e, gw, rk, cnt, wg_b, wu_b, wd_b, 1, row(ln_g[1, 1]), row(ln_b[1, 1]), tm)
    return h.reshape(bsz, seq, d)
```

```python
import functools
import math

import jax
import jax.numpy as jnp
from jax import lax
from jax.experimental import pallas as pl
from jax.experimental.pallas import tpu as pltpu

S5_WIDTH = 512
S5_GROUP = 16
S5_GROUPS = 32
S5_STATE = 64
SB_HEAD_DIM = 128
SB_HEADS = 12
SB_WIDTH = SB_HEADS * SB_HEAD_DIM
POOL_WINDOWS = (2, 4, 8, 16)
POOL_GROUP_WIDTH = 512
N_EXPERTS = 16
N_EXPERT_GROUPS = 4
EXPERTS_PER_GROUP = 4
DEPTH = 2
ALPHA = (2.0 * DEPTH) ** 0.25
LN_EPS = 1e-5

LANES = 128
VMEM_LIMIT = 56 * 1024 * 1024

S5_CHUNK = 32
ATT_TQ = 128
ATT_HEADS_PER_STEP = 2
ATT_TILES_PER_ITER = 2
ATT_LOG_UNDERFLOW = -88.0
MOE_TM = 256
MOE_W_GROUPS = 8
ROW_TILE = 512
PROJ_SLABS = 2
DMA_ISSUE_UNROLL = 8
COMBINE_ROWS_PER_TRIP = 64

BF16 = jnp.bfloat16
F32 = jnp.float32


def _params(sem, vmem=VMEM_LIMIT):
    return pltpu.CompilerParams(dimension_semantics=sem, vmem_limit_bytes=vmem)


def _piece_transpose(blocks):
    v = list(blocks)
    n = len(v)
    assert n * S5_GROUP == LANES
    piece = lax.broadcasted_iota(jnp.int32, v[0].shape, 1) // S5_GROUP
    d = n // 2
    while d >= 1:
        upper = (piece & d) != 0
        for q in range(n):
            if q & d == 0:
                a, b = v[q], v[q + d]
                v[q] = jnp.where(upper, pltpu.roll(b, d * S5_GROUP, 1), a)
                v[q + d] = jnp.where(upper, b, pltpu.roll(a, LANES - d * S5_GROUP, 1))
        d //= 2
    return v


def _in_proj_ab_kernel(x_ref, w_ref, u_ref, up_ref, qkv_ref, xb_ref, us_ref):
    j = pl.program_id(1)

    @pl.when(j == 0)
    def _():
        xb_ref[...] = x_ref[...].astype(BF16)

    acc = jnp.dot(xb_ref[...], w_ref[...], preferred_element_type=F32)

    @pl.when(j == 0)
    def _():
        u_ref[...] = acc
        chunk = S5_CHUNK
        n_chunks = acc.shape[0] // chunk
        per = LANES // S5_GROUP
        for g1 in range(S5_WIDTH // LANES):
            us_ref[g1] = acc[:, g1 * LANES:(g1 + 1) * LANES]
        for s1 in range(chunk // per):
            for g1 in range(S5_WIDTH // LANES):
                rows = [us_ref[g1, pl.ds(s1 * per + s2, n_chunks, stride=chunk), :] for s2 in range(per)]
                regrouped = _piece_transpose(rows)
                for g2 in range(per):
                    up_ref[g1 * per + g2, :, s1 * LANES:(s1 + 1) * LANES] = regrouped[g2].astype(up_ref.dtype)

    @pl.when(j > 0)
    def _():
        qkv_ref[...] = acc.astype(qkv_ref.dtype)


def _in_proj_ab(x, w, tm):
    m, k = x.shape
    tn = S5_WIDTH
    n_blocks = w.shape[1] // tn
    return pl.pallas_call(
        _in_proj_ab_kernel,
        grid=(m // tm, n_blocks),
        in_specs=[pl.BlockSpec((tm, k), lambda i, j: (i, 0)),
                  pl.BlockSpec((k, tn), lambda i, j: (0, j))],
        out_specs=[pl.BlockSpec((tm, tn), lambda i, j: (i, 0)),
                   pl.BlockSpec((S5_GROUPS, tm // S5_CHUNK, tn), lambda i, j: (0, i, 0)),
                   pl.BlockSpec((tm, tn), lambda i, j: (i, jnp.maximum(j - 1, 0)))],
        out_shape=[jax.ShapeDtypeStruct((m, tn), F32),
                   jax.ShapeDtypeStruct((S5_GROUPS, m // S5_CHUNK, tn), BF16),
                   jax.ShapeDtypeStruct((m, w.shape[1] - tn), BF16)],
        scratch_shapes=[pltpu.VMEM((tm, k), BF16), pltpu.VMEM((tn // LANES, tm, LANES), F32)],
        compiler_params=_params(("parallel", "arbitrary")),
        name="in_proj_ab",
    )(x, w)


def _bmm_f32_kernel(x_ref, y_ref, o_ref):
    o_ref[...] = jnp.dot(x_ref[...], y_ref[...], preferred_element_type=F32,
                         precision=lax.Precision.HIGHEST)


def _bmm_f32(x, y):
    g, m, k = x.shape
    n = y.shape[2]
    return pl.pallas_call(
        _bmm_f32_kernel,
        grid=(g,),
        in_specs=[pl.BlockSpec((None, m, k), lambda i: (i, 0, 0)),
                  pl.BlockSpec((None, k, n), lambda i: (i, 0, 0))],
        out_specs=pl.BlockSpec((None, m, n), lambda i: (i, 0, 0)),
        out_shape=jax.ShapeDtypeStruct((g, m, n), F32),
        compiler_params=_params(("parallel",)),
        name="s5_taps",
    )(x, y)


def _s5_kernel(u_ref, r0_ref, ws_ref, wo_ref, a1_ref, a2_ref, o_ref, wt_ref, *, chunks, nsteps):
    h = S5_GROUP
    r0 = r0_ref[...]
    lane = lax.broadcasted_iota(jnp.int32, r0.shape, 1)
    for j in range(r0.shape[1] // h):
        blk = r0 if j == 0 else jnp.where(lane >= j * h, pltpu.roll(r0, j * h, 1), 0.0)
        wt_ref[j * h:(j + 1) * h, :] = blk.astype(BF16)
    u = u_ref[...]
    y_intra = jnp.dot(u, wt_ref[...], preferred_element_type=F32)
    s = jnp.dot(u, ws_ref[...], preferred_element_type=F32)
    m = s.shape[0]
    c_idx = lax.broadcasted_iota(jnp.int32, s.shape, 0) % chunks
    half = s.shape[1] // 2
    for k in range(nsteps):
        d = 1 << k
        sh = jnp.where(c_idx >= d, pltpu.roll(s, d, 0), 0.0)
        s = s + a1_ref[k:k + 1, :] * sh + a2_ref[k:k + 1, :] * pltpu.roll(sh, half, 1)
    s_prev = jnp.where(c_idx >= 1, pltpu.roll(s, 1, 0), 0.0)
    y_inter = jnp.dot(s_prev.astype(BF16), wo_ref[...], preferred_element_type=F32)
    o_ref[...] = y_intra + y_inter


def _s5_operands(lam_re, lam_im, log_dt, b_re, b_im, c_re, c_im, chunk, nsteps):
    g, n = lam_re.shape
    h = S5_GROUP
    dt = jnp.exp(log_dt)[:, None]
    lr, li = lam_re, lam_im
    mag = jnp.exp(lr * dt)
    lbr = mag * jnp.cos(li * dt)
    lbi = mag * jnp.sin(li * dt)
    den = lr * lr + li * li
    coef_re = ((lbr - 1.0) * lr + lbi * li) / den
    coef_im = (lbi * lr - (lbr - 1.0) * li) / den
    bbar_re = coef_re[..., None] * b_re - coef_im[..., None] * b_im
    bbar_im = coef_re[..., None] * b_im + coef_im[..., None] * b_re

    def power(mult):
        mult = mult.astype(F32)[None, :, None]
        pm = jnp.exp(lr[:, None, :] * dt[:, None, :] * mult)
        ang = li[:, None, :] * dt[:, None, :] * mult
        return pm * jnp.cos(ang), pm * jnp.sin(ang)

    pr, pi = power(jnp.arange(chunk + 1))
    cre = jnp.swapaxes(c_re, 1, 2)
    cim = jnp.swapaxes(c_im, 1, 2)
    cb_re = cre[:, :, :, None] * bbar_re[:, :, None, :] - cim[:, :, :, None] * bbar_im[:, :, None, :]
    cb_im = cre[:, :, :, None] * bbar_im[:, :, None, :] + cim[:, :, :, None] * bbar_re[:, :, None, :]
    y_op = jnp.concatenate([cb_re, cb_im], axis=1).reshape(g, 2 * n, h * h)
    x_op = jnp.concatenate([pr[:, :chunk], -pi[:, :chunk]], axis=-1)
    taps = _bmm_f32(x_op, y_op).reshape(g, chunk, h, h)
    r0 = taps.transpose(0, 3, 1, 2).reshape(g, h, chunk * h)
    prr = pr[:, chunk - 1::-1][:, :chunk]
    pir = pi[:, chunk - 1::-1][:, :chunk]
    ws_re = prr[:, :, None, :] * jnp.swapaxes(bbar_re, 1, 2)[:, None] - pir[:, :, None, :] * jnp.swapaxes(bbar_im, 1, 2)[:, None]
    ws_im = prr[:, :, None, :] * jnp.swapaxes(bbar_im, 1, 2)[:, None] + pir[:, :, None, :] * jnp.swapaxes(bbar_re, 1, 2)[:, None]
    ws = jnp.concatenate([ws_re, ws_im], axis=-1).reshape(g, chunk * h, 2 * n)
    p1r = pr[:, 1:chunk + 1]
    p1i = pi[:, 1:chunk + 1]
    wo_re = c_re[:, None] * p1r[:, :, None, :] - c_im[:, None] * p1i[:, :, None, :]
    wo_im = -(c_re[:, None] * p1i[:, :, None, :] + c_im[:, None] * p1r[:, :, None, :])
    wo = jnp.concatenate([wo_re, wo_im], axis=-1)
    wo = wo.transpose(0, 3, 1, 2).reshape(g, 2 * n, chunk * h)
    sr, si = power(chunk * (2 ** jnp.arange(max(nsteps, 1))))
    a1 = jnp.concatenate([sr, sr], axis=-1)
    a2 = jnp.concatenate([-si, si], axis=-1)
    return r0, ws.astype(BF16), wo.astype(BF16), a1, a2


def _s5_scan(up, ops, bsz, seq):
    r0, ws, wo, a1, a2 = ops
    g, h, chunk = S5_GROUPS, S5_GROUP, S5_CHUNK
    chunks = seq // chunk
    m = bsz * chunks
    nsteps = a1.shape[1] if chunks > 1 else 0
    n2 = ws.shape[2]
    ns = a1.shape[1]
    return pl.pallas_call(
        functools.partial(_s5_kernel, chunks=chunks, nsteps=nsteps),
        grid=(g,),
        in_specs=[pl.BlockSpec((None, m, chunk * h), lambda i: (i, 0, 0)),
                  pl.BlockSpec((None, h, chunk * h), lambda i: (i, 0, 0)),
                  pl.BlockSpec((None, chunk * h, n2), lambda i: (i, 0, 0)),
                  pl.BlockSpec((None, n2, chunk * h), lambda i: (i, 0, 0)),
                  pl.BlockSpec((None, ns, n2), lambda i: (i, 0, 0)),
                  pl.BlockSpec((None, ns, n2), lambda i: (i, 0, 0))],
        out_specs=pl.BlockSpec((None, m, chunk * h), lambda i: (i, 0, 0)),
        out_shape=jax.ShapeDtypeStruct((g, m, chunk * h), F32),
        scratch_shapes=[pltpu.VMEM((chunk * h, chunk * h), BF16)],
        compiler_params=_params(("parallel",)),
        name="s5_scan",
    )(up, r0, ws, wo, a1, a2)


def _s5_glu_kernel(yp_ref, u_ref, d_ref, w_ref, b_ref, o_ref, y_ref):
    chunk = S5_CHUNK
    n_chunks = yp_ref.shape[1]
    per = LANES // S5_GROUP
    for s1 in range(chunk // per):
        for g1 in range(S5_WIDTH // LANES):
            by_step = _piece_transpose([yp_ref[g1 * per + g2, :, s1 * LANES:(s1 + 1) * LANES] for g2 in range(per)])
            for s2 in range(per):
                y_ref[g1, pl.ds(s1 * per + s2, n_chunks, stride=chunk), :] = by_step[s2]
    y_ssm = jnp.concatenate([y_ref[g1] for g1 in range(S5_WIDTH // LANES)], axis=1)
    y = y_ssm + d_ref[...] * u_ref[...]
    gl = jax.nn.gelu(y)
    gate = jnp.dot(gl.astype(BF16), w_ref[...], preferred_element_type=F32) + b_ref[...]
    o_ref[...] = (gl * jax.nn.sigmoid(gate)).astype(o_ref.dtype)


def _s5_glu(yp, u, d, w_glu, b_glu, tm):
    t, c = u.shape
    return pl.pallas_call(
        _s5_glu_kernel,
        grid=(t // tm,),
        in_specs=[pl.BlockSpec((S5_GROUPS, tm // S5_CHUNK, c), lambda i: (0, i, 0)),
                  pl.BlockSpec((tm, c), lambda i: (i, 0)),
                  pl.BlockSpec((1, c), lambda i: (0, 0)),
                  pl.BlockSpec((c, c), lambda i: (0, 0)),
                  pl.BlockSpec((1, c), lambda i: (0, 0))],
        out_specs=pl.BlockSpec((tm, c), lambda i: (i, 0)),
        out_shape=jax.ShapeDtypeStruct((t, c), BF16),
        scratch_shapes=[pltpu.VMEM((c // LANES, tm, LANES), F32)],
        compiler_params=_params(("parallel",)),
        name="s5_glu",
    )(yp, u, d.reshape(1, c), w_glu.astype(BF16), b_glu.reshape(1, c))


def _sb_steps(tiles, tri, scale):
    dh = SB_HEAD_DIM
    nk = tiles[0][1].shape[0]
    left = lax.broadcasted_iota(jnp.int32, (nk, 2 * dh), 1) < dh
    pre = []
    stack = []
    for q2, k2, v2, causal, _ in tiles:
        zero = jnp.zeros_like(k2)
        k_bd = jnp.concatenate([jnp.where(left, k2, zero), jnp.where(left, zero, k2)], axis=0)
        z2 = lax.dot_general(q2, k_bd, (((1,), (1,)), ((), ())), preferred_element_type=F32) * scale
        per_head = []
        for hh in range(2):
            z = z2[:, hh * nk:(hh + 1) * nk]
            sp = jnp.maximum(z, 0.0) + jnp.log(1.0 + jnp.exp(-jnp.abs(z)))
            log_keep = -sp if causal is None else jnp.where(causal, -sp, 0.0)
            hi = log_keep.astype(BF16)
            lo = (log_keep - hi.astype(F32)).astype(BF16)
            stack += [hi, lo]
            per_head.append((z - sp, log_keep))
        pre.append(per_head)
    tq = tiles[0][0].shape[0]
    sums = jnp.dot(jnp.concatenate(stack, axis=0), tri, preferred_element_type=F32)
    results = []
    for ti, (q2, k2, v2, causal, r_in) in enumerate(tiles):
        ws, rs = [], []
        for hh in range(2):
            log_beta, log_keep = pre[ti][hh]
            base = (ti * 2 + hh) * 2 * tq
            tail = sums[base:base + tq] + sums[base + tq:base + 2 * tq]
            if r_in is not None:
                tail = tail + r_in[hh]
            w = jnp.exp(log_beta + tail)
            if causal is not None:
                w = jnp.where(causal, w, 0.0)
            ws.append(w.astype(BF16))
            rs.append(jnp.sum(log_keep, axis=1, keepdims=True))
        zero = jnp.zeros_like(v2)
        v_bd = jnp.concatenate([jnp.where(left, v2, zero), jnp.where(left, zero, v2)], axis=0)
        out = jnp.dot(jnp.concatenate(ws, axis=1), v_bd, preferred_element_type=F32)
        results.append((out, rs))
    return results


def _sb_attn_kernel(q_ref, k_ref, v_ref, o_ref, acc_ref, r_ref, tri_ref, *, seq, tq, scale):
    dh = SB_HEAD_DIM
    win = 2 * tq
    row = lax.broadcasted_iota(jnp.int32, (win, win), 0)
    col = lax.broadcasted_iota(jnp.int32, (win, win), 1)
    tri_ref[...] = jnp.where(row > col, 1.0, 0.0).astype(BF16)
    assert ATT_HEADS_PER_STEP == 2

    def q_tiles(p, carry):
        first, where = [], []
        for ti in range(ATT_TILES_PER_ITER):
            i = p * ATT_TILES_PER_ITER + ti
            q0 = pl.multiple_of(i * tq, tq)
            kblk = jnp.maximum(i - 1, 0)
            k0 = pl.multiple_of(kblk * tq, tq)
            q_pos = q0 + lax.broadcasted_iota(jnp.int32, (tq, win), 0)
            causal = (k0 + lax.broadcasted_iota(jnp.int32, (tq, win), 1)) < q_pos
            first.append((q_ref[pl.ds(q0, tq), :], k_ref[pl.ds(k0, win), :], v_ref[pl.ds(k0, win), :], causal, None))
            where.append((q0, kblk))
        rmax = []
        for ti, (out, rs) in enumerate(_sb_steps(first, tri_ref[...], scale)):
            acc_ref[ti] = out
            r_ref[2 * ti] = rs[0]
            r_ref[2 * ti + 1] = rs[1]
            rmax.append(jnp.maximum(jnp.max(rs[0]), jnp.max(rs[1])))
        for ti, (q0, kblk) in enumerate(where):
            def cond(c):
                j, rm = c
                return jnp.logical_and(j >= 0, rm > ATT_LOG_UNDERFLOW)

            def body(c, ti=ti, q0=q0):
                j, _ = c
                kj = pl.multiple_of(j * tq, tq)
                r_old = [r_ref[2 * ti], r_ref[2 * ti + 1]]
                tile = (q_ref[pl.ds(q0, tq), :], k_ref[pl.ds(kj, tq), :], v_ref[pl.ds(kj, tq), :], None, r_old)
                (out, rs), = _sb_steps([tile], tri_ref[:tq, :tq], scale)
                acc_ref[ti] += out
                r_new = [r_old[0] + rs[0], r_old[1] + rs[1]]
                r_ref[2 * ti] = r_new[0]
                r_ref[2 * ti + 1] = r_new[1]
                return j - 1, jnp.maximum(jnp.max(r_new[0]), jnp.max(r_new[1]))

            lax.while_loop(cond, body, (kblk - 1, rmax[ti]))
            o_ref[pl.ds(q0, tq), :] = acc_ref[ti].astype(o_ref.dtype)
        return carry

    lax.fori_loop(0, seq // (tq * ATT_TILES_PER_ITER), q_tiles, 0)


def _sb_attention(qkv, bsz, seq):
    tq = ATT_TQ
    assert seq % (tq * ATT_TILES_PER_ITER) == 0 and seq >= 2 * tq
    wcols = ATT_HEADS_PER_STEP * SB_HEAD_DIM
    nhp = SB_WIDTH // wcols
    kern = functools.partial(_sb_attn_kernel, seq=seq, tq=tq, scale=1.0 / math.sqrt(SB_HEAD_DIM))
    return pl.pallas_call(
        kern,
        grid=(bsz, nhp),
        in_specs=[pl.BlockSpec((None, seq, wcols), lambda b, h: (b, 0, h)),
                  pl.BlockSpec((None, seq, wcols), lambda b, h: (b, 0, nhp + h)),
                  pl.BlockSpec((None, seq, wcols), lambda b, h: (b, 0, 2 * nhp + h))],
        out_specs=pl.BlockSpec((None, seq, wcols), lambda b, h: (b, 0, h)),
        out_shape=jax.ShapeDtypeStruct((bsz, seq, SB_WIDTH), BF16),
        scratch_shapes=[pltpu.VMEM((ATT_TILES_PER_ITER, tq, wcols), F32),
                        pltpu.VMEM((ATT_HEADS_PER_STEP * ATT_TILES_PER_ITER, tq, 1), F32),
                        pltpu.VMEM((2 * tq, 2 * tq), BF16)],
        compiler_params=_params(("parallel", "parallel")),
        name="sb_attention",
    )(qkv, qkv, qkv)


def _pool_in_kernel(x_ref, w_ref, o_ref, halo_ref, *, seq, tm):
    i = pl.program_id(0)
    hmax = POOL_WINDOWS[-1]
    acc = jnp.dot(x_ref[...].astype(BF16), w_ref[...], preferred_element_type=F32)
    t0 = (i * tm) % seq

    @pl.when(t0 == 0)
    def _():
        halo_ref[...] = jnp.zeros_like(halo_ref)

    ext = jnp.concatenate([halo_ref[...], acc], axis=0)
    t_idx = t0 + lax.broadcasted_iota(jnp.int32, (tm, POOL_GROUP_WIDTH), 0)
    for gi, win in enumerate(POOL_WINDOWS):
        cs = slice(gi * POOL_GROUP_WIDTH, (gi + 1) * POOL_GROUP_WIDTH)
        e = ext[:, cs]
        span = 1
        while span < win:
            e = e + pltpu.roll(e, span, 0)
            span *= 2
        wsum = e[hmax:, :]
        cnt = jnp.minimum(t_idx + 1, win).astype(F32)
        o_ref[:, cs] = (wsum / cnt - acc[:, cs]).astype(o_ref.dtype)
    halo_ref[...] = acc[tm - hmax:, :]


def _pool_in(x, w, seq, tm):
    t, k = x.shape
    n = w.shape[1]
    return pl.pallas_call(
        functools.partial(_pool_in_kernel, seq=seq, tm=tm),
        grid=(t // tm,),
        in_specs=[pl.BlockSpec((tm, k), lambda i: (i, 0)),
                  pl.BlockSpec((k, n), lambda i: (0, 0))],
        out_specs=pl.BlockSpec((tm, n), lambda i: (i, 0)),
        out_shape=jax.ShapeDtypeStruct((t, n), BF16),
        scratch_shapes=[pltpu.VMEM((POOL_WINDOWS[-1], n), F32)],
        compiler_params=_params(("arbitrary",)),
        name="pool_in",
    )(x, w)


def _layer_norm(x, g, b):
    mu = jnp.mean(x, axis=-1, keepdims=True)
    xc = x - mu
    var = jnp.mean(xc * xc, axis=-1, keepdims=True)
    return xc * lax.rsqrt(var + LN_EPS) * g + b


def _route(h, rwh_ref, rb_ref, base_ref, tm):
    h_hi = h.astype(BF16)
    h_lo = (h - h_hi.astype(F32)).astype(BF16)
    both = jnp.dot(h_hi, rwh_ref[...], preferred_element_type=F32)
    logits = (both[:, :LANES] + both[:, LANES:]
              + jnp.dot(h_lo, rwh_ref[:, :LANES], preferred_element_type=F32)) + rb_ref[...]
    lt = logits.T[:N_EXPERTS, :]
    mx = jnp.max(lt, axis=0, keepdims=True)
    ex = jnp.exp(lt - mx)
    probs = ex / jnp.sum(ex, axis=0, keepdims=True)
    p = [probs[e:e + 1, :] for e in range(N_EXPERTS)]

    def first_max(vals):
        best, idx = vals[0], jnp.zeros_like(vals[0], dtype=jnp.int32)
        for n in range(1, len(vals)):
            upd = vals[n] > best
            idx = jnp.where(upd, n, idx)
            best = jnp.where(upd, vals[n], best)
        return best, idx

    scores = []
    for gi in range(N_EXPERT_GROUPS):
        p0, p1, p2, p3 = p[4 * gi:4 * gi + 4]
        a, b = jnp.maximum(p0, p1), jnp.minimum(p0, p1)
        c, d = jnp.maximum(p2, p3), jnp.minimum(p2, p3)
        scores.append(jnp.maximum(a, c) + jnp.maximum(jnp.minimum(a, c), jnp.maximum(b, d)))
    _, g_sel = first_max(scores)
    sel = []
    for n in range(EXPERTS_PER_GROUP):
        v = p[n]
        for gi in range(1, N_EXPERT_GROUPS):
            v = jnp.where(g_sel == gi, p[4 * gi + n], v)
        sel.append(v)
    v1, i1 = first_max(sel)
    v2, i2 = first_max([jnp.where(i1 == n, -1.0, sel[n]) for n in range(EXPERTS_PER_GROUP)])
    den = v1 + v2
    e1 = g_sel * EXPERTS_PER_GROUP + i1
    e2 = g_sel * EXPERTS_PER_GROUP + i2
    e_iota = lax.broadcasted_iota(jnp.int32, (N_EXPERTS, tm), 0)
    oh1 = e_iota == e1
    oh2 = e_iota == e2
    oh = jnp.where(jnp.logical_or(oh1, oh2), 1.0, 0.0)
    r_i = lax.broadcasted_iota(jnp.int32, (tm, tm), 0)
    c_i = lax.broadcasted_iota(jnp.int32, (tm, tm), 1)
    before = jnp.where(r_i < c_i, 1.0, 0.0).astype(BF16)
    cnt = jnp.dot(oh.astype(BF16), before, preferred_element_type=F32) + base_ref[...]
    rank1 = jnp.sum(jnp.where(oh1, cnt, 0.0), axis=0, keepdims=True)
    rank2 = jnp.sum(jnp.where(oh2, cnt, 0.0), axis=0, keepdims=True)
    base_ref[...] += jnp.sum(oh, axis=1, keepdims=True)
    e_out = jnp.concatenate([e1, e2], axis=0)
    w_out = jnp.concatenate([v1 / den, v2 / den], axis=0)
    r_out = jnp.concatenate([rank1, rank2], axis=0).astype(jnp.int32)
    return e_out, w_out, r_out


def _proj_ln_kernel(*refs, n_x, grouped, tm):
    xs = refs[:n_x]
    ws = refs[n_x:2 * n_x]
    pos = 2 * n_x
    if grouped:
        wg_ref, sc_ref = refs[pos:pos + 2]
        pos += 2
    res_ref, g_ref, b_ref, rwh_ref, rb_ref = refs[pos:pos + 5]
    h_ref, e_ref, gw_ref, rk_ref, cnt_ref, base_ref = refs[pos + 5:pos + 11]

    @pl.when(pl.program_id(0) == 0)
    def _():
        base_ref[...] = jnp.zeros_like(base_ref)

    sm = tm // PROJ_SLABS
    for sl in range(PROJ_SLABS):
        rs = slice(sl * sm, (sl + 1) * sm)
        if grouped:
            x = xs[0][rs, :]
            parts = []
            for gi in range(len(POOL_WINDOWS)):
                cs = slice(gi * POOL_GROUP_WIDTH, (gi + 1) * POOL_GROUP_WIDTH)
                parts.append(jnp.dot(x[:, cs], wg_ref[gi], preferred_element_type=F32))
            y = (jnp.concatenate(parts, axis=1) * sc_ref[...]).astype(BF16)
            mix = jnp.dot(y, ws[0][...], preferred_element_type=F32)
        else:
            mix = jnp.dot(xs[0][rs, :], ws[0][...], preferred_element_type=F32)
            for n in range(1, n_x):
                mix = mix + jnp.dot(xs[n][rs, :], ws[n][...], preferred_element_type=F32)
        h = _layer_norm(ALPHA * res_ref[rs, :] + mix, g_ref[...], b_ref[...])
        h_ref[rs, :] = h
        e_out, w_out, r_out = _route(h, rwh_ref, rb_ref, base_ref, sm)
        e_ref[:, rs] = e_out
        gw_ref[:, rs] = w_out
        rk_ref[:, rs] = r_out
    cnt_ref[...] = jnp.broadcast_to(base_ref[...], cnt_ref.shape)


def _proj_ln_route(xs, ws, res, ln_g, ln_b, router, tm, grouped=None):
    t, d = res.shape
    rwh, rb = router
    n_x = len(xs)
    full = lambda a: pl.BlockSpec(a.shape, lambda i, nd=a.ndim: (0,) * nd)
    in_specs = [pl.BlockSpec((tm, x.shape[1]), lambda i: (i, 0)) for x in xs]
    in_specs += [full(w) for w in ws]
    args = list(xs) + list(ws)
    if grouped is not None:
        in_specs += [full(a) for a in grouped]
        args += list(grouped)
    in_specs += [pl.BlockSpec((tm, d), lambda i: (i, 0)), full(ln_g), full(ln_b), full(rwh), full(rb)]
    args += [res, ln_g, ln_b, rwh, rb]
    pair = lambda dt: jax.ShapeDtypeStruct((2, t), dt)
    out = pl.pallas_call(
        functools.partial(_proj_ln_kernel, n_x=n_x, grouped=grouped is not None, tm=tm),
        grid=(t // tm,),
        in_specs=in_specs,
        out_specs=[pl.BlockSpec((tm, d), lambda i: (i, 0)),
                   pl.BlockSpec((2, tm), lambda i: (0, i)),
                   pl.BlockSpec((2, tm), lambda i: (0, i)),
                   pl.BlockSpec((2, tm), lambda i: (0, i)),
                   pl.BlockSpec((N_EXPERTS, LANES), lambda i: (0, 0))],
        out_shape=[jax.ShapeDtypeStruct((t, d), F32), pair(jnp.int32), pair(F32), pair(jnp.int32),
                   jax.ShapeDtypeStruct((N_EXPERTS, LANES), F32)],
        scratch_shapes=[pltpu.VMEM((N_EXPERTS, 1), F32)],
        compiler_params=_params(("arbitrary",)),
        name="proj_ln_route",
    )(*args)
    h, e, gw, rk, cnt = out
    return h, e, gw, rk, cnt[:, 0]


def _dispatch_kernel(p1_ref, p2_ref, fill_ref, h_ref, xs_ref, zero_ref, sem, *, tm, n_fill):
    base = pl.program_id(0) * tm

    @pl.when(pl.program_id(0) == 0)
    def _():
        zero_ref[...] = jnp.zeros_like(zero_ref)
        n_zero = zero_ref.shape[0]
        fill = lambda k: pltpu.make_async_copy(
            zero_ref, xs_ref.at[pl.ds(pl.multiple_of(jnp.maximum(fill_ref[k], 0), n_zero), n_zero)], sem.at[2])
        for k in range(n_fill):
            pl.when(fill_ref[k] >= 0)(lambda k=k: fill(k).start())
        for k in range(n_fill):
            pl.when(fill_ref[k] >= 0)(lambda k=k: fill(k).wait())

    def issue(r, carry):
        src = h_ref.at[pl.ds(r, 1)]
        pltpu.make_async_copy(src, xs_ref.at[pl.ds(p1_ref[base + r], 1)], sem.at[0]).start()
        pltpu.make_async_copy(src, xs_ref.at[pl.ds(p2_ref[base + r], 1)], sem.at[1]).start()
        return carry

    lax.fori_loop(0, tm, issue, 0, unroll=DMA_ISSUE_UNROLL)
    pltpu.make_async_copy(h_ref, xs_ref.at[pl.ds(0, tm)], sem.at[0]).wait()
    pltpu.make_async_copy(h_ref, xs_ref.at[pl.ds(0, tm)], sem.at[1]).wait()


def _dispatch(h, pos1, pos2, fill_rows, n_rows, tm, tm_fill):
    t, d = h.shape
    return pl.pallas_call(
        functools.partial(_dispatch_kernel, tm=tm, n_fill=fill_rows.shape[0]),
        grid_spec=pltpu.PrefetchScalarGridSpec(
            num_scalar_prefetch=3,
            grid=(t // tm,),
            in_specs=[pl.BlockSpec((tm, d), lambda i, p1, p2, fr: (i, 0))],
            out_specs=pl.BlockSpec(memory_space=pl.ANY),
            scratch_shapes=[pltpu.VMEM((tm_fill, d), h.dtype), pltpu.SemaphoreType.DMA((3,))],
        ),
        out_shape=jax.ShapeDtypeStruct((n_rows, d), h.dtype),
        compiler_params=_params(("arbitrary",)),
        name="moe_dispatch",
    )(pos1, pos2, fill_rows, h)


def _moe_kernel(te_ref, slot_ref, nxt_ref, lo_ref, hi_ref, na_ref, x_ref, wg_hbm, wu_hbm, wd_hbm, o_ref,
                wg_buf, wu_buf, wd_buf, st_g, st_u, st_d, sem, *, layer):
    i = pl.program_id(0)
    rg = wg_buf.shape[1] // MOE_W_GROUPS
    rd = wd_buf.shape[1] // MOE_W_GROUPS

    def copies(e, c, s):
        r0g = pl.multiple_of(c * rg, rg)
        r0d = pl.multiple_of(c * rd, rd)
        return (pltpu.make_async_copy(wg_hbm.at[layer, e, pl.ds(r0g, rg)], st_g.at[s], sem.at[s, 0]),
                pltpu.make_async_copy(wu_hbm.at[layer, e, pl.ds(r0g, rg)], st_u.at[s], sem.at[s, 1]),
                pltpu.make_async_copy(wd_hbm.at[layer, e, pl.ds(r0d, rd)], st_d.at[s], sem.at[s, 2]))

    def start(e, c):
        for cp in copies(e, c, c % 2):
            cp.start()

    def finish(e, c, w_slot):
        s = c % 2
        for cp in copies(e, c, s):
            cp.wait()
        wg_buf[w_slot, pl.ds(pl.multiple_of(c * rg, rg), rg), :] = st_g[s].astype(BF16)
        wu_buf[w_slot, pl.ds(pl.multiple_of(c * rg, rg), rg), :] = st_u[s].astype(BF16)
        wd_buf[w_slot, pl.ds(pl.multiple_of(c * rd, rd), rd), :] = st_d[s].astype(BF16)

    def fetch(e, lo, hi, w_slot):
        def step(c, carry):
            @pl.when(c + 1 < hi)
            def _():
                start(e, c + 1)
            finish(e, c, w_slot)
            return carry
        lax.fori_loop(lo, hi, step, 0)

    @pl.when(i == 0)
    def _():
        start(te_ref[0], 0)
        fetch(te_ref[0], 0, MOE_W_GROUPS, slot_ref[0])

    @pl.when(i < na_ref[0])
    def _():
        lo, hi, nxt, w_slot = lo_ref[i], hi_ref[i], nxt_ref[i], slot_ref[i]

        def issue_first(_, carry):
            start(nxt, lo)
            return carry
        lax.fori_loop(0, (hi > lo).astype(jnp.int32), issue_first, 0)

        x = x_ref[...].astype(BF16)
        gate = jnp.dot(x, wg_buf[w_slot], preferred_element_type=F32)
        up = jnp.dot(x, wu_buf[w_slot], preferred_element_type=F32)
        hid = (jax.nn.silu(gate) * up).astype(BF16)
        o_ref[...] = jnp.dot(hid, wd_buf[w_slot], preferred_element_type=F32)
        fetch(nxt, lo, hi, 1 - w_slot)

    @pl.when(i >= na_ref[0])
    def _():
        o_ref[...] = jnp.zeros_like(o_ref)


def _moe_experts(xs, tables, n_active, w_gate, w_up, w_down, layer, tm):
    p, d = xs.shape
    f = w_gate.shape[3]
    row_blk = lambda i, te, sl, nx, lo, hi, na: (jnp.minimum(i, na[0] - 1), 0)
    hbm = pl.BlockSpec(memory_space=pl.ANY)
    return pl.pallas_call(
        functools.partial(_moe_kernel, layer=layer),
        grid_spec=pltpu.PrefetchScalarGridSpec(
            num_scalar_prefetch=6,
            grid=(p // tm,),
            in_specs=[pl.BlockSpec((tm, d), row_blk), hbm, hbm, hbm],
            out_specs=pl.BlockSpec((tm, d), lambda i, te, sl, nx, lo, hi, na: (i, 0)),
            scratch_shapes=[pltpu.VMEM((2, d, f), BF16), pltpu.VMEM((2, d, f), BF16), pltpu.VMEM((2, f, d), BF16),
                            pltpu.VMEM((2, d // MOE_W_GROUPS, f), F32), pltpu.VMEM((2, d // MOE_W_GROUPS, f), F32),
                            pltpu.VMEM((2, f // MOE_W_GROUPS, d), F32), pltpu.SemaphoreType.DMA((2, 3))],
        ),
        out_shape=jax.ShapeDtypeStruct((p, d), F32),
        compiler_params=_params(("arbitrary",)),
        name="moe_experts",
    )(*tables, n_active, xs, w_gate, w_up, w_down)


def _combine_ln_kernel(p1_ref, p2_ref, h_ref, g1_ref, g2_ref, lg_ref, lb_ref, ys_ref, o_ref,
                       buf1a, buf1b, buf2a, buf2b, sem, *, tm, n_tiles):
    i = pl.program_id(0)
    rows = COMBINE_ROWS_PER_TRIP
    bufs = ((buf1a, buf2a), (buf1b, buf2b))

    def gather(r0, slot):
        base = i * tm
        b1, b2 = bufs[slot]
        for r in range(rows):
            p1 = p1_ref[base + r0 + r]
            p2 = p2_ref[base + r0 + r]
            pltpu.make_async_copy(ys_ref.at[pl.ds(p1, 1)], b1.at[pl.ds(r0 + r, 1)], sem.at[slot, 0]).start()
            pltpu.make_async_copy(ys_ref.at[pl.ds(p2, 1)], b2.at[pl.ds(r0 + r, 1)], sem.at[slot, 1]).start()

    def finish(r0, slot):
        b1, b2 = bufs[slot]
        rs = pl.ds(r0, rows)
        ffn = g1_ref[rs, :] * b1[rs, :] + g2_ref[rs, :] * b2[rs, :]
        o_ref[rs, :] = _layer_norm(ALPHA * h_ref[rs, :] + ffn, lg_ref[...], lb_ref[...])

    def step(gather_slot, finish_slot):
        if finish_slot is not None:
            b1, b2 = bufs[finish_slot]
            pltpu.make_async_copy(ys_ref.at[pl.ds(0, tm)], b1, sem.at[finish_slot, 0]).wait()
            pltpu.make_async_copy(ys_ref.at[pl.ds(0, tm)], b2, sem.at[finish_slot, 1]).wait()

        def trip(k, carry):
            r0 = pl.multiple_of(k * rows, rows)
            if gather_slot is not None:
                gather(r0, gather_slot)
            if finish_slot is not None:
                finish(r0, finish_slot)
            return carry
        lax.fori_loop(0, tm // rows, trip, 0)

    middle = jnp.logical_and(i >= 1, i < n_tiles)
    pl.when(i == 0)(lambda: step(0, None))
    pl.when(jnp.logical_and(middle, i % 2 == 1))(lambda: step(1, 0))
    pl.when(jnp.logical_and(middle, i % 2 == 0))(lambda: step(0, 1))
    pl.when(i == n_tiles)(lambda: step(None, (n_tiles - 1) % 2))


def _combine_ln(h, ys, pos1, pos2, g1, g2, ln_g, ln_b, tm):
    t, d = h.shape
    n_tiles = t // tm
    row = lambda i, p1, p2: (jnp.maximum(i - 1, 0), 0)
    const = lambda i, p1, p2: (0, 0)
    return pl.pallas_call(
        functools.partial(_combine_ln_kernel, tm=tm, n_tiles=n_tiles),
        grid_spec=pltpu.PrefetchScalarGridSpec(
            num_scalar_prefetch=2,
            grid=(n_tiles + 1,),
            in_specs=[pl.BlockSpec((tm, d), row),
                      pl.BlockSpec((tm, 1), row),
                      pl.BlockSpec((tm, 1), row),
                      pl.BlockSpec((1, d), const),
                      pl.BlockSpec((1, d), const),
                      pl.BlockSpec(memory_space=pl.ANY)],
            out_specs=pl.BlockSpec((tm, d), row),
            scratch_shapes=[pltpu.VMEM((tm, d), F32)] * 4 + [pltpu.SemaphoreType.DMA((2, 2))],
        ),
        out_shape=jax.ShapeDtypeStruct((t, d), F32),
        compiler_params=_params(("arbitrary",)),
        name="moe_combine_ln",
    )(pos1, pos2, h, g1, g2, ln_g, ln_b, ys)


def _moe_block(h, e, gw, rk, counts, w_gate, w_up, w_down, layer, ln_g, ln_b, tm_tok):
    t, d = h.shape
    tm = MOE_TM
    n_rows = 2 * t + N_EXPERTS * tm
    n_tiles = n_rows // tm
    cnt = counts.astype(jnp.int32)
    padded = ((cnt + tm - 1) // tm) * tm
    ends = jnp.cumsum(padded)
    offs = ends - padded
    e_ids = jnp.arange(N_EXPERTS, dtype=jnp.int32)
    pos = jnp.sum(jnp.where(e[..., None] == e_ids, offs, 0), axis=-1) + rk
    n_active = jnp.maximum(ends[-1] // tm, 1)
    tile_start = jnp.minimum(jnp.arange(n_tiles, dtype=jnp.int32), n_active - 1) * tm
    tile_expert = jnp.minimum(jnp.sum((ends[None, :] <= tile_start[:, None]).astype(jnp.int32), axis=1),
                              N_EXPERTS - 1)
    last_tile = jnp.where(padded > 0, ends - tm, -1)
    tail = ends[-1] + jnp.arange(N_EXPERTS, dtype=jnp.int32) * tm
    fill_rows = jnp.concatenate([last_tile, jnp.where(tail < n_rows, tail, -1)]).astype(jnp.int32)
    xs = _dispatch(h, pos[0], pos[1], fill_rows, n_rows, tm_tok, tm)
    idx = jnp.arange(n_tiles, dtype=jnp.int32)
    is_first = jnp.concatenate([jnp.ones((1,), bool), tile_expert[1:] != tile_expert[:-1]])
    run_id = jnp.cumsum(is_first.astype(jnp.int32)) - 1
    run_start = lax.cummax(jnp.where(is_first, idx, 0))
    same_run = run_id[:, None] == run_id[None, :]
    run_len = jnp.sum((same_run & (idx[None, :] < n_active)).astype(jnp.int32), axis=1)
    nxt_tile = run_start + run_len
    nxt = jnp.sum(jnp.where(idx[None, :] == nxt_tile[:, None], tile_expert[None, :], 0), axis=1)
    has_next = nxt_tile < n_active
    k_in_run = idx - run_start
    lo = jnp.where(has_next, (MOE_W_GROUPS * k_in_run) // jnp.maximum(run_len, 1), 0)
    hi = jnp.where(has_next, (MOE_W_GROUPS * (k_in_run + 1)) // jnp.maximum(run_len, 1), 0)
    tables = [a.astype(jnp.int32) for a in (tile_expert, run_id % 2, jnp.where(has_next, nxt, 0), lo, hi)]
    ys = _moe_experts(xs, tables, n_active.reshape(1).astype(jnp.int32), w_gate, w_up, w_down, layer, tm)
    return _combine_ln(h, ys, pos[0], pos[1], gw[0].reshape(t, 1), gw[1].reshape(t, 1), ln_g, ln_b, tm_tok)


def kernel(x, ab_w_in, ab_lambda_re, ab_lambda_im, ab_log_dt, ab_b_re, ab_b_im, ab_c_re, ab_c_im, ab_d,
           ab_w_glu, ab_b_glu, ab_w_out, c_w_in, c_w_group, c_scale, c_w_out, ln_g, ln_b, router_w,
           router_b, moe_w_gate, moe_w_up, moe_w_down):
    bsz, seq, d = x.shape
    t = bsz * seq
    tm = min(ROW_TILE, t)
    assert t % tm == 0 and seq % tm == 0 and seq % S5_CHUNK == 0
    row = lambda a: a.reshape(1, -1)

    rw = jnp.zeros((d, LANES), F32).at[:, :N_EXPERTS].set(router_w)
    rw_hi = rw.astype(BF16)
    rw_lo = (rw - rw_hi.astype(F32)).astype(BF16)
    rb = jnp.zeros((1, LANES), F32).at[0, :N_EXPERTS].set(router_b)
    router = (jnp.concatenate([rw_hi, rw_lo], axis=1), rb)

    h = x.reshape(t, d)

    wg_b, wu_b, wd_b = moe_w_gate, moe_w_up, moe_w_down
    u, u_chunked, qkv = _in_proj_ab(h, ab_w_in[0].astype(BF16), min(2 * ROW_TILE, t))
    chunks = seq // S5_CHUNK
    nsteps = max(int(math.log2(chunks)), 0)
    assert (1 << nsteps) == chunks
    ops = _s5_operands(ab_lambda_re[0], ab_lambda_im[0], ab_log_dt[0], ab_b_re[0], ab_b_im[0],
                       ab_c_re[0], ab_c_im[0], S5_CHUNK, nsteps)
    y_ssm = _s5_scan(u_chunked, ops, bsz, seq)
    y_a = _s5_glu(y_ssm, u, ab_d[0], ab_w_glu[0], ab_b_glu[0], tm)
    y_b = _sb_attention(qkv.reshape(bsz, seq, 3 * SB_WIDTH), bsz, seq).reshape(t, SB_WIDTH)
    w_out = ab_w_out[0].astype(BF16)
    h, e, gw, rk, cnt = _proj_ln_route([y_a, y_b], [w_out[:S5_WIDTH], w_out[S5_WIDTH:]], h,
                                       row(ln_g[0, 0]), row(ln_b[0, 0]), router, tm)
    h = _moe_block(h, e, gw, rk, cnt, wg_b, wu_b, wd_b, 0, row(ln_g[0, 1]), row(ln_b[0, 1]), tm)

    pooled = _pool_in(h, c_w_in[0].astype(BF16), seq, tm)
    h, e, gw, rk, cnt = _proj_ln_route([pooled], [c_w_out[0].astype(BF16)], h,
                                       row(ln_g[1, 0]), row(ln_b[1, 0]), router, tm,
                                       grouped=(c_w_group[0].astype(BF16), row(c_scale[0])))
    h = _moe_block(h, e, gw, rk, cnt, wg_b, wu_b, wd_b, 1, row(ln_g[1, 1]), row(ln_b[1, 1]), tm)
    return h.reshape(bsz, seq, d)
```

```python
import functools
import math

import jax
import jax.numpy as jnp
from jax import lax
from jax.experimental import pallas as pl
from jax.experimental.pallas import tpu as pltpu

S5_WIDTH = 512
S5_GROUP = 16
S5_GROUPS = 32
S5_STATE = 64
SB_HEAD_DIM = 128
SB_HEADS = 12
SB_WIDTH = SB_HEADS * SB_HEAD_DIM
POOL_WINDOWS = (2, 4, 8, 16)
POOL_GROUP_WIDTH = 512
N_EXPERTS = 16
N_EXPERT_GROUPS = 4
EXPERTS_PER_GROUP = 4
DEPTH = 2
ALPHA = (2.0 * DEPTH) ** 0.25
LN_EPS = 1e-5

LANES = 128
VMEM_LIMIT = 56 * 1024 * 1024

S5_CHUNK = 32
ATT_TQ = 128
ATT_HEADS_PER_STEP = 2
ATT_TILES_PER_ITER = 8
ATT_LOG_UNDERFLOW = -88.0
ATT_MASKED_SCORE = -1e30
MOE_TM = 256
MOE_W_GROUPS = 8
ROW_TILE = 512
PROJ_SLABS = 2
DMA_ISSUE_UNROLL = 8
COMBINE_ROWS_PER_TRIP = 64

BF16 = jnp.bfloat16
F32 = jnp.float32


def _params(sem, vmem=VMEM_LIMIT):
    return pltpu.CompilerParams(dimension_semantics=sem, vmem_limit_bytes=vmem)


def _piece_transpose(blocks):
    v = list(blocks)
    n = len(v)
    assert n * S5_GROUP == LANES
    piece = lax.broadcasted_iota(jnp.int32, v[0].shape, 1) // S5_GROUP
    d = n // 2
    while d >= 1:
        upper = (piece & d) != 0
        for q in range(n):
            if q & d == 0:
                a, b = v[q], v[q + d]
                v[q] = jnp.where(upper, pltpu.roll(b, d * S5_GROUP, 1), a)
                v[q + d] = jnp.where(upper, b, pltpu.roll(a, LANES - d * S5_GROUP, 1))
        d //= 2
    return v


def _in_proj_ab_kernel(x_ref, w_ref, u_ref, up_ref, qkv_ref, xb_ref, us_ref):
    j = pl.program_id(1)

    @pl.when(j == 0)
    def _():
        xb_ref[...] = x_ref[...].astype(BF16)

    acc = jnp.dot(xb_ref[...], w_ref[...], preferred_element_type=F32)

    @pl.when(j == 0)
    def _():
        u_ref[...] = acc
        chunk = S5_CHUNK
        n_chunks = acc.shape[0] // chunk
        per = LANES // S5_GROUP
        for g1 in range(S5_WIDTH // LANES):
            us_ref[g1] = acc[:, g1 * LANES:(g1 + 1) * LANES]
        for s1 in range(chunk // per):
            for g1 in range(S5_WIDTH // LANES):
                rows = [us_ref[g1, pl.ds(s1 * per + s2, n_chunks, stride=chunk), :] for s2 in range(per)]
                regrouped = _piece_transpose(rows)
                for g2 in range(per):
                    up_ref[g1 * per + g2, :, s1 * LANES:(s1 + 1) * LANES] = regrouped[g2].astype(up_ref.dtype)

    @pl.when(j > 0)
    def _():
        qkv_ref[...] = acc.astype(qkv_ref.dtype)


def _in_proj_ab(x, w, tm):
    m, k = x.shape
    tn = S5_WIDTH
    n_blocks = w.shape[1] // tn
    return pl.pallas_call(
        _in_proj_ab_kernel,
        grid=(m // tm, n_blocks),
        in_specs=[pl.BlockSpec((tm, k), lambda i, j: (i, 0)),
                  pl.BlockSpec((k, tn), lambda i, j: (0, j))],
        out_specs=[pl.BlockSpec((tm, tn), lambda i, j: (i, 0)),
                   pl.BlockSpec((S5_GROUPS, tm // S5_CHUNK, tn), lambda i, j: (0, i, 0)),
                   pl.BlockSpec((tm, tn), lambda i, j: (i, jnp.maximum(j - 1, 0)))],
        out_shape=[jax.ShapeDtypeStruct((m, tn), F32),
                   jax.ShapeDtypeStruct((S5_GROUPS, m // S5_CHUNK, tn), BF16),
                   jax.ShapeDtypeStruct((m, w.shape[1] - tn), BF16)],
        scratch_shapes=[pltpu.VMEM((tm, k), BF16), pltpu.VMEM((tn // LANES, tm, LANES), F32)],
        compiler_params=_params(("parallel", "arbitrary")),
        name="in_proj_ab",
    )(x, w)


def _bmm_f32_kernel(x_ref, y_ref, o_ref):
    o_ref[...] = jnp.dot(x_ref[...], y_ref[...], preferred_element_type=F32,
                         precision=lax.Precision.HIGHEST)


def _bmm_f32(x, y):
    g, m, k = x.shape
    n = y.shape[2]
    return pl.pallas_call(
        _bmm_f32_kernel,
        grid=(g,),
        in_specs=[pl.BlockSpec((None, m, k), lambda i: (i, 0, 0)),
                  pl.BlockSpec((None, k, n), lambda i: (i, 0, 0))],
        out_specs=pl.BlockSpec((None, m, n), lambda i: (i, 0, 0)),
        out_shape=jax.ShapeDtypeStruct((g, m, n), F32),
        compiler_params=_params(("parallel",)),
        name="s5_taps",
    )(x, y)


def _s5_kernel(u_ref, r0_ref, ws_ref, wo_ref, a1_ref, a2_ref, o_ref, wt_ref, *, chunks, nsteps):
    h = S5_GROUP
    r0 = r0_ref[...]
    lane = lax.broadcasted_iota(jnp.int32, r0.shape, 1)
    for j in range(r0.shape[1] // h):
        blk = r0 if j == 0 else jnp.where(lane >= j * h, pltpu.roll(r0, j * h, 1), 0.0)
        wt_ref[j * h:(j + 1) * h, :] = blk.astype(BF16)
    u = u_ref[...]
    y_intra = jnp.dot(u, wt_ref[...], preferred_element_type=F32)
    s = jnp.dot(u, ws_ref[...], preferred_element_type=F32)
    m = s.shape[0]
    c_idx = lax.broadcasted_iota(jnp.int32, s.shape, 0) % chunks
    half = s.shape[1] // 2
    for k in range(nsteps):
        d = 1 << k
        sh = jnp.where(c_idx >= d, pltpu.roll(s, d, 0), 0.0)
        s = s + a1_ref[k:k + 1, :] * sh + a2_ref[k:k + 1, :] * pltpu.roll(sh, half, 1)
    s_prev = jnp.where(c_idx >= 1, pltpu.roll(s, 1, 0), 0.0)
    y_inter = jnp.dot(s_prev.astype(BF16), wo_ref[...], preferred_element_type=F32)
    o_ref[...] = y_intra + y_inter


def _s5_operands(lam_re, lam_im, log_dt, b_re, b_im, c_re, c_im, chunk, nsteps):
    g, n = lam_re.shape
    h = S5_GROUP
    dt = jnp.exp(log_dt)[:, None]
    lr, li = lam_re, lam_im
    mag = jnp.exp(lr * dt)
    lbr = mag * jnp.cos(li * dt)
    lbi = mag * jnp.sin(li * dt)
    den = lr * lr + li * li
    coef_re = ((lbr - 1.0) * lr + lbi * li) / den
    coef_im = (lbi * lr - (lbr - 1.0) * li) / den
    bbar_re = coef_re[..., None] * b_re - coef_im[..., None] * b_im
    bbar_im = coef_re[..., None] * b_im + coef_im[..., None] * b_re

    def power(mult):
        mult = mult.astype(F32)[None, :, None]
        pm = jnp.exp(lr[:, None, :] * dt[:, None, :] * mult)
        ang = li[:, None, :] * dt[:, None, :] * mult
        return pm * jnp.cos(ang), pm * jnp.sin(ang)

    pr, pi = power(jnp.arange(chunk + 1))
    cre = jnp.swapaxes(c_re, 1, 2)
    cim = jnp.swapaxes(c_im, 1, 2)
    cb_re = cre[:, :, :, None] * bbar_re[:, :, None, :] - cim[:, :, :, None] * bbar_im[:, :, None, :]
    cb_im = cre[:, :, :, None] * bbar_im[:, :, None, :] + cim[:, :, :, None] * bbar_re[:, :, None, :]
    y_op = jnp.concatenate([cb_re, cb_im], axis=1).reshape(g, 2 * n, h * h)
    x_op = jnp.concatenate([pr[:, :chunk], -pi[:, :chunk]], axis=-1)
    taps = _bmm_f32(x_op, y_op).reshape(g, chunk, h, h)
    r0 = taps.transpose(0, 3, 1, 2).reshape(g, h, chunk * h)
    prr = pr[:, chunk - 1::-1][:, :chunk]
    pir = pi[:, chunk - 1::-1][:, :chunk]
    ws_re = prr[:, :, None, :] * jnp.swapaxes(bbar_re, 1, 2)[:, None] - pir[:, :, None, :] * jnp.swapaxes(bbar_im, 1, 2)[:, None]
    ws_im = prr[:, :, None, :] * jnp.swapaxes(bbar_im, 1, 2)[:, None] + pir[:, :, None, :] * jnp.swapaxes(bbar_re, 1, 2)[:, None]
    ws = jnp.concatenate([ws_re, ws_im], axis=-1).reshape(g, chunk * h, 2 * n)
    p1r = pr[:, 1:chunk + 1]
    p1i = pi[:, 1:chunk + 1]
    wo_re = c_re[:, None] * p1r[:, :, None, :] - c_im[:, None] * p1i[:, :, None, :]
    wo_im = -(c_re[:, None] * p1i[:, :, None, :] + c_im[:, None] * p1r[:, :, None, :])
    wo = jnp.concatenate([wo_re, wo_im], axis=-1)
    wo = wo.transpose(0, 3, 1, 2).reshape(g, 2 * n, chunk * h)
    sr, si = power(chunk * (2 ** jnp.arange(max(nsteps, 1))))
    a1 = jnp.concatenate([sr, sr], axis=-1)
    a2 = jnp.concatenate([-si, si], axis=-1)
    return r0, ws.astype(BF16), wo.astype(BF16), a1, a2


def _s5_scan(up, ops, bsz, seq):
    r0, ws, wo, a1, a2 = ops
    g, h, chunk = S5_GROUPS, S5_GROUP, S5_CHUNK
    chunks = seq // chunk
    m = bsz * chunks
    nsteps = a1.shape[1] if chunks > 1 else 0
    n2 = ws.shape[2]
    ns = a1.shape[1]
    return pl.pallas_call(
        functools.partial(_s5_kernel, chunks=chunks, nsteps=nsteps),
        grid=(g,),
        in_specs=[pl.BlockSpec((None, m, chunk * h), lambda i: (i, 0, 0)),
                  pl.BlockSpec((None, h, chunk * h), lambda i: (i, 0, 0)),
                  pl.BlockSpec((None, chunk * h, n2), lambda i: (i, 0, 0)),
                  pl.BlockSpec((None, n2, chunk * h), lambda i: (i, 0, 0)),
                  pl.BlockSpec((None, ns, n2), lambda i: (i, 0, 0)),
                  pl.BlockSpec((None, ns, n2), lambda i: (i, 0, 0))],
        out_specs=pl.BlockSpec((None, m, chunk * h), lambda i: (i, 0, 0)),
        out_shape=jax.ShapeDtypeStruct((g, m, chunk * h), F32),
        scratch_shapes=[pltpu.VMEM((chunk * h, chunk * h), BF16)],
        compiler_params=_params(("parallel",)),
        name="s5_scan",
    )(up, r0, ws, wo, a1, a2)


def _s5_glu_kernel(yp_ref, u_ref, d_ref, w_ref, b_ref, o_ref, y_ref):
    chunk = S5_CHUNK
    n_chunks = yp_ref.shape[1]
    per = LANES // S5_GROUP
    for s1 in range(chunk // per):
        for g1 in range(S5_WIDTH // LANES):
            by_step = _piece_transpose([yp_ref[g1 * per + g2, :, s1 * LANES:(s1 + 1) * LANES] for g2 in range(per)])
            for s2 in range(per):
                y_ref[g1, pl.ds(s1 * per + s2, n_chunks, stride=chunk), :] = by_step[s2]
    y_ssm = jnp.concatenate([y_ref[g1] for g1 in range(S5_WIDTH // LANES)], axis=1)
    y = y_ssm + d_ref[...] * u_ref[...]
    gl = jax.nn.gelu(y)
    gate = jnp.dot(gl.astype(BF16), w_ref[...], preferred_element_type=F32) + b_ref[...]
    o_ref[...] = (gl * jax.nn.sigmoid(gate)).astype(o_ref.dtype)


def _s5_glu(yp, u, d, w_glu, b_glu, tm):
    t, c = u.shape
    return pl.pallas_call(
        _s5_glu_kernel,
        grid=(t // tm,),
        in_specs=[pl.BlockSpec((S5_GROUPS, tm // S5_CHUNK, c), lambda i: (0, i, 0)),
                  pl.BlockSpec((tm, c), lambda i: (i, 0)),
                  pl.BlockSpec((1, c), lambda i: (0, 0)),
                  pl.BlockSpec((c, c), lambda i: (0, 0)),
                  pl.BlockSpec((1, c), lambda i: (0, 0))],
        out_specs=pl.BlockSpec((tm, c), lambda i: (i, 0)),
        out_shape=jax.ShapeDtypeStruct((t, c), BF16),
        scratch_shapes=[pltpu.VMEM((c // LANES, tm, LANES), F32)],
        compiler_params=_params(("parallel",)),
        name="s5_glu",
    )(yp, u, d.reshape(1, c), w_glu.astype(BF16), b_glu.reshape(1, c))


def _sb_steps(tiles, tri, scale):
    dh = SB_HEAD_DIM
    nk = tiles[0][1].shape[0]
    left = lax.broadcasted_iota(jnp.int32, (nk, 2 * dh), 1) < dh
    pre = []
    stack = []
    for q2, k2, v2, causal, _ in tiles:
        zero = jnp.zeros_like(k2)
        k_bd = jnp.concatenate([jnp.where(left, k2, zero), jnp.where(left, zero, k2)], axis=0)
        z2 = lax.dot_general(q2, k_bd, (((1,), (1,)), ((), ())), preferred_element_type=F32) * scale
        per_head = []
        for hh in range(2):
            z = z2[:, hh * nk:(hh + 1) * nk]
            if causal is not None:
                z = jnp.where(causal, z, ATT_MASKED_SCORE)
            sp = jnp.maximum(z, 0.0) + jnp.log(1.0 + jnp.exp(-jnp.abs(z)))
            log_keep = -sp
            hi = log_keep.astype(BF16)
            lo = (log_keep - hi.astype(F32)).astype(BF16)
            stack += [hi, lo]
            per_head.append((z - sp, log_keep))
        pre.append(per_head)
    tq = tiles[0][0].shape[0]
    sums = jnp.dot(jnp.concatenate(stack, axis=0), tri, preferred_element_type=F32)
    results = []
    for ti, (q2, k2, v2, causal, r_in) in enumerate(tiles):
        ws, rs = [], []
        for hh in range(2):
            log_beta, log_keep = pre[ti][hh]
            base = (ti * 2 + hh) * 2 * tq
            tail = sums[base:base + tq] + sums[base + tq:base + 2 * tq]
            if r_in is not None:
                tail = tail + r_in[hh]
            ws.append(jnp.exp(log_beta + tail).astype(BF16))
            rs.append(jnp.sum(log_keep, axis=1, keepdims=True))
        zero = jnp.zeros_like(v2)
        v_bd = jnp.concatenate([jnp.where(left, v2, zero), jnp.where(left, zero, v2)], axis=0)
        out = jnp.dot(jnp.concatenate(ws, axis=1), v_bd, preferred_element_type=F32)
        results.append((out, rs))
    return results


def _sb_attn_kernel(q_ref, k_ref, v_ref, o_ref, acc_ref, r_ref, tri_ref, *, seq, tq, scale):
    dh = SB_HEAD_DIM
    win = 2 * tq
    row = lax.broadcasted_iota(jnp.int32, (win, win), 0)
    col = lax.broadcasted_iota(jnp.int32, (win, win), 1)
    tri_ref[...] = jnp.where(row > col, 1.0, 0.0).astype(BF16)
    assert ATT_HEADS_PER_STEP == 2

    def q_tiles(p, carry):
        first, where = [], []
        for ti in range(ATT_TILES_PER_ITER):
            i = p * ATT_TILES_PER_ITER + ti
            q0 = pl.multiple_of(i * tq, tq)
            kblk = jnp.maximum(i - 1, 0)
            k0 = pl.multiple_of(kblk * tq, tq)
            q_pos = q0 + lax.broadcasted_iota(jnp.int32, (tq, win), 0)
            causal = (k0 + lax.broadcasted_iota(jnp.int32, (tq, win), 1)) < q_pos
            first.append((q_ref[pl.ds(q0, tq), :], k_ref[pl.ds(k0, win), :], v_ref[pl.ds(k0, win), :], causal, None))
            where.append((q0, kblk))
        rmax = []
        for ti, (out, rs) in enumerate(_sb_steps(first, tri_ref[...], scale)):
            acc_ref[ti] = out
            r_ref[2 * ti] = rs[0]
            r_ref[2 * ti + 1] = rs[1]
            rmax.append(jnp.maximum(jnp.max(rs[0]), jnp.max(rs[1])))
        for ti, (q0, kblk) in enumerate(where):
            def cond(c):
                j, rm = c
                return jnp.logical_and(j >= 0, rm > ATT_LOG_UNDERFLOW)

            def body(c, ti=ti, q0=q0):
                j, _ = c
                kj = pl.multiple_of(j * tq, tq)
                r_old = [r_ref[2 * ti], r_ref[2 * ti + 1]]
                tile = (q_ref[pl.ds(q0, tq), :], k_ref[pl.ds(kj, tq), :], v_ref[pl.ds(kj, tq), :], None, r_old)
                (out, rs), = _sb_steps([tile], tri_ref[:tq, :tq], scale)
                acc_ref[ti] += out
                r_new = [r_old[0] + rs[0], r_old[1] + rs[1]]
                r_ref[2 * ti] = r_new[0]
                r_ref[2 * ti + 1] = r_new[1]
                return j - 1, jnp.maximum(jnp.max(r_new[0]), jnp.max(r_new[1]))

            lax.while_loop(cond, body, (kblk - 1, rmax[ti]))
            o_ref[pl.ds(q0, tq), :] = acc_ref[ti].astype(o_ref.dtype)
        return carry

    lax.fori_loop(0, seq // (tq * ATT_TILES_PER_ITER), q_tiles, 0)


def _sb_attention(qkv, bsz, seq):
    tq = ATT_TQ
    assert seq % (tq * ATT_TILES_PER_ITER) == 0 and seq >= 2 * tq
    wcols = ATT_HEADS_PER_STEP * SB_HEAD_DIM
    nhp = SB_WIDTH // wcols
    kern = functools.partial(_sb_attn_kernel, seq=seq, tq=tq, scale=1.0 / math.sqrt(SB_HEAD_DIM))
    return pl.pallas_call(
        kern,
        grid=(bsz, nhp),
        in_specs=[pl.BlockSpec((None, seq, wcols), lambda b, h: (b, 0, h)),
                  pl.BlockSpec((None, seq, wcols), lambda b, h: (b, 0, nhp + h)),
                  pl.BlockSpec((None, seq, wcols), lambda b, h: (b, 0, 2 * nhp + h))],
        out_specs=pl.BlockSpec((None, seq, wcols), lambda b, h: (b, 0, h)),
        out_shape=jax.ShapeDtypeStruct((bsz, seq, SB_WIDTH), BF16),
        scratch_shapes=[pltpu.VMEM((ATT_TILES_PER_ITER, tq, wcols), F32),
                        pltpu.VMEM((ATT_HEADS_PER_STEP * ATT_TILES_PER_ITER, tq, 1), F32),
                        pltpu.VMEM((2 * tq, 2 * tq), BF16)],
        compiler_params=_params(("parallel", "parallel")),
        name="sb_attention",
    )(qkv, qkv, qkv)


def _pool_in_kernel(x_ref, w_ref, o_ref, halo_ref, *, seq, tm):
    i = pl.program_id(0)
    hmax = POOL_WINDOWS[-1]
    acc = jnp.dot(x_ref[...].astype(BF16), w_ref[...], preferred_element_type=F32)
    t0 = (i * tm) % seq

    @pl.when(t0 == 0)
    def _():
        halo_ref[...] = jnp.zeros_like(halo_ref)

    ext = jnp.concatenate([halo_ref[...], acc], axis=0)
    t_idx = t0 + lax.broadcasted_iota(jnp.int32, (tm, POOL_GROUP_WIDTH), 0)
    for gi, win in enumerate(POOL_WINDOWS):
        cs = slice(gi * POOL_GROUP_WIDTH, (gi + 1) * POOL_GROUP_WIDTH)
        e = ext[:, cs]
        span = 1
        while span < win:
            e = e + pltpu.roll(e, span, 0)
            span *= 2
        wsum = e[hmax:, :]
        cnt = jnp.minimum(t_idx + 1, win).astype(F32)
        o_ref[:, cs] = (wsum / cnt - acc[:, cs]).astype(o_ref.dtype)
    halo_ref[...] = acc[tm - hmax:, :]


def _pool_in(x, w, seq, tm):
    t, k = x.shape
    n = w.shape[1]
    return pl.pallas_call(
        functools.partial(_pool_in_kernel, seq=seq, tm=tm),
        grid=(t // tm,),
        in_specs=[pl.BlockSpec((tm, k), lambda i: (i, 0)),
                  pl.BlockSpec((k, n), lambda i: (0, 0))],
        out_specs=pl.BlockSpec((tm, n), lambda i: (i, 0)),
        out_shape=jax.ShapeDtypeStruct((t, n), BF16),
        scratch_shapes=[pltpu.VMEM((POOL_WINDOWS[-1], n), F32)],
        compiler_params=_params(("arbitrary",)),
        name="pool_in",
    )(x, w)


def _layer_norm(x, g, b):
    mu = jnp.mean(x, axis=-1, keepdims=True)
    xc = x - mu
    var = jnp.mean(xc * xc, axis=-1, keepdims=True)
    return xc * lax.rsqrt(var + LN_EPS) * g + b


def _route(h, rwh_ref, rb_ref, base_ref, tm):
    h_hi = h.astype(BF16)
    h_lo = (h - h_hi.astype(F32)).astype(BF16)
    both = jnp.dot(h_hi, rwh_ref[...], preferred_element_type=F32)
    logits = (both[:, :LANES] + both[:, LANES:]
              + jnp.dot(h_lo, rwh_ref[:, :LANES], preferred_element_type=F32)) + rb_ref[...]
    lt = logits.T[:N_EXPERTS, :]
    mx = jnp.max(lt, axis=0, keepdims=True)
    ex = jnp.exp(lt - mx)
    probs = ex / jnp.sum(ex, axis=0, keepdims=True)
    p = [probs[e:e + 1, :] for e in range(N_EXPERTS)]

    def first_max(vals):
        best, idx = vals[0], jnp.zeros_like(vals[0], dtype=jnp.int32)
        for n in range(1, len(vals)):
            upd = vals[n] > best
            idx = jnp.where(upd, n, idx)
            best = jnp.where(upd, vals[n], best)
        return best, idx

    scores = []
    for gi in range(N_EXPERT_GROUPS):
        p0, p1, p2, p3 = p[4 * gi:4 * gi + 4]
        a, b = jnp.maximum(p0, p1), jnp.minimum(p0, p1)
        c, d = jnp.maximum(p2, p3), jnp.minimum(p2, p3)
        scores.append(jnp.maximum(a, c) + jnp.maximum(jnp.minimum(a, c), jnp.maximum(b, d)))
    _, g_sel = first_max(scores)
    sel = []
    for n in range(EXPERTS_PER_GROUP):
        v = p[n]
        for gi in range(1, N_EXPERT_GROUPS):
            v = jnp.where(g_sel == gi, p[4 * gi + n], v)
        sel.append(v)
    v1, i1 = first_max(sel)
    v2, i2 = first_max([jnp.where(i1 == n, -1.0, sel[n]) for n in range(EXPERTS_PER_GROUP)])
    den = v1 + v2
    e1 = g_sel * EXPERTS_PER_GROUP + i1
    e2 = g_sel * EXPERTS_PER_GROUP + i2
    e_iota = lax.broadcasted_iota(jnp.int32, (N_EXPERTS, tm), 0)
    oh1 = e_iota == e1
    oh2 = e_iota == e2
    oh = jnp.where(jnp.logical_or(oh1, oh2), 1.0, 0.0)
    r_i = lax.broadcasted_iota(jnp.int32, (tm, tm), 0)
    c_i = lax.broadcasted_iota(jnp.int32, (tm, tm), 1)
    before = jnp.where(r_i < c_i, 1.0, 0.0).astype(BF16)
    cnt = jnp.dot(oh.astype(BF16), before, preferred_element_type=F32) + base_ref[...]
    rank1 = jnp.sum(jnp.where(oh1, cnt, 0.0), axis=0, keepdims=True)
    rank2 = jnp.sum(jnp.where(oh2, cnt, 0.0), axis=0, keepdims=True)
    base_ref[...] += jnp.sum(oh, axis=1, keepdims=True)
    e_out = jnp.concatenate([e1, e2], axis=0)
    w_out = jnp.concatenate([v1 / den, v2 / den], axis=0)
    r_out = jnp.concatenate([rank1, rank2], axis=0).astype(jnp.int32)
    return e_out, w_out, r_out


def _proj_ln_kernel(*refs, n_x, grouped, tm):
    xs = refs[:n_x]
    ws = refs[n_x:2 * n_x]
    pos = 2 * n_x
    if grouped:
        wg_ref, sc_ref = refs[pos:pos + 2]
        pos += 2
    res_ref, g_ref, b_ref, rwh_ref, rb_ref = refs[pos:pos + 5]
    h_ref, e_ref, gw_ref, rk_ref, cnt_ref, base_ref = refs[pos + 5:pos + 11]

    @pl.when(pl.program_id(0) == 0)
    def _():
        base_ref[...] = jnp.zeros_like(base_ref)

    sm = tm // PROJ_SLABS
    for sl in range(PROJ_SLABS):
        rs = slice(sl * sm, (sl + 1) * sm)
        if grouped:
            x = xs[0][rs, :]
            parts = []
            for gi in range(len(POOL_WINDOWS)):
                cs = slice(gi * POOL_GROUP_WIDTH, (gi + 1) * POOL_GROUP_WIDTH)
                parts.append(jnp.dot(x[:, cs], wg_ref[gi], preferred_element_type=F32))
            y = (jnp.concatenate(parts, axis=1) * sc_ref[...]).astype(BF16)
            mix = jnp.dot(y, ws[0][...], preferred_element_type=F32)
        else:
            mix = jnp.dot(xs[0][rs, :], ws[0][...], preferred_element_type=F32)
            for n in range(1, n_x):
                mix = mix + jnp.dot(xs[n][rs, :], ws[n][...], preferred_element_type=F32)
        h = _layer_norm(ALPHA * res_ref[rs, :] + mix, g_ref[...], b_ref[...])
        h_ref[rs, :] = h
        e_out, w_out, r_out = _route(h, rwh_ref, rb_ref, base_ref, sm)
        e_ref[:, rs] = e_out
        gw_ref[:, rs] = w_out
        rk_ref[:, rs] = r_out
    cnt_ref[...] = jnp.broadcast_to(base_ref[...], cnt_ref.shape)


def _proj_ln_route(xs, ws, res, ln_g, ln_b, router, tm, grouped=None):
    t, d = res.shape
    rwh, rb = router
    n_x = len(xs)
    full = lambda a: pl.BlockSpec(a.shape, lambda i, nd=a.ndim: (0,) * nd)
    in_specs = [pl.BlockSpec((tm, x.shape[1]), lambda i: (i, 0)) for x in xs]
    in_specs += [full(w) for w in ws]
    args = list(xs) + list(ws)
    if grouped is not None:
        in_specs += [full(a) for a in grouped]
        args += list(grouped)
    in_specs += [pl.BlockSpec((tm, d), lambda i: (i, 0)), full(ln_g), full(ln_b), full(rwh), full(rb)]
    args += [res, ln_g, ln_b, rwh, rb]
    pair = lambda dt: jax.ShapeDtypeStruct((2, t), dt)
    out = pl.pallas_call(
        functools.partial(_proj_ln_kernel, n_x=n_x, grouped=grouped is not None, tm=tm),
        grid=(t // tm,),
        in_specs=in_specs,
        out_specs=[pl.BlockSpec((tm, d), lambda i: (i, 0)),
                   pl.BlockSpec((2, tm), lambda i: (0, i)),
                   pl.BlockSpec((2, tm), lambda i: (0, i)),
                   pl.BlockSpec((2, tm), lambda i: (0, i)),
                   pl.BlockSpec((N_EXPERTS, LANES), lambda i: (0, 0))],
        out_shape=[jax.ShapeDtypeStruct((t, d), F32), pair(jnp.int32), pair(F32), pair(jnp.int32),
                   jax.ShapeDtypeStruct((N_EXPERTS, LANES), F32)],
        scratch_shapes=[pltpu.VMEM((N_EXPERTS, 1), F32)],
        compiler_params=_params(("arbitrary",)),
        name="proj_ln_route",
    )(*args)
    h, e, gw, rk, cnt = out
    return h, e, gw, rk, cnt[:, 0]


def _dispatch_kernel(p1_ref, p2_ref, fill_ref, h_ref, xs_ref, zero_ref, sem, *, tm, n_fill):
    base = pl.program_id(0) * tm

    @pl.when(pl.program_id(0) == 0)
    def _():
        zero_ref[...] = jnp.zeros_like(zero_ref)
        n_zero = zero_ref.shape[0]
        fill = lambda k: pltpu.make_async_copy(
            zero_ref, xs_ref.at[pl.ds(pl.multiple_of(jnp.maximum(fill_ref[k], 0), n_zero), n_zero)], sem.at[2])
        for k in range(n_fill):
            pl.when(fill_ref[k] >= 0)(lambda k=k: fill(k).start())
        for k in range(n_fill):
            pl.when(fill_ref[k] >= 0)(lambda k=k: fill(k).wait())

    def issue(r, carry):
        src = h_ref.at[pl.ds(r, 1)]
        pltpu.make_async_copy(src, xs_ref.at[pl.ds(p1_ref[base + r], 1)], sem.at[0]).start()
        pltpu.make_async_copy(src, xs_ref.at[pl.ds(p2_ref[base + r], 1)], sem.at[1]).start()
        return carry

    lax.fori_loop(0, tm, issue, 0, unroll=DMA_ISSUE_UNROLL)
    pltpu.make_async_copy(h_ref, xs_ref.at[pl.ds(0, tm)], sem.at[0]).wait()
    pltpu.make_async_copy(h_ref, xs_ref.at[pl.ds(0, tm)], sem.at[1]).wait()


def _dispatch(h, pos1, pos2, fill_rows, n_rows, tm, tm_fill):
    t, d = h.shape
    return pl.pallas_call(
        functools.partial(_dispatch_kernel, tm=tm, n_fill=fill_rows.shape[0]),
        grid_spec=pltpu.PrefetchScalarGridSpec(
            num_scalar_prefetch=3,
            grid=(t // tm,),
            in_specs=[pl.BlockSpec((tm, d), lambda i, p1, p2, fr: (i, 0))],
            out_specs=pl.BlockSpec(memory_space=pl.ANY),
            scratch_shapes=[pltpu.VMEM((tm_fill, d), h.dtype), pltpu.SemaphoreType.DMA((3,))],
        ),
        out_shape=jax.ShapeDtypeStruct((n_rows, d), h.dtype),
        compiler_params=_params(("arbitrary",)),
        name="moe_dispatch",
    )(pos1, pos2, fill_rows, h)


def _moe_kernel(te_ref, slot_ref, nxt_ref, lo_ref, hi_ref, na_ref, x_ref, wg_hbm, wu_hbm, wd_hbm, o_ref,
                wg_buf, wu_buf, wd_buf, st_g, st_u, st_d, sem, *, layer):
    i = pl.program_id(0)
    rg = wg_buf.shape[1] // MOE_W_GROUPS
    rd = wd_buf.shape[1] // MOE_W_GROUPS

    def copies(e, c, s):
        r0g = pl.multiple_of(c * rg, rg)
        r0d = pl.multiple_of(c * rd, rd)
        return (pltpu.make_async_copy(wg_hbm.at[layer, e, pl.ds(r0g, rg)], st_g.at[s], sem.at[s, 0]),
                pltpu.make_async_copy(wu_hbm.at[layer, e, pl.ds(r0g, rg)], st_u.at[s], sem.at[s, 1]),
                pltpu.make_async_copy(wd_hbm.at[layer, e, pl.ds(r0d, rd)], st_d.at[s], sem.at[s, 2]))

    def start(e, c):
        for cp in copies(e, c, c % 2):
            cp.start()

    def finish(e, c, w_slot):
        s = c % 2
        for cp in copies(e, c, s):
            cp.wait()
        wg_buf[w_slot, pl.ds(pl.multiple_of(c * rg, rg), rg), :] = st_g[s].astype(BF16)
        wu_buf[w_slot, pl.ds(pl.multiple_of(c * rg, rg), rg), :] = st_u[s].astype(BF16)
        wd_buf[w_slot, pl.ds(pl.multiple_of(c * rd, rd), rd), :] = st_d[s].astype(BF16)

    def fetch(e, lo, hi, w_slot):
        def step(c, carry):
            @pl.when(c + 1 < hi)
            def _():
                start(e, c + 1)
            finish(e, c, w_slot)
            return carry
        lax.fori_loop(lo, hi, step, 0)

    @pl.when(i == 0)
    def _():
        start(te_ref[0], 0)
        fetch(te_ref[0], 0, MOE_W_GROUPS, slot_ref[0])

    @pl.when(i < na_ref[0])
    def _():
        lo, hi, nxt, w_slot = lo_ref[i], hi_ref[i], nxt_ref[i], slot_ref[i]

        def issue_first(_, carry):
            start(nxt, lo)
            return carry
        lax.fori_loop(0, (hi > lo).astype(jnp.int32), issue_first, 0)

        x = x_ref[...].astype(BF16)
        gate = jnp.dot(x, wg_buf[w_slot], preferred_element_type=F32)
        up = jnp.dot(x, wu_buf[w_slot], preferred_element_type=F32)
        hid = (jax.nn.silu(gate) * up).astype(BF16)
        o_ref[...] = jnp.dot(hid, wd_buf[w_slot], preferred_element_type=F32)
        fetch(nxt, lo, hi, 1 - w_slot)

    @pl.when(i >= na_ref[0])
    def _():
        o_ref[...] = jnp.zeros_like(o_ref)


def _moe_experts(xs, tables, n_active, w_gate, w_up, w_down, layer, tm):
    p, d = xs.shape
    f = w_gate.shape[3]
    row_blk = lambda i, te, sl, nx, lo, hi, na: (jnp.minimum(i, na[0] - 1), 0)
    hbm = pl.BlockSpec(memory_space=pl.ANY)
    return pl.pallas_call(
        functools.partial(_moe_kernel, layer=layer),
        grid_spec=pltpu.PrefetchScalarGridSpec(
            num_scalar_prefetch=6,
            grid=(p // tm,),
            in_specs=[pl.BlockSpec((tm, d), row_blk), hbm, hbm, hbm],
            out_specs=pl.BlockSpec((tm, d), lambda i, te, sl, nx, lo, hi, na: (i, 0)),
            scratch_shapes=[pltpu.VMEM((2, d, f), BF16), pltpu.VMEM((2, d, f), BF16), pltpu.VMEM((2, f, d), BF16),
                            pltpu.VMEM((2, d // MOE_W_GROUPS, f), F32), pltpu.VMEM((2, d // MOE_W_GROUPS, f), F32),
                            pltpu.VMEM((2, f // MOE_W_GROUPS, d), F32), pltpu.SemaphoreType.DMA((2, 3))],
        ),
        out_shape=jax.ShapeDtypeStruct((p, d), F32),
        compiler_params=_params(("arbitrary",)),
        name="moe_experts",
    )(*tables, n_active, xs, w_gate, w_up, w_down)


def _combine_ln_kernel(p1_ref, p2_ref, h_ref, g1_ref, g2_ref, lg_ref, lb_ref, ys_ref, o_ref,
                       buf1a, buf1b, buf2a, buf2b, sem, *, tm, n_tiles):
    i = pl.program_id(0)
    rows = COMBINE_ROWS_PER_TRIP
    bufs = ((buf1a, buf2a), (buf1b, buf2b))

    def gather(r0, slot):
        base = i * tm
        b1, b2 = bufs[slot]
        for r in range(rows):
            p1 = p1_ref[base + r0 + r]
            p2 = p2_ref[base + r0 + r]
            pltpu.make_async_copy(ys_ref.at[pl.ds(p1, 1)], b1.at[pl.ds(r0 + r, 1)], sem.at[slot, 0]).start()
            pltpu.make_async_copy(ys_ref.at[pl.ds(p2, 1)], b2.at[pl.ds(r0 + r, 1)], sem.at[slot, 1]).start()

    def finish(r0, slot):
        b1, b2 = bufs[slot]
        rs = pl.ds(r0, rows)
        ffn = g1_ref[rs, :] * b1[rs, :] + g2_ref[rs, :] * b2[rs, :]
        o_ref[rs, :] = _layer_norm(ALPHA * h_ref[rs, :] + ffn, lg_ref[...], lb_ref[...])

    def step(gather_slot, finish_slot):
        if finish_slot is not None:
            b1, b2 = bufs[finish_slot]
            pltpu.make_async_copy(ys_ref.at[pl.ds(0, tm)], b1, sem.at[finish_slot, 0]).wait()
            pltpu.make_async_copy(ys_ref.at[pl.ds(0, tm)], b2, sem.at[finish_slot, 1]).wait()

        def trip(k, carry):
            r0 = pl.multiple_of(k * rows, rows)
            if gather_slot is not None:
                gather(r0, gather_slot)
            if finish_slot is not None:
                finish(r0, finish_slot)
            return carry
        lax.fori_loop(0, tm // rows, trip, 0)

    middle = jnp.logical_and(i >= 1, i < n_tiles)
    pl.when(i == 0)(lambda: step(0, None))
    pl.when(jnp.logical_and(middle, i % 2 == 1))(lambda: step(1, 0))
    pl.when(jnp.logical_and(middle, i % 2 == 0))(lambda: step(0, 1))
    pl.when(i == n_tiles)(lambda: step(None, (n_tiles - 1) % 2))


def _combine_ln(h, ys, pos1, pos2, g1, g2, ln_g, ln_b, tm):
    t, d = h.shape
    n_tiles = t // tm
    row = lambda i, p1, p2: (jnp.maximum(i - 1, 0), 0)
    const = lambda i, p1, p2: (0, 0)
    return pl.pallas_call(
        functools.partial(_combine_ln_kernel, tm=tm, n_tiles=n_tiles),
        grid_spec=pltpu.PrefetchScalarGridSpec(
            num_scalar_prefetch=2,
            grid=(n_tiles + 1,),
            in_specs=[pl.BlockSpec((tm, d), row),
                      pl.BlockSpec((tm, 1), row),
                      pl.BlockSpec((tm, 1), row),
                      pl.BlockSpec((1, d), const),
                      pl.BlockSpec((1, d), const),
                      pl.BlockSpec(memory_space=pl.ANY)],
            out_specs=pl.BlockSpec((tm, d), row),
            scratch_shapes=[pltpu.VMEM((tm, d), F32)] * 4 + [pltpu.SemaphoreType.DMA((2, 2))],
        ),
        out_shape=jax.ShapeDtypeStruct((t, d), F32),
        compiler_params=_params(("arbitrary",)),
        name="moe_combine_ln",
    )(pos1, pos2, h, g1, g2, ln_g, ln_b, ys)


def _moe_block(h, e, gw, rk, counts, w_gate, w_up, w_down, layer, ln_g, ln_b, tm_tok):
    t, d = h.shape
    tm = MOE_TM
    n_rows = 2 * t + N_EXPERTS * tm
    n_tiles = n_rows // tm
    cnt = counts.astype(jnp.int32)
    padded = ((cnt + tm - 1) // tm) * tm
    ends = jnp.cumsum(padded)
    offs = ends - padded
    e_ids = jnp.arange(N_EXPERTS, dtype=jnp.int32)
    pos = jnp.sum(jnp.where(e[..., None] == e_ids, offs, 0), axis=-1) + rk
    n_active = jnp.maximum(ends[-1] // tm, 1)
    tile_start = jnp.minimum(jnp.arange(n_tiles, dtype=jnp.int32), n_active - 1) * tm
    tile_expert = jnp.minimum(jnp.sum((ends[None, :] <= tile_start[:, None]).astype(jnp.int32), axis=1),
                              N_EXPERTS - 1)
    last_tile = jnp.where(padded > 0, ends - tm, -1)
    tail = ends[-1] + jnp.arange(N_EXPERTS, dtype=jnp.int32) * tm
    fill_rows = jnp.concatenate([last_tile, jnp.where(tail < n_rows, tail, -1)]).astype(jnp.int32)
    xs = _dispatch(h, pos[0], pos[1], fill_rows, n_rows, tm_tok, tm)
    idx = jnp.arange(n_tiles, dtype=jnp.int32)
    is_first = jnp.concatenate([jnp.ones((1,), bool), tile_expert[1:] != tile_expert[:-1]])
    run_id = jnp.cumsum(is_first.astype(jnp.int32)) - 1
    run_start = lax.cummax(jnp.where(is_first, idx, 0))
    same_run = run_id[:, None] == run_id[None, :]
    run_len = jnp.sum((same_run & (idx[None, :] < n_active)).astype(jnp.int32), axis=1)
    nxt_tile = run_start + run_len
    nxt = jnp.sum(jnp.where(idx[None, :] == nxt_tile[:, None], tile_expert[None, :], 0), axis=1)
    has_next = nxt_tile < n_active
    k_in_run = idx - run_start
    lo = jnp.where(has_next, (MOE_W_GROUPS * k_in_run) // jnp.maximum(run_len, 1), 0)
    hi = jnp.where(has_next, (MOE_W_GROUPS * (k_in_run + 1)) // jnp.maximum(run_len, 1), 0)
    tables = [a.astype(jnp.int32) for a in (tile_expert, run_id % 2, jnp.where(has_next, nxt, 0), lo, hi)]
    ys = _moe_experts(xs, tables, n_active.reshape(1).astype(jnp.int32), w_gate, w_up, w_down, layer, tm)
    return _combine_ln(h, ys, pos[0], pos[1], gw[0].reshape(t, 1), gw[1].reshape(t, 1), ln_g, ln_b, tm_tok)


def kernel(x, ab_w_in, ab_lambda_re, ab_lambda_im, ab_log_dt, ab_b_re, ab_b_im, ab_c_re, ab_c_im, ab_d,
           ab_w_glu, ab_b_glu, ab_w_out, c_w_in, c_w_group, c_scale, c_w_out, ln_g, ln_b, router_w,
           router_b, moe_w_gate, moe_w_up, moe_w_down):
    bsz, seq, d = x.shape
    t = bsz * seq
    tm = min(ROW_TILE, t)
    assert t % tm == 0 and seq % tm == 0 and seq % S5_CHUNK == 0
    row = lambda a: a.reshape(1, -1)

    rw = jnp.zeros((d, LANES), F32).at[:, :N_EXPERTS].set(router_w)
    rw_hi = rw.astype(BF16)
    rw_lo = (rw - rw_hi.astype(F32)).astype(BF16)
    rb = jnp.zeros((1, LANES), F32).at[0, :N_EXPERTS].set(router_b)
    router = (jnp.concatenate([rw_hi, rw_lo], axis=1), rb)

    h = x.reshape(t, d)

    wg_b, wu_b, wd_b = moe_w_gate, moe_w_up, moe_w_down
    u, u_chunked, qkv = _in_proj_ab(h, ab_w_in[0].astype(BF16), min(2 * ROW_TILE, t))
    chunks = seq // S5_CHUNK
    nsteps = max(int(math.log2(chunks)), 0)
    assert (1 << nsteps) == chunks
    ops = _s5_operands(ab_lambda_re[0], ab_lambda_im[0], ab_log_dt[0], ab_b_re[0], ab_b_im[0],
                       ab_c_re[0], ab_c_im[0], S5_CHUNK, nsteps)
    y_ssm = _s5_scan(u_chunked, ops, bsz, seq)
    y_a = _s5_glu(y_ssm, u, ab_d[0], ab_w_glu[0], ab_b_glu[0], tm)
    y_b = _sb_attention(qkv.reshape(bsz, seq, 3 * SB_WIDTH), bsz, seq).reshape(t, SB_WIDTH)
    w_out = ab_w_out[0].astype(BF16)
    h, e, gw, rk, cnt = _proj_ln_route([y_a, y_b], [w_out[:S5_WIDTH], w_out[S5_WIDTH:]], h,
                                       row(ln_g[0, 0]), row(ln_b[0, 0]), router, tm)
    h = _moe_block(h, e, gw, rk, cnt, wg_b, wu_b, wd_b, 0, row(ln_g[0, 1]), row(ln_b[0, 1]), tm)

    pooled = _pool_in(h, c_w_in[0].astype(BF16), seq, tm)
    h, e, gw, rk, cnt = _proj_ln_route([pooled], [c_w_out[0].astype(BF16)], h,
                                       row(ln_g[1, 0]), row(ln_b[1, 0]), router, tm,
                                       grouped=(c_w_group[0].astype(BF16), row(c_scale[0])))
    h = _moe_block(h, e, gw, rk, cnt, wg_b, wu_b, wd_b, 1, row(ln_g[1, 1]), row(ln_b[1, 1]), tm)
    return h.reshape(bsz, seq, d)
```

```python
import functools
import math

import jax
import jax.numpy as jnp
from jax import lax
from jax.experimental import pallas as pl
from jax.experimental.pallas import tpu as pltpu

S5_WIDTH = 512
S5_GROUP = 16
S5_GROUPS = 32
S5_STATE = 64
SB_HEAD_DIM = 128
SB_HEADS = 12
SB_WIDTH = SB_HEADS * SB_HEAD_DIM
POOL_WINDOWS = (2, 4, 8, 16)
POOL_GROUP_WIDTH = 512
N_EXPERTS = 16
N_EXPERT_GROUPS = 4
EXPERTS_PER_GROUP = 4
DEPTH = 2
ALPHA = (2.0 * DEPTH) ** 0.25
LN_EPS = 1e-5

LANES = 128
VMEM_LIMIT = 56 * 1024 * 1024

S5_CHUNK = 32
ATT_TQ = 128
ATT_HEADS_PER_STEP = 2
ATT_TILES_PER_ITER = 8
ATT_LOG_UNDERFLOW = -88.0
ATT_MASKED_SCORE = -1e30
MOE_TM = 256
MOE_W_GROUPS = 8
ROW_TILE = 512
PROJ_SLABS = 2
DMA_ISSUE_UNROLL = 8
COMBINE_ROWS_PER_TRIP = 64

BF16 = jnp.bfloat16
F32 = jnp.float32


def _params(sem, vmem=VMEM_LIMIT):
    return pltpu.CompilerParams(dimension_semantics=sem, vmem_limit_bytes=vmem)


def _piece_transpose(blocks):
    v = list(blocks)
    n = len(v)
    assert n * S5_GROUP == LANES
    piece = lax.broadcasted_iota(jnp.int32, v[0].shape, 1) // S5_GROUP
    d = n // 2
    while d >= 1:
        upper = (piece & d) != 0
        for q in range(n):
            if q & d == 0:
                a, b = v[q], v[q + d]
                v[q] = jnp.where(upper, pltpu.roll(b, d * S5_GROUP, 1), a)
                v[q + d] = jnp.where(upper, b, pltpu.roll(a, LANES - d * S5_GROUP, 1))
        d //= 2
    return v


def _in_proj_ab_kernel(x_ref, w_ref, u_ref, up_ref, qkv_ref, xb_ref, us_ref):
    j = pl.program_id(1)

    @pl.when(j == 0)
    def _():
        xb_ref[...] = x_ref[...].astype(BF16)

    acc = jnp.dot(xb_ref[...], w_ref[...], preferred_element_type=F32)

    @pl.when(j == 0)
    def _():
        u_ref[...] = acc
        chunk = S5_CHUNK
        n_chunks = acc.shape[0] // chunk
        per = LANES // S5_GROUP
        for g1 in range(S5_WIDTH // LANES):
            us_ref[g1] = acc[:, g1 * LANES:(g1 + 1) * LANES]
        for s1 in range(chunk // per):
            for g1 in range(S5_WIDTH // LANES):
                rows = [us_ref[g1, pl.ds(s1 * per + s2, n_chunks, stride=chunk), :] for s2 in range(per)]
                regrouped = _piece_transpose(rows)
                for g2 in range(per):
                    up_ref[g1 * per + g2, :, s1 * LANES:(s1 + 1) * LANES] = regrouped[g2].astype(up_ref.dtype)

    @pl.when(j > 0)
    def _():
        qkv_ref[...] = acc.astype(qkv_ref.dtype)


def _in_proj_ab(x, w, tm):
    m, k = x.shape
    tn = S5_WIDTH
    n_blocks = w.shape[1] // tn
    return pl.pallas_call(
        _in_proj_ab_kernel,
        grid=(m // tm, n_blocks),
        in_specs=[pl.BlockSpec((tm, k), lambda i, j: (i, 0)),
                  pl.BlockSpec((k, tn), lambda i, j: (0, j))],
        out_specs=[pl.BlockSpec((tm, tn), lambda i, j: (i, 0)),
                   pl.BlockSpec((S5_GROUPS, tm // S5_CHUNK, tn), lambda i, j: (0, i, 0)),
                   pl.BlockSpec((tm, tn), lambda i, j: (i, jnp.maximum(j - 1, 0)))],
        out_shape=[jax.ShapeDtypeStruct((m, tn), F32),
                   jax.ShapeDtypeStruct((S5_GROUPS, m // S5_CHUNK, tn), BF16),
                   jax.ShapeDtypeStruct((m, w.shape[1] - tn), BF16)],
        scratch_shapes=[pltpu.VMEM((tm, k), BF16), pltpu.VMEM((tn // LANES, tm, LANES), F32)],
        compiler_params=_params(("parallel", "arbitrary")),
        name="in_proj_ab",
    )(x, w)


def _bmm_f32_kernel(x_ref, y_ref, o_ref):
    o_ref[...] = jnp.dot(x_ref[...], y_ref[...], preferred_element_type=F32,
                         precision=lax.Precision.HIGHEST)


def _bmm_f32(x, y):
    g, m, k = x.shape
    n = y.shape[2]
    return pl.pallas_call(
        _bmm_f32_kernel,
        grid=(g,),
        in_specs=[pl.BlockSpec((None, m, k), lambda i: (i, 0, 0)),
                  pl.BlockSpec((None, k, n), lambda i: (i, 0, 0))],
        out_specs=pl.BlockSpec((None, m, n), lambda i: (i, 0, 0)),
        out_shape=jax.ShapeDtypeStruct((g, m, n), F32),
        compiler_params=_params(("parallel",)),
        name="s5_taps",
    )(x, y)


def _s5_kernel(u_ref, r0_ref, ws_ref, wo_ref, a1_ref, a2_ref, o_ref, wt_ref, *, chunks, nsteps):
    h = S5_GROUP
    r0 = r0_ref[...]
    lane = lax.broadcasted_iota(jnp.int32, r0.shape, 1)
    for j in range(r0.shape[1] // h):
        blk = r0 if j == 0 else jnp.where(lane >= j * h, pltpu.roll(r0, j * h, 1), 0.0)
        wt_ref[j * h:(j + 1) * h, :] = blk.astype(BF16)
    u = u_ref[...]
    y_intra = jnp.dot(u, wt_ref[...], preferred_element_type=F32)
    s = jnp.dot(u, ws_ref[...], preferred_element_type=F32)
    m = s.shape[0]
    c_idx = lax.broadcasted_iota(jnp.int32, s.shape, 0) % chunks
    half = s.shape[1] // 2
    for k in range(nsteps):
        d = 1 << k
        sh = jnp.where(c_idx >= d, pltpu.roll(s, d, 0), 0.0)
        s = s + a1_ref[k:k + 1, :] * sh + a2_ref[k:k + 1, :] * pltpu.roll(sh, half, 1)
    s_prev = jnp.where(c_idx >= 1, pltpu.roll(s, 1, 0), 0.0)
    y_inter = jnp.dot(s_prev.astype(BF16), wo_ref[...], preferred_element_type=F32)
    o_ref[...] = y_intra + y_inter


def _s5_operands(lam_re, lam_im, log_dt, b_re, b_im, c_re, c_im, chunk, nsteps):
    g, n = lam_re.shape
    h = S5_GROUP
    dt = jnp.exp(log_dt)[:, None]
    lr, li = lam_re, lam_im
    mag = jnp.exp(lr * dt)
    lbr = mag * jnp.cos(li * dt)
    lbi = mag * jnp.sin(li * dt)
    den = lr * lr + li * li
    coef_re = ((lbr - 1.0) * lr + lbi * li) / den
    coef_im = (lbi * lr - (lbr - 1.0) * li) / den
    bbar_re = coef_re[..., None] * b_re - coef_im[..., None] * b_im
    bbar_im = coef_re[..., None] * b_im + coef_im[..., None] * b_re

    def power(mult):
        mult = mult.astype(F32)[None, :, None]
        pm = jnp.exp(lr[:, None, :] * dt[:, None, :] * mult)
        ang = li[:, None, :] * dt[:, None, :] * mult
        return pm * jnp.cos(ang), pm * jnp.sin(ang)

    pr, pi = power(jnp.arange(chunk + 1))
    cre = jnp.swapaxes(c_re, 1, 2)
    cim = jnp.swapaxes(c_im, 1, 2)
    cb_re = cre[:, :, :, None] * bbar_re[:, :, None, :] - cim[:, :, :, None] * bbar_im[:, :, None, :]
    cb_im = cre[:, :, :, None] * bbar_im[:, :, None, :] + cim[:, :, :, None] * bbar_re[:, :, None, :]
    y_op = jnp.concatenate([cb_re, cb_im], axis=1).reshape(g, 2 * n, h * h)
    x_op = jnp.concatenate([pr[:, :chunk], -pi[:, :chunk]], axis=-1)
    taps = _bmm_f32(x_op, y_op).reshape(g, chunk, h, h)
    r0 = taps.transpose(0, 3, 1, 2).reshape(g, h, chunk * h)
    prr = pr[:, chunk - 1::-1][:, :chunk]
    pir = pi[:, chunk - 1::-1][:, :chunk]
    ws_re = prr[:, :, None, :] * jnp.swapaxes(bbar_re, 1, 2)[:, None] - pir[:, :, None, :] * jnp.swapaxes(bbar_im, 1, 2)[:, None]
    ws_im = prr[:, :, None, :] * jnp.swapaxes(bbar_im, 1, 2)[:, None] + pir[:, :, None, :] * jnp.swapaxes(bbar_re, 1, 2)[:, None]
    ws = jnp.concatenate([ws_re, ws_im], axis=-1).reshape(g, chunk * h, 2 * n)
    p1r = pr[:, 1:chunk + 1]
    p1i = pi[:, 1:chunk + 1]
    wo_re = c_re[:, None] * p1r[:, :, None, :] - c_im[:, None] * p1i[:, :, None, :]
    wo_im = -(c_re[:, None] * p1i[:, :, None, :] + c_im[:, None] * p1r[:, :, None, :])
    wo = jnp.concatenate([wo_re, wo_im], axis=-1)
    wo = wo.transpose(0, 3, 1, 2).reshape(g, 2 * n, chunk * h)
    sr, si = power(chunk * (2 ** jnp.arange(max(nsteps, 1))))
    a1 = jnp.concatenate([sr, sr], axis=-1)
    a2 = jnp.concatenate([-si, si], axis=-1)
    return r0, ws.astype(BF16), wo.astype(BF16), a1, a2


def _s5_scan(up, ops, bsz, seq):
    r0, ws, wo, a1, a2 = ops
    g, h, chunk = S5_GROUPS, S5_GROUP, S5_CHUNK
    chunks = seq // chunk
    m = bsz * chunks
    nsteps = a1.shape[1] if chunks > 1 else 0
    n2 = ws.shape[2]
    ns = a1.shape[1]
    return pl.pallas_call(
        functools.partial(_s5_kernel, chunks=chunks, nsteps=nsteps),
        grid=(g,),
        in_specs=[pl.BlockSpec((None, m, chunk * h), lambda i: (i, 0, 0)),
                  pl.BlockSpec((None, h, chunk * h), lambda i: (i, 0, 0)),
                  pl.BlockSpec((None, chunk * h, n2), lambda i: (i, 0, 0)),
                  pl.BlockSpec((None, n2, chunk * h), lambda i: (i, 0, 0)),
                  pl.BlockSpec((None, ns, n2), lambda i: (i, 0, 0)),
                  pl.BlockSpec((None, ns, n2), lambda i: (i, 0, 0))],
        out_specs=pl.BlockSpec((None, m, chunk * h), lambda i: (i, 0, 0)),
        out_shape=jax.ShapeDtypeStruct((g, m, chunk * h), F32),
        scratch_shapes=[pltpu.VMEM((chunk * h, chunk * h), BF16)],
        compiler_params=_params(("parallel",)),
        name="s5_scan",
    )(up, r0, ws, wo, a1, a2)


def _s5_glu_kernel(yp_ref, u_ref, d_ref, w_ref, b_ref, o_ref, y_ref):
    chunk = S5_CHUNK
    n_chunks = yp_ref.shape[1]
    per = LANES // S5_GROUP
    for s1 in range(chunk // per):
        for g1 in range(S5_WIDTH // LANES):
            by_step = _piece_transpose([yp_ref[g1 * per + g2, :, s1 * LANES:(s1 + 1) * LANES] for g2 in range(per)])
            for s2 in range(per):
                y_ref[g1, pl.ds(s1 * per + s2, n_chunks, stride=chunk), :] = by_step[s2]
    y_ssm = jnp.concatenate([y_ref[g1] for g1 in range(S5_WIDTH // LANES)], axis=1)
    y = y_ssm + d_ref[...] * u_ref[...]
    gl = jax.nn.gelu(y)
    gate = jnp.dot(gl.astype(BF16), w_ref[...], preferred_element_type=F32) + b_ref[...]
    o_ref[...] = (gl * jax.nn.sigmoid(gate)).astype(o_ref.dtype)


def _s5_glu(yp, u, d, w_glu, b_glu, tm):
    t, c = u.shape
    return pl.pallas_call(
        _s5_glu_kernel,
        grid=(t // tm,),
        in_specs=[pl.BlockSpec((S5_GROUPS, tm // S5_CHUNK, c), lambda i: (0, i, 0)),
                  pl.BlockSpec((tm, c), lambda i: (i, 0)),
                  pl.BlockSpec((1, c), lambda i: (0, 0)),
                  pl.BlockSpec((c, c), lambda i: (0, 0)),
                  pl.BlockSpec((1, c), lambda i: (0, 0))],
        out_specs=pl.BlockSpec((tm, c), lambda i: (i, 0)),
        out_shape=jax.ShapeDtypeStruct((t, c), BF16),
        scratch_shapes=[pltpu.VMEM((c // LANES, tm, LANES), F32)],
        compiler_params=_params(("parallel",)),
        name="s5_glu",
    )(yp, u, d.reshape(1, c), w_glu.astype(BF16), b_glu.reshape(1, c))


def _sb_steps(tiles, tri, scale):
    dh = SB_HEAD_DIM
    nk = tiles[0][1].shape[0]
    left = lax.broadcasted_iota(jnp.int32, (nk, 2 * dh), 1) < dh
    pre = []
    stack = []
    for q2, k2, v2, causal, _ in tiles:
        zero = jnp.zeros_like(k2)
        k_bd = jnp.concatenate([jnp.where(left, k2, zero), jnp.where(left, zero, k2)], axis=0)
        z2 = lax.dot_general(q2, k_bd, (((1,), (1,)), ((), ())), preferred_element_type=F32) * scale
        per_head = []
        for hh in range(2):
            z = z2[:, hh * nk:(hh + 1) * nk]
            if causal is not None:
                z = jnp.where(causal, z, ATT_MASKED_SCORE)
            sp = jnp.maximum(z, 0.0) + jnp.log(1.0 + jnp.exp(-jnp.abs(z)))
            log_keep = -sp
            hi = log_keep.astype(BF16)
            lo = (log_keep - hi.astype(F32)).astype(BF16)
            stack += [hi, lo]
            per_head.append((z - sp, log_keep))
        pre.append(per_head)
    tq = tiles[0][0].shape[0]
    sums = jnp.dot(jnp.concatenate(stack, axis=0), tri, preferred_element_type=F32)
    results = []
    for ti, (q2, k2, v2, causal, r_in) in enumerate(tiles):
        ws, rs = [], []
        for hh in range(2):
            log_beta, log_keep = pre[ti][hh]
            base = (ti * 2 + hh) * 2 * tq
            tail = sums[base:base + tq] + sums[base + tq:base + 2 * tq]
            if r_in is not None:
                tail = tail + r_in[hh]
            ws.append(jnp.exp(log_beta + tail).astype(BF16))
            rs.append(jnp.sum(log_keep, axis=1, keepdims=True))
        zero = jnp.zeros_like(v2)
        v_bd = jnp.concatenate([jnp.where(left, v2, zero), jnp.where(left, zero, v2)], axis=0)
        out = jnp.dot(jnp.concatenate(ws, axis=1), v_bd, preferred_element_type=F32)
        results.append((out, rs))
    return results


def _sb_attn_kernel(q_ref, k_ref, v_ref, o_ref, acc_ref, r_ref, tri_ref, *, seq, tq, scale):
    dh = SB_HEAD_DIM
    win = 2 * tq
    row = lax.broadcasted_iota(jnp.int32, (win, win), 0)
    col = lax.broadcasted_iota(jnp.int32, (win, win), 1)
    tri_ref[...] = jnp.where(row > col, 1.0, 0.0).astype(BF16)
    assert ATT_HEADS_PER_STEP == 2

    def q_tiles(p, carry):
        first, where = [], []
        for ti in range(ATT_TILES_PER_ITER):
            i = p * ATT_TILES_PER_ITER + ti
            q0 = pl.multiple_of(i * tq, tq)
            kblk = jnp.maximum(i - 1, 0)
            k0 = pl.multiple_of(kblk * tq, tq)
            q_pos = q0 + lax.broadcasted_iota(jnp.int32, (tq, win), 0)
            causal = (k0 + lax.broadcasted_iota(jnp.int32, (tq, win), 1)) < q_pos
            first.append((q_ref[pl.ds(q0, tq), :], k_ref[pl.ds(k0, win), :], v_ref[pl.ds(k0, win), :], causal, None))
            where.append((q0, kblk))
        rmax = []
        for ti, (out, rs) in enumerate(_sb_steps(first, tri_ref[...], scale)):
            acc_ref[ti] = out
            r_ref[2 * ti] = rs[0]
            r_ref[2 * ti + 1] = rs[1]
            rmax.append(jnp.maximum(jnp.max(rs[0]), jnp.max(rs[1])))
        for ti, (q0, kblk) in enumerate(where):
            def cond(c):
                j, rm = c
                return jnp.logical_and(j >= 0, rm > ATT_LOG_UNDERFLOW)

            def body(c, ti=ti, q0=q0):
                j, _ = c
                kj = pl.multiple_of(j * tq, tq)
                r_old = [r_ref[2 * ti], r_ref[2 * ti + 1]]
                tile = (q_ref[pl.ds(q0, tq), :], k_ref[pl.ds(kj, tq), :], v_ref[pl.ds(kj, tq), :], None, r_old)
                (out, rs), = _sb_steps([tile], tri_ref[:tq, :tq], scale)
                acc_ref[ti] += out
                r_new = [r_old[0] + rs[0], r_old[1] + rs[1]]
                r_ref[2 * ti] = r_new[0]
                r_ref[2 * ti + 1] = r_new[1]
                return j - 1, jnp.maximum(jnp.max(r_new[0]), jnp.max(r_new[1]))

            lax.while_loop(cond, body, (kblk - 1, rmax[ti]))
            o_ref[pl.ds(q0, tq), :] = acc_ref[ti].astype(o_ref.dtype)
        return carry

    lax.fori_loop(0, seq // (tq * ATT_TILES_PER_ITER), q_tiles, 0)


def _sb_attention(qkv, bsz, seq):
    tq = ATT_TQ
    assert seq % (tq * ATT_TILES_PER_ITER) == 0 and seq >= 2 * tq
    wcols = ATT_HEADS_PER_STEP * SB_HEAD_DIM
    nhp = SB_WIDTH // wcols
    kern = functools.partial(_sb_attn_kernel, seq=seq, tq=tq, scale=1.0 / math.sqrt(SB_HEAD_DIM))
    return pl.pallas_call(
        kern,
        grid=(bsz, nhp),
        in_specs=[pl.BlockSpec((None, seq, wcols), lambda b, h: (b, 0, h)),
                  pl.BlockSpec((None, seq, wcols), lambda b, h: (b, 0, nhp + h)),
                  pl.BlockSpec((None, seq, wcols), lambda b, h: (b, 0, 2 * nhp + h))],
        out_specs=pl.BlockSpec((None, seq, wcols), lambda b, h: (b, 0, h)),
        out_shape=jax.ShapeDtypeStruct((bsz, seq, SB_WIDTH), BF16),
        scratch_shapes=[pltpu.VMEM((ATT_TILES_PER_ITER, tq, wcols), F32),
                        pltpu.VMEM((ATT_HEADS_PER_STEP * ATT_TILES_PER_ITER, tq, 1), F32),
                        pltpu.VMEM((2 * tq, 2 * tq), BF16)],
        compiler_params=_params(("parallel", "parallel")),
        name="sb_attention",
    )(qkv, qkv, qkv)


def _pool_in_kernel(x_ref, w_ref, o_ref, halo_ref, *, seq, tm):
    i = pl.program_id(0)
    hmax = POOL_WINDOWS[-1]
    acc = jnp.dot(x_ref[...].astype(BF16), w_ref[...], preferred_element_type=F32)
    t0 = (i * tm) % seq

    @pl.when(t0 == 0)
    def _():
        halo_ref[...] = jnp.zeros_like(halo_ref)

    ext = jnp.concatenate([halo_ref[...], acc], axis=0)
    t_idx = t0 + lax.broadcasted_iota(jnp.int32, (tm, POOL_GROUP_WIDTH), 0)
    for gi, win in enumerate(POOL_WINDOWS):
        cs = slice(gi * POOL_GROUP_WIDTH, (gi + 1) * POOL_GROUP_WIDTH)
        e = ext[:, cs]
        span = 1
        while span < win:
            e = e + pltpu.roll(e, span, 0)
            span *= 2
        wsum = e[hmax:, :]
        cnt = jnp.minimum(t_idx + 1, win).astype(F32)
        o_ref[:, cs] = (wsum / cnt - acc[:, cs]).astype(o_ref.dtype)
    halo_ref[...] = acc[tm - hmax:, :]


def _pool_in(x, w, seq, tm):
    t, k = x.shape
    n = w.shape[1]
    return pl.pallas_call(
        functools.partial(_pool_in_kernel, seq=seq, tm=tm),
        grid=(t // tm,),
        in_specs=[pl.BlockSpec((tm, k), lambda i: (i, 0)),
                  pl.BlockSpec((k, n), lambda i: (0, 0))],
        out_specs=pl.BlockSpec((tm, n), lambda i: (i, 0)),
        out_shape=jax.ShapeDtypeStruct((t, n), BF16),
        scratch_shapes=[pltpu.VMEM((POOL_WINDOWS[-1], n), F32)],
        compiler_params=_params(("arbitrary",)),
        name="pool_in",
    )(x, w)


def _layer_norm(x, g, b):
    mu = jnp.mean(x, axis=-1, keepdims=True)
    xc = x - mu
    var = jnp.mean(xc * xc, axis=-1, keepdims=True)
    return xc * lax.rsqrt(var + LN_EPS) * g + b


def _route(h, rwh_ref, rb_ref, base_ref, tm):
    h_hi = h.astype(BF16)
    h_lo = (h - h_hi.astype(F32)).astype(BF16)
    both = jnp.dot(h_hi, rwh_ref[...], preferred_element_type=F32)
    logits = (both[:, :LANES] + both[:, LANES:]
              + jnp.dot(h_lo, rwh_ref[:, :LANES], preferred_element_type=F32)) + rb_ref[...]
    lt = logits.T[:N_EXPERTS, :]
    mx = jnp.max(lt, axis=0, keepdims=True)
    ex = jnp.exp(lt - mx)
    probs = ex / jnp.sum(ex, axis=0, keepdims=True)
    p = [probs[e:e + 1, :] for e in range(N_EXPERTS)]

    def first_max(vals):
        best, idx = vals[0], jnp.zeros_like(vals[0], dtype=jnp.int32)
        for n in range(1, len(vals)):
            upd = vals[n] > best
            idx = jnp.where(upd, n, idx)
            best = jnp.where(upd, vals[n], best)
        return best, idx

    scores = []
    for gi in range(N_EXPERT_GROUPS):
        p0, p1, p2, p3 = p[4 * gi:4 * gi + 4]
        a, b = jnp.maximum(p0, p1), jnp.minimum(p0, p1)
        c, d = jnp.maximum(p2, p3), jnp.minimum(p2, p3)
        scores.append(jnp.maximum(a, c) + jnp.maximum(jnp.minimum(a, c), jnp.maximum(b, d)))
    _, g_sel = first_max(scores)
    sel = []
    for n in range(EXPERTS_PER_GROUP):
        v = p[n]
        for gi in range(1, N_EXPERT_GROUPS):
            v = jnp.where(g_sel == gi, p[4 * gi + n], v)
        sel.append(v)
    v1, i1 = first_max(sel)
    v2, i2 = first_max([jnp.where(i1 == n, -1.0, sel[n]) for n in range(EXPERTS_PER_GROUP)])
    den = v1 + v2
    e1 = g_sel * EXPERTS_PER_GROUP + i1
    e2 = g_sel * EXPERTS_PER_GROUP + i2
    e_iota = lax.broadcasted_iota(jnp.int32, (N_EXPERTS, tm), 0)
    oh1 = e_iota == e1
    oh2 = e_iota == e2
    oh = jnp.where(jnp.logical_or(oh1, oh2), 1.0, 0.0)
    r_i = lax.broadcasted_iota(jnp.int32, (tm, tm), 0)
    c_i = lax.broadcasted_iota(jnp.int32, (tm, tm), 1)
    before = jnp.where(r_i < c_i, 1.0, 0.0).astype(BF16)
    cnt = jnp.dot(oh.astype(BF16), before, preferred_element_type=F32) + base_ref[...]
    rank1 = jnp.sum(jnp.where(oh1, cnt, 0.0), axis=0, keepdims=True)
    rank2 = jnp.sum(jnp.where(oh2, cnt, 0.0), axis=0, keepdims=True)
    base_ref[...] += jnp.sum(oh, axis=1, keepdims=True)
    e_out = jnp.concatenate([e1, e2], axis=0)
    w_out = jnp.concatenate([v1 / den, v2 / den], axis=0)
    r_out = jnp.concatenate([rank1, rank2], axis=0).astype(jnp.int32)
    return e_out, w_out, r_out


def _proj_ln_kernel(*refs, n_x, grouped, tm, n_tiles):
    xs = refs[:n_x]
    ws = refs[n_x:2 * n_x]
    pos = 2 * n_x
    if grouped:
        wg_ref, sc_ref = refs[pos:pos + 2]
        pos += 2
    res_ref, g_ref, b_ref, rwh_ref, rb_ref = refs[pos:pos + 5]
    h_ref, e_ref, gw_ref, rk_ref, cnt_ref, base_ref, mix_a, mix_b = refs[pos + 5:pos + 13]
    i = pl.program_id(0)
    mix_bufs = (mix_a, mix_b)
    sm = tm // PROJ_SLABS

    @pl.when(i == 0)
    def _():
        base_ref[...] = jnp.zeros_like(base_ref)

    def project(slot):
        for sl in range(PROJ_SLABS):
            rs = slice(sl * sm, (sl + 1) * sm)
            if grouped:
                x = xs[0][rs, :]
                parts = []
                for gi in range(len(POOL_WINDOWS)):
                    cs = slice(gi * POOL_GROUP_WIDTH, (gi + 1) * POOL_GROUP_WIDTH)
                    parts.append(jnp.dot(x[:, cs], wg_ref[gi], preferred_element_type=F32))
                y = (jnp.concatenate(parts, axis=1) * sc_ref[...]).astype(BF16)
                mix = jnp.dot(y, ws[0][...], preferred_element_type=F32)
            else:
                mix = jnp.dot(xs[0][rs, :], ws[0][...], preferred_element_type=F32)
                for n in range(1, n_x):
                    mix = mix + jnp.dot(xs[n][rs, :], ws[n][...], preferred_element_type=F32)
            mix_bufs[slot][rs, :] = mix

    def normalise_and_route(slot):
        for sl in range(PROJ_SLABS):
            rs = slice(sl * sm, (sl + 1) * sm)
            h = _layer_norm(ALPHA * res_ref[rs, :] + mix_bufs[slot][rs, :], g_ref[...], b_ref[...])
            h_ref[rs, :] = h
            e_out, w_out, r_out = _route(h, rwh_ref, rb_ref, base_ref, sm)
            e_ref[:, rs] = e_out
            gw_ref[:, rs] = w_out
            rk_ref[:, rs] = r_out
        cnt_ref[...] = jnp.broadcast_to(base_ref[...], cnt_ref.shape)

    def step(project_slot, finish_slot):
        if project_slot is not None:
            project(project_slot)
        if finish_slot is not None:
            normalise_and_route(finish_slot)

    middle = jnp.logical_and(i >= 1, i < n_tiles)
    pl.when(i == 0)(lambda: step(0, None))
    pl.when(jnp.logical_and(middle, i % 2 == 1))(lambda: step(1, 0))
    pl.when(jnp.logical_and(middle, i % 2 == 0))(lambda: step(0, 1))
    pl.when(i == n_tiles)(lambda: step(None, (n_tiles - 1) % 2))


def _proj_ln_route(xs, ws, res, ln_g, ln_b, router, tm, grouped=None):
    t, d = res.shape
    rwh, rb = router
    n_x = len(xs)
    n_tiles = t // tm
    full = lambda a: pl.BlockSpec(a.shape, lambda i, nd=a.ndim: (0,) * nd, pipeline_mode=pl.Buffered(1))
    cur = lambda i: (jnp.minimum(i, n_tiles - 1), 0)
    prv = lambda i: (jnp.maximum(i - 1, 0), 0)
    prv_t = lambda i: (0, jnp.maximum(i - 1, 0))
    in_specs = [pl.BlockSpec((tm, x.shape[1]), cur) for x in xs]
    in_specs += [full(w) for w in ws]
    args = list(xs) + list(ws)
    if grouped is not None:
        in_specs += [full(a) for a in grouped]
        args += list(grouped)
    in_specs += [pl.BlockSpec((tm, d), prv), full(ln_g), full(ln_b), full(rwh), full(rb)]
    args += [res, ln_g, ln_b, rwh, rb]
    pair = lambda dt: jax.ShapeDtypeStruct((2, t), dt)
    out = pl.pallas_call(
        functools.partial(_proj_ln_kernel, n_x=n_x, grouped=grouped is not None, tm=tm, n_tiles=n_tiles),
        grid=(n_tiles + 1,),
        in_specs=in_specs,
        out_specs=[pl.BlockSpec((tm, d), prv),
                   pl.BlockSpec((2, tm), prv_t),
                   pl.BlockSpec((2, tm), prv_t),
                   pl.BlockSpec((2, tm), prv_t),
                   pl.BlockSpec((N_EXPERTS, LANES), lambda i: (0, 0))],
        out_shape=[jax.ShapeDtypeStruct((t, d), F32), pair(jnp.int32), pair(F32), pair(jnp.int32),
                   jax.ShapeDtypeStruct((N_EXPERTS, LANES), F32)],
        scratch_shapes=[pltpu.VMEM((N_EXPERTS, 1), F32), pltpu.VMEM((tm, d), F32), pltpu.VMEM((tm, d), F32)],
        compiler_params=_params(("arbitrary",)),
        name="proj_ln_route",
    )(*args)
    h, e, gw, rk, cnt = out
    return h, e, gw, rk, cnt[:, 0]


def _dispatch_kernel(p1_ref, p2_ref, fill_ref, h_ref, xs_ref, zero_ref, sem, *, tm, n_fill):
    base = pl.program_id(0) * tm

    @pl.when(pl.program_id(0) == 0)
    def _():
        zero_ref[...] = jnp.zeros_like(zero_ref)
        n_zero = zero_ref.shape[0]
        fill = lambda k: pltpu.make_async_copy(
            zero_ref, xs_ref.at[pl.ds(pl.multiple_of(jnp.maximum(fill_ref[k], 0), n_zero), n_zero)], sem.at[2])
        for k in range(n_fill):
            pl.when(fill_ref[k] >= 0)(lambda k=k: fill(k).start())
        for k in range(n_fill):
            pl.when(fill_ref[k] >= 0)(lambda k=k: fill(k).wait())

    def issue(r, carry):
        src = h_ref.at[pl.ds(r, 1)]
        pltpu.make_async_copy(src, xs_ref.at[pl.ds(p1_ref[base + r], 1)], sem.at[0]).start()
        pltpu.make_async_copy(src, xs_ref.at[pl.ds(p2_ref[base + r], 1)], sem.at[1]).start()
        return carry

    lax.fori_loop(0, tm, issue, 0, unroll=DMA_ISSUE_UNROLL)
    pltpu.make_async_copy(h_ref, xs_ref.at[pl.ds(0, tm)], sem.at[0]).wait()
    pltpu.make_async_copy(h_ref, xs_ref.at[pl.ds(0, tm)], sem.at[1]).wait()


def _dispatch(h, pos1, pos2, fill_rows, n_rows, tm, tm_fill):
    t, d = h.shape
    return pl.pallas_call(
        functools.partial(_dispatch_kernel, tm=tm, n_fill=fill_rows.shape[0]),
        grid_spec=pltpu.PrefetchScalarGridSpec(
            num_scalar_prefetch=3,
            grid=(t // tm,),
            in_specs=[pl.BlockSpec((tm, d), lambda i, p1, p2, fr: (i, 0))],
            out_specs=pl.BlockSpec(memory_space=pl.ANY),
            scratch_shapes=[pltpu.VMEM((tm_fill, d), h.dtype), pltpu.SemaphoreType.DMA((3,))],
        ),
        out_shape=jax.ShapeDtypeStruct((n_rows, d), h.dtype),
        compiler_params=_params(("arbitrary",)),
        name="moe_dispatch",
    )(pos1, pos2, fill_rows, h)


def _moe_kernel(te_ref, slot_ref, nxt_ref, lo_ref, hi_ref, na_ref, x_ref, wg_hbm, wu_hbm, wd_hbm, o_ref,
                wg_buf, wu_buf, wd_buf, st_g, st_u, st_d, sem, *, layer):
    i = pl.program_id(0)
    rg = wg_buf.shape[1] // MOE_W_GROUPS
    rd = wd_buf.shape[1] // MOE_W_GROUPS

    def copies(e, c, s):
        r0g = pl.multiple_of(c * rg, rg)
        r0d = pl.multiple_of(c * rd, rd)
        return (pltpu.make_async_copy(wg_hbm.at[layer, e, pl.ds(r0g, rg)], st_g.at[s], sem.at[s, 0]),
                pltpu.make_async_copy(wu_hbm.at[layer, e, pl.ds(r0g, rg)], st_u.at[s], sem.at[s, 1]),
                pltpu.make_async_copy(wd_hbm.at[layer, e, pl.ds(r0d, rd)], st_d.at[s], sem.at[s, 2]))

    def start(e, c):
        for cp in copies(e, c, c % 2):
            cp.start()

    def finish(e, c, w_slot):
        s = c % 2
        for cp in copies(e, c, s):
            cp.wait()
        wg_buf[w_slot, pl.ds(pl.multiple_of(c * rg, rg), rg), :] = st_g[s].astype(BF16)
        wu_buf[w_slot, pl.ds(pl.multiple_of(c * rg, rg), rg), :] = st_u[s].astype(BF16)
        wd_buf[w_slot, pl.ds(pl.multiple_of(c * rd, rd), rd), :] = st_d[s].astype(BF16)

    def fetch(e, lo, hi, w_slot):
        def step(c, carry):
            @pl.when(c + 1 < hi)
            def _():
                start(e, c + 1)
            finish(e, c, w_slot)
            return carry
        lax.fori_loop(lo, hi, step, 0)

    @pl.when(i == 0)
    def _():
        start(te_ref[0], 0)
        fetch(te_ref[0], 0, MOE_W_GROUPS, slot_ref[0])

    @pl.when(i < na_ref[0])
    def _():
        lo, hi, nxt, w_slot = lo_ref[i], hi_ref[i], nxt_ref[i], slot_ref[i]

        def issue_first(_, carry):
            start(nxt, lo)
            return carry
        lax.fori_loop(0, (hi > lo).astype(jnp.int32), issue_first, 0)

        x = x_ref[...].astype(BF16)
        gate = jnp.dot(x, wg_buf[w_slot], preferred_element_type=F32)
        up = jnp.dot(x, wu_buf[w_slot], preferred_element_type=F32)
        hid = (jax.nn.silu(gate) * up).astype(BF16)
        o_ref[...] = jnp.dot(hid, wd_buf[w_slot], preferred_element_type=F32)
        fetch(nxt, lo, hi, 1 - w_slot)

    @pl.when(i >= na_ref[0])
    def _():
        o_ref[...] = jnp.zeros_like(o_ref)


def _moe_experts(xs, tables, n_active, w_gate, w_up, w_down, layer, tm):
    p, d = xs.shape
    f = w_gate.shape[3]
    row_blk = lambda i, te, sl, nx, lo, hi, na: (jnp.minimum(i, na[0] - 1), 0)
    hbm = pl.BlockSpec(memory_space=pl.ANY)
    return pl.pallas_call(
        functools.partial(_moe_kernel, layer=layer),
        grid_spec=pltpu.PrefetchScalarGridSpec(
            num_scalar_prefetch=6,
            grid=(p // tm,),
            in_specs=[pl.BlockSpec((tm, d), row_blk), hbm, hbm, hbm],
            out_specs=pl.BlockSpec((tm, d), lambda i, te, sl, nx, lo, hi, na: (i, 0)),
            scratch_shapes=[pltpu.VMEM((2, d, f), BF16), pltpu.VMEM((2, d, f), BF16), pltpu.VMEM((2, f, d), BF16),
                            pltpu.VMEM((2, d // MOE_W_GROUPS, f), F32), pltpu.VMEM((2, d // MOE_W_GROUPS, f), F32),
                            pltpu.VMEM((2, f // MOE_W_GROUPS, d), F32), pltpu.SemaphoreType.DMA((2, 3))],
        ),
        out_shape=jax.ShapeDtypeStruct((p, d), F32),
        compiler_params=_params(("arbitrary",)),
        name="moe_experts",
    )(*tables, n_active, xs, w_gate, w_up, w_down)


def _combine_ln_kernel(p1_ref, p2_ref, h_ref, g1_ref, g2_ref, lg_ref, lb_ref, ys_ref, o_ref,
                       buf1a, buf1b, buf2a, buf2b, sem, *, tm, n_tiles):
    i = pl.program_id(0)
    rows = COMBINE_ROWS_PER_TRIP
    bufs = ((buf1a, buf2a), (buf1b, buf2b))

    def gather(r0, slot):
        base = i * tm
        b1, b2 = bufs[slot]
        for r in range(rows):
            p1 = p1_ref[base + r0 + r]
            p2 = p2_ref[base + r0 + r]
            pltpu.make_async_copy(ys_ref.at[pl.ds(p1, 1)], b1.at[pl.ds(r0 + r, 1)], sem.at[slot, 0]).start()
            pltpu.make_async_copy(ys_ref.at[pl.ds(p2, 1)], b2.at[pl.ds(r0 + r, 1)], sem.at[slot, 1]).start()

    def finish(r0, slot):
        b1, b2 = bufs[slot]
        rs = pl.ds(r0, rows)
        ffn = g1_ref[rs, :] * b1[rs, :] + g2_ref[rs, :] * b2[rs, :]
        o_ref[rs, :] = _layer_norm(ALPHA * h_ref[rs, :] + ffn, lg_ref[...], lb_ref[...])

    def step(gather_slot, finish_slot):
        if finish_slot is not None:
            b1, b2 = bufs[finish_slot]
            pltpu.make_async_copy(ys_ref.at[pl.ds(0, tm)], b1, sem.at[finish_slot, 0]).wait()
            pltpu.make_async_copy(ys_ref.at[pl.ds(0, tm)], b2, sem.at[finish_slot, 1]).wait()

        def trip(k, carry):
            r0 = pl.multiple_of(k * rows, rows)
            if gather_slot is not None:
                gather(r0, gather_slot)
            if finish_slot is not None:
                finish(r0, finish_slot)
            return carry
        lax.fori_loop(0, tm // rows, trip, 0)

    middle = jnp.logical_and(i >= 1, i < n_tiles)
    pl.when(i == 0)(lambda: step(0, None))
    pl.when(jnp.logical_and(middle, i % 2 == 1))(lambda: step(1, 0))
    pl.when(jnp.logical_and(middle, i % 2 == 0))(lambda: step(0, 1))
    pl.when(i == n_tiles)(lambda: step(None, (n_tiles - 1) % 2))


def _combine_ln(h, ys, pos1, pos2, g1, g2, ln_g, ln_b, tm):
    t, d = h.shape
    n_tiles = t // tm
    row = lambda i, p1, p2: (jnp.maximum(i - 1, 0), 0)
    const = lambda i, p1, p2: (0, 0)
    return pl.pallas_call(
        functools.partial(_combine_ln_kernel, tm=tm, n_tiles=n_tiles),
        grid_spec=pltpu.PrefetchScalarGridSpec(
            num_scalar_prefetch=2,
            grid=(n_tiles + 1,),
            in_specs=[pl.BlockSpec((tm, d), row),
                      pl.BlockSpec((tm, 1), row),
                      pl.BlockSpec((tm, 1), row),
                      pl.BlockSpec((1, d), const),
                      pl.BlockSpec((1, d), const),
                      pl.BlockSpec(memory_space=pl.ANY)],
            out_specs=pl.BlockSpec((tm, d), row),
            scratch_shapes=[pltpu.VMEM((tm, d), F32)] * 4 + [pltpu.SemaphoreType.DMA((2, 2))],
        ),
        out_shape=jax.ShapeDtypeStruct((t, d), F32),
        compiler_params=_params(("arbitrary",)),
        name="moe_combine_ln",
    )(pos1, pos2, h, g1, g2, ln_g, ln_b, ys)


def _moe_block(h, e, gw, rk, counts, w_gate, w_up, w_down, layer, ln_g, ln_b, tm_tok):
    t, d = h.shape
    tm = MOE_TM
    n_rows = 2 * t + N_EXPERTS * tm
    n_tiles = n_rows // tm
    cnt = counts.astype(jnp.int32)
    padded = ((cnt + tm - 1) // tm) * tm
    ends = jnp.cumsum(padded)
    offs = ends - padded
    e_ids = jnp.arange(N_EXPERTS, dtype=jnp.int32)
    pos = jnp.sum(jnp.where(e[..., None] == e_ids, offs, 0), axis=-1) + rk
    n_active = jnp.maximum(ends[-1] // tm, 1)
    tile_start = jnp.minimum(jnp.arange(n_tiles, dtype=jnp.int32), n_active - 1) * tm
    tile_expert = jnp.minimum(jnp.sum((ends[None, :] <= tile_start[:, None]).astype(jnp.int32), axis=1),
                              N_EXPERTS - 1)
    last_tile = jnp.where(padded > 0, ends - tm, -1)
    tail = ends[-1] + jnp.arange(N_EXPERTS, dtype=jnp.int32) * tm
    fill_rows = jnp.concatenate([last_tile, jnp.where(tail < n_rows, tail, -1)]).astype(jnp.int32)
    xs = _dispatch(h, pos[0], pos[1], fill_rows, n_rows, tm_tok, tm)
    idx = jnp.arange(n_tiles, dtype=jnp.int32)
    is_first = jnp.concatenate([jnp.ones((1,), bool), tile_expert[1:] != tile_expert[:-1]])
    run_id = jnp.cumsum(is_first.astype(jnp.int32)) - 1
    run_start = lax.cummax(jnp.where(is_first, idx, 0))
    same_run = run_id[:, None] == run_id[None, :]
    run_len = jnp.sum((same_run & (idx[None, :] < n_active)).astype(jnp.int32), axis=1)
    nxt_tile = run_start + run_len
    nxt = jnp.sum(jnp.where(idx[None, :] == nxt_tile[:, None], tile_expert[None, :], 0), axis=1)
    has_next = nxt_tile < n_active
    k_in_run = idx - run_start
    lo = jnp.where(has_next, (MOE_W_GROUPS * k_in_run) // jnp.maximum(run_len, 1), 0)
    hi = jnp.where(has_next, (MOE_W_GROUPS * (k_in_run + 1)) // jnp.maximum(run_len, 1), 0)
    tables = [a.astype(jnp.int32) for a in (tile_expert, run_id % 2, jnp.where(has_next, nxt, 0), lo, hi)]
    ys = _moe_experts(xs, tables, n_active.reshape(1).astype(jnp.int32), w_gate, w_up, w_down, layer, tm)
    return _combine_ln(h, ys, pos[0], pos[1], gw[0].reshape(t, 1), gw[1].reshape(t, 1), ln_g, ln_b, tm_tok)


def kernel(x, ab_w_in, ab_lambda_re, ab_lambda_im, ab_log_dt, ab_b_re, ab_b_im, ab_c_re, ab_c_im, ab_d,
           ab_w_glu, ab_b_glu, ab_w_out, c_w_in, c_w_group, c_scale, c_w_out, ln_g, ln_b, router_w,
           router_b, moe_w_gate, moe_w_up, moe_w_down):
    bsz, seq, d = x.shape
    t = bsz * seq
    tm = min(ROW_TILE, t)
    assert t % tm == 0 and seq % tm == 0 and seq % S5_CHUNK == 0
    row = lambda a: a.reshape(1, -1)

    rw = jnp.zeros((d, LANES), F32).at[:, :N_EXPERTS].set(router_w)
    rw_hi = rw.astype(BF16)
    rw_lo = (rw - rw_hi.astype(F32)).astype(BF16)
    rb = jnp.zeros((1, LANES), F32).at[0, :N_EXPERTS].set(router_b)
    router = (jnp.concatenate([rw_hi, rw_lo], axis=1), rb)

    h = x.reshape(t, d)

    wg_b, wu_b, wd_b = moe_w_gate, moe_w_up, moe_w_down
    u, u_chunked, qkv = _in_proj_ab(h, ab_w_in[0].astype(BF16), min(2 * ROW_TILE, t))
    chunks = seq // S5_CHUNK
    nsteps = max(int(math.log2(chunks)), 0)
    assert (1 << nsteps) == chunks
    ops = _s5_operands(ab_lambda_re[0], ab_lambda_im[0], ab_log_dt[0], ab_b_re[0], ab_b_im[0],
                       ab_c_re[0], ab_c_im[0], S5_CHUNK, nsteps)
    y_ssm = _s5_scan(u_chunked, ops, bsz, seq)
    y_a = _s5_glu(y_ssm, u, ab_d[0], ab_w_glu[0], ab_b_glu[0], tm)
    y_b = _sb_attention(qkv.reshape(bsz, seq, 3 * SB_WIDTH), bsz, seq).reshape(t, SB_WIDTH)
    w_out = ab_w_out[0].astype(BF16)
    h, e, gw, rk, cnt = _proj_ln_route([y_a, y_b], [w_out[:S5_WIDTH], w_out[S5_WIDTH:]], h,
                                       row(ln_g[0, 0]), row(ln_b[0, 0]), router, tm)
    h = _moe_block(h, e, gw, rk, cnt, wg_b, wu_b, wd_b, 0, row(ln_g[0, 1]), row(ln_b[0, 1]), tm)

    pooled = _pool_in(h, c_w_in[0].astype(BF16), seq, tm)
    h, e, gw, rk, cnt = _proj_ln_route([pooled], [c_w_out[0].astype(BF16)], h,
                                       row(ln_g[1, 0]), row(ln_b[1, 0]), router, tm,
                                       grouped=(c_w_group[0].astype(BF16), row(c_scale[0])))
    h = _moe_block(h, e, gw, rk, cnt, wg_b, wu_b, wd_b, 1, row(ln_g[1, 1]), row(ln_b[1, 1]), tm)
    return h.reshape(bsz, seq, d)
```

```python
import functools
import math

import jax
import jax.numpy as jnp
from jax import lax
from jax.experimental import pallas as pl
from jax.experimental.pallas import tpu as pltpu

S5_WIDTH = 512
S5_GROUP = 16
S5_GROUPS = 32
S5_STATE = 64
SB_HEAD_DIM = 128
SB_HEADS = 12
SB_WIDTH = SB_HEADS * SB_HEAD_DIM
POOL_WINDOWS = (2, 4, 8, 16)
POOL_GROUP_WIDTH = 512
N_EXPERTS = 16
N_EXPERT_GROUPS = 4
EXPERTS_PER_GROUP = 4
DEPTH = 2
ALPHA = (2.0 * DEPTH) ** 0.25
LN_EPS = 1e-5

LANES = 128
ROW_SUB = 2048 // LANES
VMEM_LIMIT = 56 * 1024 * 1024

S5_CHUNK = 32
ATT_TQ = 128
ATT_HEADS_PER_STEP = 2
ATT_TILES_PER_ITER = 8
ATT_LOG_UNDERFLOW = -88.0
ATT_MASKED_SCORE = -1e30
MOE_TM = 256
MOE_W_GROUPS = 8
ROW_TILE = 512
PROJ_SLABS = 2
DMA_ISSUE_UNROLL = 8
COMBINE_ROWS_PER_TRIP = 64

BF16 = jnp.bfloat16
F32 = jnp.float32


def _params(sem, vmem=VMEM_LIMIT):
    return pltpu.CompilerParams(dimension_semantics=sem, vmem_limit_bytes=vmem)


def _piece_transpose(blocks):
    v = list(blocks)
    n = len(v)
    assert n * S5_GROUP == LANES
    piece = lax.broadcasted_iota(jnp.int32, v[0].shape, 1) // S5_GROUP
    d = n // 2
    while d >= 1:
        upper = (piece & d) != 0
        for q in range(n):
            if q & d == 0:
                a, b = v[q], v[q + d]
                v[q] = jnp.where(upper, pltpu.roll(b, d * S5_GROUP, 1), a)
                v[q + d] = jnp.where(upper, b, pltpu.roll(a, LANES - d * S5_GROUP, 1))
        d //= 2
    return v


def _in_proj_ab_kernel(x_ref, w_ref, u_ref, up_ref, qkv_ref, xb_ref, us_ref):
    j = pl.program_id(1)

    @pl.when(j == 0)
    def _():
        xb_ref[...] = x_ref[...].astype(BF16)

    acc = jnp.dot(xb_ref[...], w_ref[...], preferred_element_type=F32)

    @pl.when(j == 0)
    def _():
        u_ref[...] = acc
        chunk = S5_CHUNK
        n_chunks = acc.shape[0] // chunk
        per = LANES // S5_GROUP
        for g1 in range(S5_WIDTH // LANES):
            us_ref[g1] = acc[:, g1 * LANES:(g1 + 1) * LANES]
        for s1 in range(chunk // per):
            for g1 in range(S5_WIDTH // LANES):
                rows = [us_ref[g1, pl.ds(s1 * per + s2, n_chunks, stride=chunk), :] for s2 in range(per)]
                regrouped = _piece_transpose(rows)
                for g2 in range(per):
                    up_ref[g1 * per + g2, :, s1 * LANES:(s1 + 1) * LANES] = regrouped[g2].astype(up_ref.dtype)

    @pl.when(j > 0)
    def _():
        qkv_ref[...] = acc.astype(qkv_ref.dtype)


def _in_proj_ab(x, w, tm):
    m, k = x.shape
    tn = S5_WIDTH
    n_blocks = w.shape[1] // tn
    return pl.pallas_call(
        _in_proj_ab_kernel,
        grid=(m // tm, n_blocks),
        in_specs=[pl.BlockSpec((tm, k), lambda i, j: (i, 0)),
                  pl.BlockSpec((k, tn), lambda i, j: (0, j))],
        out_specs=[pl.BlockSpec((tm, tn), lambda i, j: (i, 0)),
                   pl.BlockSpec((S5_GROUPS, tm // S5_CHUNK, tn), lambda i, j: (0, i, 0)),
                   pl.BlockSpec((tm, tn), lambda i, j: (i, jnp.maximum(j - 1, 0)))],
        out_shape=[jax.ShapeDtypeStruct((m, tn), F32),
                   jax.ShapeDtypeStruct((S5_GROUPS, m // S5_CHUNK, tn), BF16),
                   jax.ShapeDtypeStruct((m, w.shape[1] - tn), BF16)],
        scratch_shapes=[pltpu.VMEM((tm, k), BF16), pltpu.VMEM((tn // LANES, tm, LANES), F32)],
        compiler_params=_params(("parallel", "arbitrary")),
        name="in_proj_ab",
    )(x, w)


def _bmm_f32_kernel(x_ref, y_ref, o_ref):
    o_ref[...] = jnp.dot(x_ref[...], y_ref[...], preferred_element_type=F32,
                         precision=lax.Precision.HIGHEST)


def _bmm_f32(x, y):
    g, m, k = x.shape
    n = y.shape[2]
    return pl.pallas_call(
        _bmm_f32_kernel,
        grid=(g,),
        in_specs=[pl.BlockSpec((None, m, k), lambda i: (i, 0, 0)),
                  pl.BlockSpec((None, k, n), lambda i: (i, 0, 0))],
        out_specs=pl.BlockSpec((None, m, n), lambda i: (i, 0, 0)),
        out_shape=jax.ShapeDtypeStruct((g, m, n), F32),
        compiler_params=_params(("parallel",)),
        name="s5_taps",
    )(x, y)


def _s5_kernel(u_ref, r0_ref, ws_ref, wo_ref, a1_ref, a2_ref, o_ref, wt_ref, *, chunks, nsteps):
    h = S5_GROUP
    r0 = r0_ref[...]
    lane = lax.broadcasted_iota(jnp.int32, r0.shape, 1)
    for j in range(r0.shape[1] // h):
        blk = r0 if j == 0 else jnp.where(lane >= j * h, pltpu.roll(r0, j * h, 1), 0.0)
        wt_ref[j * h:(j + 1) * h, :] = blk.astype(BF16)
    u = u_ref[...]
    y_intra = jnp.dot(u, wt_ref[...], preferred_element_type=F32)
    s = jnp.dot(u, ws_ref[...], preferred_element_type=F32)
    m = s.shape[0]
    c_idx = lax.broadcasted_iota(jnp.int32, s.shape, 0) % chunks
    half = s.shape[1] // 2
    for k in range(nsteps):
        d = 1 << k
        sh = jnp.where(c_idx >= d, pltpu.roll(s, d, 0), 0.0)
        s = s + a1_ref[k:k + 1, :] * sh + a2_ref[k:k + 1, :] * pltpu.roll(sh, half, 1)
    s_prev = jnp.where(c_idx >= 1, pltpu.roll(s, 1, 0), 0.0)
    y_inter = jnp.dot(s_prev.astype(BF16), wo_ref[...], preferred_element_type=F32)
    o_ref[...] = y_intra + y_inter


def _s5_operands(lam_re, lam_im, log_dt, b_re, b_im, c_re, c_im, chunk, nsteps):
    g, n = lam_re.shape
    h = S5_GROUP
    dt = jnp.exp(log_dt)[:, None]
    lr, li = lam_re, lam_im
    mag = jnp.exp(lr * dt)
    lbr = mag * jnp.cos(li * dt)
    lbi = mag * jnp.sin(li * dt)
    den = lr * lr + li * li
    coef_re = ((lbr - 1.0) * lr + lbi * li) / den
    coef_im = (lbi * lr - (lbr - 1.0) * li) / den
    bbar_re = coef_re[..., None] * b_re - coef_im[..., None] * b_im
    bbar_im = coef_re[..., None] * b_im + coef_im[..., None] * b_re

    def power(mult):
        mult = mult.astype(F32)[None, :, None]
        pm = jnp.exp(lr[:, None, :] * dt[:, None, :] * mult)
        ang = li[:, None, :] * dt[:, None, :] * mult
        return pm * jnp.cos(ang), pm * jnp.sin(ang)

    pr, pi = power(jnp.arange(chunk + 1))
    cre = jnp.swapaxes(c_re, 1, 2)
    cim = jnp.swapaxes(c_im, 1, 2)
    cb_re = cre[:, :, :, None] * bbar_re[:, :, None, :] - cim[:, :, :, None] * bbar_im[:, :, None, :]
    cb_im = cre[:, :, :, None] * bbar_im[:, :, None, :] + cim[:, :, :, None] * bbar_re[:, :, None, :]
    y_op = jnp.concatenate([cb_re, cb_im], axis=1).reshape(g, 2 * n, h * h)
    x_op = jnp.concatenate([pr[:, :chunk], -pi[:, :chunk]], axis=-1)
    taps = _bmm_f32(x_op, y_op).reshape(g, chunk, h, h)
    r0 = taps.transpose(0, 3, 1, 2).reshape(g, h, chunk * h)
    prr = pr[:, chunk - 1::-1][:, :chunk]
    pir = pi[:, chunk - 1::-1][:, :chunk]
    ws_re = prr[:, :, None, :] * jnp.swapaxes(bbar_re, 1, 2)[:, None] - pir[:, :, None, :] * jnp.swapaxes(bbar_im, 1, 2)[:, None]
    ws_im = prr[:, :, None, :] * jnp.swapaxes(bbar_im, 1, 2)[:, None] + pir[:, :, None, :] * jnp.swapaxes(bbar_re, 1, 2)[:, None]
    ws = jnp.concatenate([ws_re, ws_im], axis=-1).reshape(g, chunk * h, 2 * n)
    p1r = pr[:, 1:chunk + 1]
    p1i = pi[:, 1:chunk + 1]
    wo_re = c_re[:, None] * p1r[:, :, None, :] - c_im[:, None] * p1i[:, :, None, :]
    wo_im = -(c_re[:, None] * p1i[:, :, None, :] + c_im[:, None] * p1r[:, :, None, :])
    wo = jnp.concatenate([wo_re, wo_im], axis=-1)
    wo = wo.transpose(0, 3, 1, 2).reshape(g, 2 * n, chunk * h)
    sr, si = power(chunk * (2 ** jnp.arange(max(nsteps, 1))))
    a1 = jnp.concatenate([sr, sr], axis=-1)
    a2 = jnp.concatenate([-si, si], axis=-1)
    return r0, ws.astype(BF16), wo.astype(BF16), a1, a2


def _s5_scan(up, ops, bsz, seq):
    r0, ws, wo, a1, a2 = ops
    g, h, chunk = S5_GROUPS, S5_GROUP, S5_CHUNK
    chunks = seq // chunk
    m = bsz * chunks
    nsteps = a1.shape[1] if chunks > 1 else 0
    n2 = ws.shape[2]
    ns = a1.shape[1]
    return pl.pallas_call(
        functools.partial(_s5_kernel, chunks=chunks, nsteps=nsteps),
        grid=(g,),
        in_specs=[pl.BlockSpec((None, m, chunk * h), lambda i: (i, 0, 0)),
                  pl.BlockSpec((None, h, chunk * h), lambda i: (i, 0, 0)),
                  pl.BlockSpec((None, chunk * h, n2), lambda i: (i, 0, 0)),
                  pl.BlockSpec((None, n2, chunk * h), lambda i: (i, 0, 0)),
                  pl.BlockSpec((None, ns, n2), lambda i: (i, 0, 0)),
                  pl.BlockSpec((None, ns, n2), lambda i: (i, 0, 0))],
        out_specs=pl.BlockSpec((None, m, chunk * h), lambda i: (i, 0, 0)),
        out_shape=jax.ShapeDtypeStruct((g, m, chunk * h), F32),
        scratch_shapes=[pltpu.VMEM((chunk * h, chunk * h), BF16)],
        compiler_params=_params(("parallel",)),
        name="s5_scan",
    )(up, r0, ws, wo, a1, a2)


def _s5_glu_kernel(yp_ref, u_ref, d_ref, w_ref, b_ref, o_ref, y_ref):
    chunk = S5_CHUNK
    n_chunks = yp_ref.shape[1]
    per = LANES // S5_GROUP
    for s1 in range(chunk // per):
        for g1 in range(S5_WIDTH // LANES):
            by_step = _piece_transpose([yp_ref[g1 * per + g2, :, s1 * LANES:(s1 + 1) * LANES] for g2 in range(per)])
            for s2 in range(per):
                y_ref[g1, pl.ds(s1 * per + s2, n_chunks, stride=chunk), :] = by_step[s2]
    y_ssm = jnp.concatenate([y_ref[g1] for g1 in range(S5_WIDTH // LANES)], axis=1)
    y = y_ssm + d_ref[...] * u_ref[...]
    gl = jax.nn.gelu(y)
    gate = jnp.dot(gl.astype(BF16), w_ref[...], preferred_element_type=F32) + b_ref[...]
    o_ref[...] = (gl * jax.nn.sigmoid(gate)).astype(o_ref.dtype)


def _s5_glu(yp, u, d, w_glu, b_glu, tm):
    t, c = u.shape
    return pl.pallas_call(
        _s5_glu_kernel,
        grid=(t // tm,),
        in_specs=[pl.BlockSpec((S5_GROUPS, tm // S5_CHUNK, c), lambda i: (0, i, 0)),
                  pl.BlockSpec((tm, c), lambda i: (i, 0)),
                  pl.BlockSpec((1, c), lambda i: (0, 0)),
                  pl.BlockSpec((c, c), lambda i: (0, 0)),
                  pl.BlockSpec((1, c), lambda i: (0, 0))],
        out_specs=pl.BlockSpec((tm, c), lambda i: (i, 0)),
        out_shape=jax.ShapeDtypeStruct((t, c), BF16),
        scratch_shapes=[pltpu.VMEM((c // LANES, tm, LANES), F32)],
        compiler_params=_params(("parallel",)),
        name="s5_glu",
    )(yp, u, d.reshape(1, c), w_glu.astype(BF16), b_glu.reshape(1, c))


def _sb_steps(tiles, tri, scale):
    dh = SB_HEAD_DIM
    nk = tiles[0][1].shape[0]
    left = lax.broadcasted_iota(jnp.int32, (nk, 2 * dh), 1) < dh
    pre = []
    stack = []
    for q2, k2, v2, causal, _ in tiles:
        zero = jnp.zeros_like(k2)
        k_bd = jnp.concatenate([jnp.where(left, k2, zero), jnp.where(left, zero, k2)], axis=0)
        z2 = lax.dot_general(q2, k_bd, (((1,), (1,)), ((), ())), preferred_element_type=F32) * scale
        per_head = []
        for hh in range(2):
            z = z2[:, hh * nk:(hh + 1) * nk]
            if causal is not None:
                z = jnp.where(causal, z, ATT_MASKED_SCORE)
            sp = jnp.maximum(z, 0.0) + jnp.log(1.0 + jnp.exp(-jnp.abs(z)))
            log_keep = -sp
            hi = log_keep.astype(BF16)
            lo = (log_keep - hi.astype(F32)).astype(BF16)
            stack += [hi, lo]
            per_head.append((z - sp, log_keep))
        pre.append(per_head)
    tq = tiles[0][0].shape[0]
    sums = jnp.dot(jnp.concatenate(stack, axis=0), tri, preferred_element_type=F32)
    results = []
    for ti, (q2, k2, v2, causal, r_in) in enumerate(tiles):
        ws, rs = [], []
        for hh in range(2):
            log_beta, log_keep = pre[ti][hh]
            base = (ti * 2 + hh) * 2 * tq
            tail = sums[base:base + tq] + sums[base + tq:base + 2 * tq]
            if r_in is not None:
                tail = tail + r_in[hh]
            ws.append(jnp.exp(log_beta + tail).astype(BF16))
            rs.append(jnp.sum(log_keep, axis=1, keepdims=True))
        zero = jnp.zeros_like(v2)
        v_bd = jnp.concatenate([jnp.where(left, v2, zero), jnp.where(left, zero, v2)], axis=0)
        out = jnp.dot(jnp.concatenate(ws, axis=1), v_bd, preferred_element_type=F32)
        results.append((out, rs))
    return results


def _sb_attn_kernel(q_ref, k_ref, v_ref, o_ref, acc_ref, r_ref, tri_ref, *, seq, tq, scale):
    dh = SB_HEAD_DIM
    win = 2 * tq
    row = lax.broadcasted_iota(jnp.int32, (win, win), 0)
    col = lax.broadcasted_iota(jnp.int32, (win, win), 1)
    tri_ref[...] = jnp.where(row > col, 1.0, 0.0).astype(BF16)
    assert ATT_HEADS_PER_STEP == 2

    def q_tiles(p, carry):
        first, where = [], []
        for ti in range(ATT_TILES_PER_ITER):
            i = p * ATT_TILES_PER_ITER + ti
            q0 = pl.multiple_of(i * tq, tq)
            kblk = jnp.maximum(i - 1, 0)
            k0 = pl.multiple_of(kblk * tq, tq)
            q_pos = q0 + lax.broadcasted_iota(jnp.int32, (tq, win), 0)
            causal = (k0 + lax.broadcasted_iota(jnp.int32, (tq, win), 1)) < q_pos
            first.append((q_ref[pl.ds(q0, tq), :], k_ref[pl.ds(k0, win), :], v_ref[pl.ds(k0, win), :], causal, None))
            where.append((q0, kblk))
        rmax = []
        for ti, (out, rs) in enumerate(_sb_steps(first, tri_ref[...], scale)):
            acc_ref[ti] = out
            r_ref[2 * ti] = rs[0]
            r_ref[2 * ti + 1] = rs[1]
            rmax.append(jnp.maximum(jnp.max(rs[0]), jnp.max(rs[1])))
        for ti, (q0, kblk) in enumerate(where):
            def cond(c):
                j, rm = c
                return jnp.logical_and(j >= 0, rm > ATT_LOG_UNDERFLOW)

            def body(c, ti=ti, q0=q0):
                j, _ = c
                kj = pl.multiple_of(j * tq, tq)
                r_old = [r_ref[2 * ti], r_ref[2 * ti + 1]]
                tile = (q_ref[pl.ds(q0, tq), :], k_ref[pl.ds(kj, tq), :], v_ref[pl.ds(kj, tq), :], None, r_old)
                (out, rs), = _sb_steps([tile], tri_ref[:tq, :tq], scale)
                acc_ref[ti] += out
                r_new = [r_old[0] + rs[0], r_old[1] + rs[1]]
                r_ref[2 * ti] = r_new[0]
                r_ref[2 * ti + 1] = r_new[1]
                return j - 1, jnp.maximum(jnp.max(r_new[0]), jnp.max(r_new[1]))

            lax.while_loop(cond, body, (kblk - 1, rmax[ti]))
            o_ref[pl.ds(q0, tq), :] = acc_ref[ti].astype(o_ref.dtype)
        return carry

    lax.fori_loop(0, seq // (tq * ATT_TILES_PER_ITER), q_tiles, 0)


def _sb_attention(qkv, bsz, seq):
    tq = ATT_TQ
    assert seq % (tq * ATT_TILES_PER_ITER) == 0 and seq >= 2 * tq
    wcols = ATT_HEADS_PER_STEP * SB_HEAD_DIM
    nhp = SB_WIDTH // wcols
    kern = functools.partial(_sb_attn_kernel, seq=seq, tq=tq, scale=1.0 / math.sqrt(SB_HEAD_DIM))
    return pl.pallas_call(
        kern,
        grid=(bsz, nhp),
        in_specs=[pl.BlockSpec((None, seq, wcols), lambda b, h: (b, 0, h)),
                  pl.BlockSpec((None, seq, wcols), lambda b, h: (b, 0, nhp + h)),
                  pl.BlockSpec((None, seq, wcols), lambda b, h: (b, 0, 2 * nhp + h))],
        out_specs=pl.BlockSpec((None, seq, wcols), lambda b, h: (b, 0, h)),
        out_shape=jax.ShapeDtypeStruct((bsz, seq, SB_WIDTH), BF16),
        scratch_shapes=[pltpu.VMEM((ATT_TILES_PER_ITER, tq, wcols), F32),
                        pltpu.VMEM((ATT_HEADS_PER_STEP * ATT_TILES_PER_ITER, tq, 1), F32),
                        pltpu.VMEM((2 * tq, 2 * tq), BF16)],
        compiler_params=_params(("parallel", "parallel")),
        name="sb_attention",
    )(qkv, qkv, qkv)


def _pool_in_kernel(x_ref, w_ref, o_ref, halo_ref, *, seq, tm):
    i = pl.program_id(0)
    hmax = POOL_WINDOWS[-1]
    acc = jnp.dot(x_ref[...].astype(BF16), w_ref[...], preferred_element_type=F32)
    t0 = (i * tm) % seq

    @pl.when(t0 == 0)
    def _():
        halo_ref[...] = jnp.zeros_like(halo_ref)

    ext = jnp.concatenate([halo_ref[...], acc], axis=0)
    t_idx = t0 + lax.broadcasted_iota(jnp.int32, (tm, POOL_GROUP_WIDTH), 0)
    for gi, win in enumerate(POOL_WINDOWS):
        cs = slice(gi * POOL_GROUP_WIDTH, (gi + 1) * POOL_GROUP_WIDTH)
        e = ext[:, cs]
        span = 1
        while span < win:
            e = e + pltpu.roll(e, span, 0)
            span *= 2
        wsum = e[hmax:, :]
        cnt = jnp.minimum(t_idx + 1, win).astype(F32)
        o_ref[:, cs] = (wsum / cnt - acc[:, cs]).astype(o_ref.dtype)
    halo_ref[...] = acc[tm - hmax:, :]


def _pool_in(x, w, seq, tm):
    t, k = x.shape
    n = w.shape[1]
    return pl.pallas_call(
        functools.partial(_pool_in_kernel, seq=seq, tm=tm),
        grid=(t // tm,),
        in_specs=[pl.BlockSpec((tm, k), lambda i: (i, 0)),
                  pl.BlockSpec((k, n), lambda i: (0, 0))],
        out_specs=pl.BlockSpec((tm, n), lambda i: (i, 0)),
        out_shape=jax.ShapeDtypeStruct((t, n), BF16),
        scratch_shapes=[pltpu.VMEM((POOL_WINDOWS[-1], n), F32)],
        compiler_params=_params(("arbitrary",)),
        name="pool_in",
    )(x, w)


def _layer_norm(x, g, b):
    mu = jnp.mean(x, axis=-1, keepdims=True)
    xc = x - mu
    var = jnp.mean(xc * xc, axis=-1, keepdims=True)
    return xc * lax.rsqrt(var + LN_EPS) * g + b


def _route(h, rwh_ref, rb_ref, base_ref, tm):
    h_hi = h.astype(BF16)
    h_lo = (h - h_hi.astype(F32)).astype(BF16)
    both = jnp.dot(h_hi, rwh_ref[...], preferred_element_type=F32)
    logits = (both[:, :LANES] + both[:, LANES:]
              + jnp.dot(h_lo, rwh_ref[:, :LANES], preferred_element_type=F32)) + rb_ref[...]
    lt = logits.T[:N_EXPERTS, :]
    mx = jnp.max(lt, axis=0, keepdims=True)
    ex = jnp.exp(lt - mx)
    probs = ex / jnp.sum(ex, axis=0, keepdims=True)
    p = [probs[e:e + 1, :] for e in range(N_EXPERTS)]

    def first_max(vals):
        best, idx = vals[0], jnp.zeros_like(vals[0], dtype=jnp.int32)
        for n in range(1, len(vals)):
            upd = vals[n] > best
            idx = jnp.where(upd, n, idx)
            best = jnp.where(upd, vals[n], best)
        return best, idx

    scores = []
    for gi in range(N_EXPERT_GROUPS):
        p0, p1, p2, p3 = p[4 * gi:4 * gi + 4]
        a, b = jnp.maximum(p0, p1), jnp.minimum(p0, p1)
        c, d = jnp.maximum(p2, p3), jnp.minimum(p2, p3)
        scores.append(jnp.maximum(a, c) + jnp.maximum(jnp.minimum(a, c), jnp.maximum(b, d)))
    _, g_sel = first_max(scores)
    sel = []
    for n in range(EXPERTS_PER_GROUP):
        v = p[n]
        for gi in range(1, N_EXPERT_GROUPS):
            v = jnp.where(g_sel == gi, p[4 * gi + n], v)
        sel.append(v)
    v1, i1 = first_max(sel)
    v2, i2 = first_max([jnp.where(i1 == n, -1.0, sel[n]) for n in range(EXPERTS_PER_GROUP)])
    den = v1 + v2
    e1 = g_sel * EXPERTS_PER_GROUP + i1
    e2 = g_sel * EXPERTS_PER_GROUP + i2
    e_iota = lax.broadcasted_iota(jnp.int32, (N_EXPERTS, tm), 0)
    oh1 = e_iota == e1
    oh2 = e_iota == e2
    oh = jnp.where(jnp.logical_or(oh1, oh2), 1.0, 0.0)
    r_i = lax.broadcasted_iota(jnp.int32, (tm, tm), 0)
    c_i = lax.broadcasted_iota(jnp.int32, (tm, tm), 1)
    before = jnp.where(r_i < c_i, 1.0, 0.0).astype(BF16)
    cnt = jnp.dot(oh.astype(BF16), before, preferred_element_type=F32) + base_ref[...]
    rank1 = jnp.sum(jnp.where(oh1, cnt, 0.0), axis=0, keepdims=True)
    rank2 = jnp.sum(jnp.where(oh2, cnt, 0.0), axis=0, keepdims=True)
    base_ref[...] += jnp.sum(oh, axis=1, keepdims=True)
    e_out = jnp.concatenate([e1, e2], axis=0)
    w_out = jnp.concatenate([v1 / den, v2 / den], axis=0)
    r_out = jnp.concatenate([rank1, rank2], axis=0).astype(jnp.int32)
    return e_out, w_out, r_out


def _proj_ln_kernel(*refs, n_x, grouped, tm, n_tiles):
    xs = refs[:n_x]
    ws = refs[n_x:2 * n_x]
    pos = 2 * n_x
    if grouped:
        wg_ref, sc_ref = refs[pos:pos + 2]
        pos += 2
    res_ref, g_ref, b_ref, rwh_ref, rb_ref = refs[pos:pos + 5]
    h_ref, hrow_ref, e_ref, gw_ref, rk_ref, cnt_ref, base_ref, mix_a, mix_b = refs[pos + 5:pos + 14]
    i = pl.program_id(0)
    mix_bufs = (mix_a, mix_b)
    sm = tm // PROJ_SLABS

    @pl.when(i == 0)
    def _():
        base_ref[...] = jnp.zeros_like(base_ref)

    def project(slot):
        for sl in range(PROJ_SLABS):
            rs = slice(sl * sm, (sl + 1) * sm)
            if grouped:
                x = xs[0][rs, :]
                parts = []
                for gi in range(len(POOL_WINDOWS)):
                    cs = slice(gi * POOL_GROUP_WIDTH, (gi + 1) * POOL_GROUP_WIDTH)
                    parts.append(jnp.dot(x[:, cs], wg_ref[gi], preferred_element_type=F32))
                y = (jnp.concatenate(parts, axis=1) * sc_ref[...]).astype(BF16)
                mix = jnp.dot(y, ws[0][...], preferred_element_type=F32)
            else:
                mix = jnp.dot(xs[0][rs, :], ws[0][...], preferred_element_type=F32)
                for n in range(1, n_x):
                    mix = mix + jnp.dot(xs[n][rs, :], ws[n][...], preferred_element_type=F32)
            mix_bufs[slot][rs, :] = mix

    def normalise_and_route(slot):
        for sl in range(PROJ_SLABS):
            rs = slice(sl * sm, (sl + 1) * sm)
            h = _layer_norm(ALPHA * res_ref[rs, :] + mix_bufs[slot][rs, :], g_ref[...], b_ref[...])
            h_ref[rs, :] = h
            sub = h.shape[1] // LANES
            for s in range(sub):
                hrow_ref[pl.ds(sl * sm * sub + s, sm, stride=sub), :] = h[:, s * LANES:(s + 1) * LANES]
            e_out, w_out, r_out = _route(h, rwh_ref, rb_ref, base_ref, sm)
            e_ref[:, rs] = e_out
            gw_ref[:, rs] = w_out
            rk_ref[:, rs] = r_out
        cnt_ref[...] = jnp.broadcast_to(base_ref[...], cnt_ref.shape)

    def step(project_slot, finish_slot):
        if project_slot is not None:
            project(project_slot)
        if finish_slot is not None:
            normalise_and_route(finish_slot)

    middle = jnp.logical_and(i >= 1, i < n_tiles)
    pl.when(i == 0)(lambda: step(0, None))
    pl.when(jnp.logical_and(middle, i % 2 == 1))(lambda: step(1, 0))
    pl.when(jnp.logical_and(middle, i % 2 == 0))(lambda: step(0, 1))
    pl.when(i == n_tiles)(lambda: step(None, (n_tiles - 1) % 2))


def _proj_ln_route(xs, ws, res, ln_g, ln_b, router, tm, grouped=None):
    t, d = res.shape
    rwh, rb = router
    n_x = len(xs)
    n_tiles = t // tm
    full = lambda a: pl.BlockSpec(a.shape, lambda i, nd=a.ndim: (0,) * nd, pipeline_mode=pl.Buffered(1))
    cur = lambda i: (jnp.minimum(i, n_tiles - 1), 0)
    prv = lambda i: (jnp.maximum(i - 1, 0), 0)
    prv_t = lambda i: (0, jnp.maximum(i - 1, 0))
    in_specs = [pl.BlockSpec((tm, x.shape[1]), cur) for x in xs]
    in_specs += [full(w) for w in ws]
    args = list(xs) + list(ws)
    if grouped is not None:
        in_specs += [full(a) for a in grouped]
        args += list(grouped)
    in_specs += [pl.BlockSpec((tm, d), prv), full(ln_g), full(ln_b), full(rwh), full(rb)]
    args += [res, ln_g, ln_b, rwh, rb]
    pair = lambda dt: jax.ShapeDtypeStruct((2, t), dt)
    out = pl.pallas_call(
        functools.partial(_proj_ln_kernel, n_x=n_x, grouped=grouped is not None, tm=tm, n_tiles=n_tiles),
        grid=(n_tiles + 1,),
        in_specs=in_specs,
        out_specs=[pl.BlockSpec((tm, d), prv),
                   pl.BlockSpec((tm * (d // LANES), LANES), prv),
                   pl.BlockSpec((2, tm), prv_t),
                   pl.BlockSpec((2, tm), prv_t),
                   pl.BlockSpec((2, tm), prv_t),
                   pl.BlockSpec((N_EXPERTS, LANES), lambda i: (0, 0))],
        out_shape=[jax.ShapeDtypeStruct((t, d), F32), jax.ShapeDtypeStruct((t * (d // LANES), LANES), F32),
                   pair(jnp.int32), pair(F32), pair(jnp.int32),
                   jax.ShapeDtypeStruct((N_EXPERTS, LANES), F32)],
        scratch_shapes=[pltpu.VMEM((N_EXPERTS, 1), F32), pltpu.VMEM((tm, d), F32), pltpu.VMEM((tm, d), F32)],
        compiler_params=_params(("arbitrary",)),
        name="proj_ln_route",
    )(*args)
    h, h_rows, e, gw, rk, cnt = out
    return h, h_rows, e, gw, rk, cnt[:, 0]


def _dispatch_kernel(p1_ref, p2_ref, fill_ref, h_ref, xs_ref, zero_ref, sem, *, tm, n_fill):
    base = pl.program_id(0) * tm

    @pl.when(pl.program_id(0) == 0)
    def _():
        zero_ref[...] = jnp.zeros_like(zero_ref)
        n_zero = zero_ref.shape[0]
        fill = lambda k: pltpu.make_async_copy(
            zero_ref, xs_ref.at[pl.ds(pl.multiple_of(jnp.maximum(fill_ref[k], 0) * ROW_SUB, n_zero), n_zero)],
            sem.at[2])
        for k in range(n_fill):
            pl.when(fill_ref[k] >= 0)(lambda k=k: fill(k).start())
        for k in range(n_fill):
            pl.when(fill_ref[k] >= 0)(lambda k=k: fill(k).wait())

    def sub_rows(row):
        return pl.ds(pl.multiple_of(row * ROW_SUB, ROW_SUB), ROW_SUB)

    def issue(r, carry):
        src = h_ref.at[sub_rows(r)]
        pltpu.make_async_copy(src, xs_ref.at[sub_rows(p1_ref[base + r])], sem.at[0]).start()
        pltpu.make_async_copy(src, xs_ref.at[sub_rows(p2_ref[base + r])], sem.at[1]).start()
        return carry

    lax.fori_loop(0, tm, issue, 0, unroll=DMA_ISSUE_UNROLL)
    pltpu.make_async_copy(h_ref, xs_ref.at[pl.ds(0, tm * ROW_SUB)], sem.at[0]).wait()
    pltpu.make_async_copy(h_ref, xs_ref.at[pl.ds(0, tm * ROW_SUB)], sem.at[1]).wait()


def _dispatch(h_rows, pos1, pos2, fill_rows, n_rows, tm, tm_fill):
    t = h_rows.shape[0] // ROW_SUB
    return pl.pallas_call(
        functools.partial(_dispatch_kernel, tm=tm, n_fill=fill_rows.shape[0]),
        grid_spec=pltpu.PrefetchScalarGridSpec(
            num_scalar_prefetch=3,
            grid=(t // tm,),
            in_specs=[pl.BlockSpec((tm * ROW_SUB, LANES), lambda i, p1, p2, fr: (i, 0))],
            out_specs=pl.BlockSpec(memory_space=pl.ANY),
            scratch_shapes=[pltpu.VMEM((tm_fill * ROW_SUB, LANES), h_rows.dtype), pltpu.SemaphoreType.DMA((3,))],
        ),
        out_shape=jax.ShapeDtypeStruct((n_rows * ROW_SUB, LANES), h_rows.dtype),
        compiler_params=_params(("arbitrary",)),
        name="moe_dispatch",
    )(pos1, pos2, fill_rows, h_rows)


def _moe_kernel(te_ref, slot_ref, nxt_ref, lo_ref, hi_ref, na_ref, x_ref, wg_hbm, wu_hbm, wd_hbm, o_ref,
                wg_buf, wu_buf, wd_buf, st_g, st_u, st_d, sem, *, layer):
    i = pl.program_id(0)
    rg = wg_buf.shape[1] // MOE_W_GROUPS
    rd = wd_buf.shape[1] // MOE_W_GROUPS

    def copies(e, c, s):
        r0g = pl.multiple_of(c * rg, rg)
        r0d = pl.multiple_of(c * rd, rd)
        return (pltpu.make_async_copy(wg_hbm.at[layer, e, pl.ds(r0g, rg)], st_g.at[s], sem.at[s, 0]),
                pltpu.make_async_copy(wu_hbm.at[layer, e, pl.ds(r0g, rg)], st_u.at[s], sem.at[s, 1]),
                pltpu.make_async_copy(wd_hbm.at[layer, e, pl.ds(r0d, rd)], st_d.at[s], sem.at[s, 2]))

    def start(e, c):
        for cp in copies(e, c, c % 2):
            cp.start()

    def finish(e, c, w_slot):
        s = c % 2
        for cp in copies(e, c, s):
            cp.wait()
        wg_buf[w_slot, pl.ds(pl.multiple_of(c * rg, rg), rg), :] = st_g[s].astype(BF16)
        wu_buf[w_slot, pl.ds(pl.multiple_of(c * rg, rg), rg), :] = st_u[s].astype(BF16)
        wd_buf[w_slot, pl.ds(pl.multiple_of(c * rd, rd), rd), :] = st_d[s].astype(BF16)

    def fetch(e, lo, hi, w_slot):
        def step(c, carry):
            @pl.when(c + 1 < hi)
            def _():
                start(e, c + 1)
            finish(e, c, w_slot)
            return carry
        lax.fori_loop(lo, hi, step, 0)

    @pl.when(i == 0)
    def _():
        start(te_ref[0], 0)
        fetch(te_ref[0], 0, MOE_W_GROUPS, slot_ref[0])

    @pl.when(i < na_ref[0])
    def _():
        lo, hi, nxt, w_slot = lo_ref[i], hi_ref[i], nxt_ref[i], slot_ref[i]

        def issue_first(_, carry):
            start(nxt, lo)
            return carry
        lax.fori_loop(0, (hi > lo).astype(jnp.int32), issue_first, 0)

        tm = x_ref.shape[0] // ROW_SUB
        x = jnp.concatenate([x_ref[pl.ds(s, tm, stride=ROW_SUB), :] for s in range(ROW_SUB)], axis=1).astype(BF16)
        gate = jnp.dot(x, wg_buf[w_slot], preferred_element_type=F32)
        up = jnp.dot(x, wu_buf[w_slot], preferred_element_type=F32)
        hid = (jax.nn.silu(gate) * up).astype(BF16)
        o_ref[...] = jnp.dot(hid, wd_buf[w_slot], preferred_element_type=F32)
        fetch(nxt, lo, hi, 1 - w_slot)

    @pl.when(i >= na_ref[0])
    def _():
        o_ref[...] = jnp.zeros_like(o_ref)


def _moe_experts(xs, tables, n_active, w_gate, w_up, w_down, layer, tm):
    p = xs.shape[0] // ROW_SUB
    d, f = w_gate.shape[2], w_gate.shape[3]
    row_blk = lambda i, te, sl, nx, lo, hi, na: (jnp.minimum(i, na[0] - 1), 0)
    hbm = pl.BlockSpec(memory_space=pl.ANY)
    return pl.pallas_call(
        functools.partial(_moe_kernel, layer=layer),
        grid_spec=pltpu.PrefetchScalarGridSpec(
            num_scalar_prefetch=6,
            grid=(p // tm,),
            in_specs=[pl.BlockSpec((tm * ROW_SUB, LANES), row_blk), hbm, hbm, hbm],
            out_specs=pl.BlockSpec((tm, d), lambda i, te, sl, nx, lo, hi, na: (i, 0)),
            scratch_shapes=[pltpu.VMEM((2, d, f), BF16), pltpu.VMEM((2, d, f), BF16), pltpu.VMEM((2, f, d), BF16),
                            pltpu.VMEM((2, d // MOE_W_GROUPS, f), F32), pltpu.VMEM((2, d // MOE_W_GROUPS, f), F32),
                            pltpu.VMEM((2, f // MOE_W_GROUPS, d), F32), pltpu.SemaphoreType.DMA((2, 3))],
        ),
        out_shape=jax.ShapeDtypeStruct((p, d), F32),
        compiler_params=_params(("arbitrary",)),
        name="moe_experts",
    )(*tables, n_active, xs, w_gate, w_up, w_down)


def _combine_ln_kernel(p1_ref, p2_ref, h_ref, g1_ref, g2_ref, lg_ref, lb_ref, ys_ref, o_ref,
                       buf1a, buf1b, buf2a, buf2b, sem, *, tm, n_tiles):
    i = pl.program_id(0)
    rows = COMBINE_ROWS_PER_TRIP
    bufs = ((buf1a, buf2a), (buf1b, buf2b))

    def gather(r0, slot):
        base = i * tm
        b1, b2 = bufs[slot]
        for r in range(rows):
            p1 = p1_ref[base + r0 + r]
            p2 = p2_ref[base + r0 + r]
            pltpu.make_async_copy(ys_ref.at[pl.ds(p1, 1)], b1.at[pl.ds(r0 + r, 1)], sem.at[slot, 0]).start()
            pltpu.make_async_copy(ys_ref.at[pl.ds(p2, 1)], b2.at[pl.ds(r0 + r, 1)], sem.at[slot, 1]).start()

    def finish(r0, slot):
        b1, b2 = bufs[slot]
        rs = pl.ds(r0, rows)
        ffn = g1_ref[rs, :] * b1[rs, :] + g2_ref[rs, :] * b2[rs, :]
        o_ref[rs, :] = _layer_norm(ALPHA * h_ref[rs, :] + ffn, lg_ref[...], lb_ref[...])

    def step(gather_slot, finish_slot):
        if finish_slot is not None:
            b1, b2 = bufs[finish_slot]
            pltpu.make_async_copy(ys_ref.at[pl.ds(0, tm)], b1, sem.at[finish_slot, 0]).wait()
            pltpu.make_async_copy(ys_ref.at[pl.ds(0, tm)], b2, sem.at[finish_slot, 1]).wait()

        def trip(k, carry):
            r0 = pl.multiple_of(k * rows, rows)
            if gather_slot is not None:
                gather(r0, gather_slot)
            if finish_slot is not None:
                finish(r0, finish_slot)
            return carry
        lax.fori_loop(0, tm // rows, trip, 0)

    middle = jnp.logical_and(i >= 1, i < n_tiles)
    pl.when(i == 0)(lambda: step(0, None))
    pl.when(jnp.logical_and(middle, i % 2 == 1))(lambda: step(1, 0))
    pl.when(jnp.logical_and(middle, i % 2 == 0))(lambda: step(0, 1))
    pl.when(i == n_tiles)(lambda: step(None, (n_tiles - 1) % 2))


def _combine_ln(h, ys, pos1, pos2, g1, g2, ln_g, ln_b, tm):
    t, d = h.shape
    n_tiles = t // tm
    row = lambda i, p1, p2: (jnp.maximum(i - 1, 0), 0)
    const = lambda i, p1, p2: (0, 0)
    return pl.pallas_call(
        functools.partial(_combine_ln_kernel, tm=tm, n_tiles=n_tiles),
        grid_spec=pltpu.PrefetchScalarGridSpec(
            num_scalar_prefetch=2,
            grid=(n_tiles + 1,),
            in_specs=[pl.BlockSpec((tm, d), row),
                      pl.BlockSpec((tm, 1), row),
                      pl.BlockSpec((tm, 1), row),
                      pl.BlockSpec((1, d), const),
                      pl.BlockSpec((1, d), const),
                      pl.BlockSpec(memory_space=pl.ANY)],
            out_specs=pl.BlockSpec((tm, d), row),
            scratch_shapes=[pltpu.VMEM((tm, d), F32)] * 4 + [pltpu.SemaphoreType.DMA((2, 2))],
        ),
        out_shape=jax.ShapeDtypeStruct((t, d), F32),
        compiler_params=_params(("arbitrary",)),
        name="moe_combine_ln",
    )(pos1, pos2, h, g1, g2, ln_g, ln_b, ys)


def _moe_block(h, h_rows, e, gw, rk, counts, w_gate, w_up, w_down, layer, ln_g, ln_b, tm_tok):
    t, d = h.shape
    tm = MOE_TM
    n_rows = 2 * t + N_EXPERTS * tm
    n_tiles = n_rows // tm
    cnt = counts.astype(jnp.int32)
    padded = ((cnt + tm - 1) // tm) * tm
    ends = jnp.cumsum(padded)
    offs = ends - padded
    e_ids = jnp.arange(N_EXPERTS, dtype=jnp.int32)
    pos = jnp.sum(jnp.where(e[..., None] == e_ids, offs, 0), axis=-1) + rk
    n_active = jnp.maximum(ends[-1] // tm, 1)
    tile_start = jnp.minimum(jnp.arange(n_tiles, dtype=jnp.int32), n_active - 1) * tm
    tile_expert = jnp.minimum(jnp.sum((ends[None, :] <= tile_start[:, None]).astype(jnp.int32), axis=1),
                              N_EXPERTS - 1)
    last_tile = jnp.where(padded > 0, ends - tm, -1)
    tail = ends[-1] + jnp.arange(N_EXPERTS, dtype=jnp.int32) * tm
    fill_rows = jnp.concatenate([last_tile, jnp.where(tail < n_rows, tail, -1)]).astype(jnp.int32)
    xs = _dispatch(h_rows, pos[0], pos[1], fill_rows, n_rows, tm_tok, tm)
    idx = jnp.arange(n_tiles, dtype=jnp.int32)
    is_first = jnp.concatenate([jnp.ones((1,), bool), tile_expert[1:] != tile_expert[:-1]])
    run_id = jnp.cumsum(is_first.astype(jnp.int32)) - 1
    run_start = lax.cummax(jnp.where(is_first, idx, 0))
    same_run = run_id[:, None] == run_id[None, :]
    run_len = jnp.sum((same_run & (idx[None, :] < n_active)).astype(jnp.int32), axis=1)
    nxt_tile = run_start + run_len
    nxt = jnp.sum(jnp.where(idx[None, :] == nxt_tile[:, None], tile_expert[None, :], 0), axis=1)
    has_next = nxt_tile < n_active
    k_in_run = idx - run_start
    lo = jnp.where(has_next, (MOE_W_GROUPS * k_in_run) // jnp.maximum(run_len, 1), 0)
    hi = jnp.where(has_next, (MOE_W_GROUPS * (k_in_run + 1)) // jnp.maximum(run_len, 1), 0)
    tables = [a.astype(jnp.int32) for a in (tile_expert, run_id % 2, jnp.where(has_next, nxt, 0), lo, hi)]
    ys = _moe_experts(xs, tables, n_active.reshape(1).astype(jnp.int32), w_gate, w_up, w_down, layer, tm)
    return _combine_ln(h, ys, pos[0], pos[1], gw[0].reshape(t, 1), gw[1].reshape(t, 1), ln_g, ln_b, tm_tok)


def kernel(x, ab_w_in, ab_lambda_re, ab_lambda_im, ab_log_dt, ab_b_re, ab_b_im, ab_c_re, ab_c_im, ab_d,
           ab_w_glu, ab_b_glu, ab_w_out, c_w_in, c_w_group, c_scale, c_w_out, ln_g, ln_b, router_w,
           router_b, moe_w_gate, moe_w_up, moe_w_down):
    bsz, seq, d = x.shape
    t = bsz * seq
    tm = min(ROW_TILE, t)
    assert t % tm == 0 and seq % tm == 0 and seq % S5_CHUNK == 0
    row = lambda a: a.reshape(1, -1)

    rw = jnp.zeros((d, LANES), F32).at[:, :N_EXPERTS].set(router_w)
    rw_hi = rw.astype(BF16)
    rw_lo = (rw - rw_hi.astype(F32)).astype(BF16)
    rb = jnp.zeros((1, LANES), F32).at[0, :N_EXPERTS].set(router_b)
    router = (jnp.concatenate([rw_hi, rw_lo], axis=1), rb)

    h = x.reshape(t, d)

    wg_b, wu_b, wd_b = moe_w_gate, moe_w_up, moe_w_down
    u, u_chunked, qkv = _in_proj_ab(h, ab_w_in[0].astype(BF16), min(2 * ROW_TILE, t))
    chunks = seq // S5_CHUNK
    nsteps = max(int(math.log2(chunks)), 0)
    assert (1 << nsteps) == chunks
    ops = _s5_operands(ab_lambda_re[0], ab_lambda_im[0], ab_log_dt[0], ab_b_re[0], ab_b_im[0],
                       ab_c_re[0], ab_c_im[0], S5_CHUNK, nsteps)
    y_ssm = _s5_scan(u_chunked, ops, bsz, seq)
    y_a = _s5_glu(y_ssm, u, ab_d[0], ab_w_glu[0], ab_b_glu[0], tm)
    y_b = _sb_attention(qkv.reshape(bsz, seq, 3 * SB_WIDTH), bsz, seq).reshape(t, SB_WIDTH)
    w_out = ab_w_out[0].astype(BF16)
    h, h_rows, e, gw, rk, cnt = _proj_ln_route([y_a, y_b], [w_out[:S5_WIDTH], w_out[S5_WIDTH:]], h,
                                       row(ln_g[0, 0]), row(ln_b[0, 0]), router, tm)
    h = _moe_block(h, h_rows, e, gw, rk, cnt, wg_b, wu_b, wd_b,0, row(ln_g[0, 1]), row(ln_b[0, 1]), tm)

    pooled = _pool_in(h, c_w_in[0].astype(BF16), seq, tm)
    h, h_rows, e, gw, rk, cnt = _proj_ln_route([pooled], [c_w_out[0].astype(BF16)], h,
                                       row(ln_g[1, 0]), row(ln_b[1, 0]), router, tm,
                                       grouped=(c_w_group[0].astype(BF16), row(c_scale[0])))
    h = _moe_block(h, h_rows, e, gw, rk, cnt, wg_b, wu_b, wd_b,1, row(ln_g[1, 1]), row(ln_b[1, 1]), tm)
    return h.reshape(bsz, seq, d)
```

```python
import functools
import math

import jax
import jax.numpy as jnp
from jax import lax
from jax.experimental import pallas as pl
from jax.experimental.pallas import tpu as pltpu

S5_WIDTH = 512
S5_GROUP = 16
S5_GROUPS = 32
S5_STATE = 64
SB_HEAD_DIM = 128
SB_HEADS = 12
SB_WIDTH = SB_HEADS * SB_HEAD_DIM
POOL_WINDOWS = (2, 4, 8, 16)
POOL_GROUP_WIDTH = 512
N_EXPERTS = 16
N_EXPERT_GROUPS = 4
EXPERTS_PER_GROUP = 4
DEPTH = 2
ALPHA = (2.0 * DEPTH) ** 0.25
LN_EPS = 1e-5

LANES = 128
VMEM_LIMIT = 56 * 1024 * 1024

S5_CHUNK = 32
ATT_TQ = 128
ATT_HEADS_PER_STEP = 2
ATT_TILES_PER_ITER = 8
ATT_LOG_UNDERFLOW = -88.0
ATT_MASKED_SCORE = -1e30
MOE_TM = 256
MOE_W_GROUPS = 8
ROW_TILE = 512
PROJ_SLABS = 2
DMA_ISSUE_UNROLL = 8
COMBINE_ROWS_PER_TRIP = 64
SECOND_STREAM_PRIORITY = 1

BF16 = jnp.bfloat16
F32 = jnp.float32


def _params(sem, vmem=VMEM_LIMIT):
    return pltpu.CompilerParams(dimension_semantics=sem, vmem_limit_bytes=vmem)


def _piece_transpose(blocks):
    v = list(blocks)
    n = len(v)
    assert n * S5_GROUP == LANES
    piece = lax.broadcasted_iota(jnp.int32, v[0].shape, 1) // S5_GROUP
    d = n // 2
    while d >= 1:
        upper = (piece & d) != 0
        for q in range(n):
            if q & d == 0:
                a, b = v[q], v[q + d]
                v[q] = jnp.where(upper, pltpu.roll(b, d * S5_GROUP, 1), a)
                v[q + d] = jnp.where(upper, b, pltpu.roll(a, LANES - d * S5_GROUP, 1))
        d //= 2
    return v


def _in_proj_ab_kernel(x_ref, w_ref, u_ref, up_ref, qkv_ref, xb_ref, us_ref):
    j = pl.program_id(1)

    @pl.when(j == 0)
    def _():
        xb_ref[...] = x_ref[...].astype(BF16)

    acc = jnp.dot(xb_ref[...], w_ref[...], preferred_element_type=F32)

    @pl.when(j == 0)
    def _():
        u_ref[...] = acc
        chunk = S5_CHUNK
        n_chunks = acc.shape[0] // chunk
        per = LANES // S5_GROUP
        for g1 in range(S5_WIDTH // LANES):
            us_ref[g1] = acc[:, g1 * LANES:(g1 + 1) * LANES]
        for s1 in range(chunk // per):
            for g1 in range(S5_WIDTH // LANES):
                rows = [us_ref[g1, pl.ds(s1 * per + s2, n_chunks, stride=chunk), :] for s2 in range(per)]
                regrouped = _piece_transpose(rows)
                for g2 in range(per):
                    up_ref[g1 * per + g2, :, s1 * LANES:(s1 + 1) * LANES] = regrouped[g2].astype(up_ref.dtype)

    @pl.when(j > 0)
    def _():
        qkv_ref[...] = acc.astype(qkv_ref.dtype)


def _in_proj_ab(x, w, tm):
    m, k = x.shape
    tn = S5_WIDTH
    n_blocks = w.shape[1] // tn
    return pl.pallas_call(
        _in_proj_ab_kernel,
        grid=(m // tm, n_blocks),
        in_specs=[pl.BlockSpec((tm, k), lambda i, j: (i, 0)),
                  pl.BlockSpec((k, tn), lambda i, j: (0, j))],
        out_specs=[pl.BlockSpec((tm, tn), lambda i, j: (i, 0)),
                   pl.BlockSpec((S5_GROUPS, tm // S5_CHUNK, tn), lambda i, j: (0, i, 0)),
                   pl.BlockSpec((tm, tn), lambda i, j: (i, jnp.maximum(j - 1, 0)))],
        out_shape=[jax.ShapeDtypeStruct((m, tn), F32),
                   jax.ShapeDtypeStruct((S5_GROUPS, m // S5_CHUNK, tn), BF16),
                   jax.ShapeDtypeStruct((m, w.shape[1] - tn), BF16)],
        scratch_shapes=[pltpu.VMEM((tm, k), BF16), pltpu.VMEM((tn // LANES, tm, LANES), F32)],
        compiler_params=_params(("parallel", "arbitrary")),
        name="in_proj_ab",
    )(x, w)


def _bmm_f32_kernel(x_ref, y_ref, o_ref):
    o_ref[...] = jnp.dot(x_ref[...], y_ref[...], preferred_element_type=F32,
                         precision=lax.Precision.HIGHEST)


def _bmm_f32(x, y):
    g, m, k = x.shape
    n = y.shape[2]
    return pl.pallas_call(
        _bmm_f32_kernel,
        grid=(g,),
        in_specs=[pl.BlockSpec((None, m, k), lambda i: (i, 0, 0)),
                  pl.BlockSpec((None, k, n), lambda i: (i, 0, 0))],
        out_specs=pl.BlockSpec((None, m, n), lambda i: (i, 0, 0)),
        out_shape=jax.ShapeDtypeStruct((g, m, n), F32),
        compiler_params=_params(("parallel",)),
        name="s5_taps",
    )(x, y)


def _s5_kernel(u_ref, r0_ref, ws_ref, wo_ref, a1_ref, a2_ref, o_ref, wt_ref, *, chunks, nsteps):
    h = S5_GROUP
    r0 = r0_ref[...]
    lane = lax.broadcasted_iota(jnp.int32, r0.shape, 1)
    for j in range(r0.shape[1] // h):
        blk = r0 if j == 0 else jnp.where(lane >= j * h, pltpu.roll(r0, j * h, 1), 0.0)
        wt_ref[j * h:(j + 1) * h, :] = blk.astype(BF16)
    u = u_ref[...]
    y_intra = jnp.dot(u, wt_ref[...], preferred_element_type=F32)
    s = jnp.dot(u, ws_ref[...], preferred_element_type=F32)
    m = s.shape[0]
    c_idx = lax.broadcasted_iota(jnp.int32, s.shape, 0) % chunks
    half = s.shape[1] // 2
    for k in range(nsteps):
        d = 1 << k
        sh = jnp.where(c_idx >= d, pltpu.roll(s, d, 0), 0.0)
        s = s + a1_ref[k:k + 1, :] * sh + a2_ref[k:k + 1, :] * pltpu.roll(sh, half, 1)
    s_prev = jnp.where(c_idx >= 1, pltpu.roll(s, 1, 0), 0.0)
    y_inter = jnp.dot(s_prev.astype(BF16), wo_ref[...], preferred_element_type=F32)
    o_ref[...] = y_intra + y_inter


def _s5_operands(lam_re, lam_im, log_dt, b_re, b_im, c_re, c_im, chunk, nsteps):
    g, n = lam_re.shape
    h = S5_GROUP
    dt = jnp.exp(log_dt)[:, None]
    lr, li = lam_re, lam_im
    mag = jnp.exp(lr * dt)
    lbr = mag * jnp.cos(li * dt)
    lbi = mag * jnp.sin(li * dt)
    den = lr * lr + li * li
    coef_re = ((lbr - 1.0) * lr + lbi * li) / den
    coef_im = (lbi * lr - (lbr - 1.0) * li) / den
    bbar_re = coef_re[..., None] * b_re - coef_im[..., None] * b_im
    bbar_im = coef_re[..., None] * b_im + coef_im[..., None] * b_re

    def power(mult):
        mult = mult.astype(F32)[None, :, None]
        pm = jnp.exp(lr[:, None, :] * dt[:, None, :] * mult)
        ang = li[:, None, :] * dt[:, None, :] * mult
        return pm * jnp.cos(ang), pm * jnp.sin(ang)

    pr, pi = power(jnp.arange(chunk + 1))
    cre = jnp.swapaxes(c_re, 1, 2)
    cim = jnp.swapaxes(c_im, 1, 2)
    cb_re = cre[:, :, :, None] * bbar_re[:, :, None, :] - cim[:, :, :, None] * bbar_im[:, :, None, :]
    cb_im = cre[:, :, :, None] * bbar_im[:, :, None, :] + cim[:, :, :, None] * bbar_re[:, :, None, :]
    y_op = jnp.concatenate([cb_re, cb_im], axis=1).reshape(g, 2 * n, h * h)
    x_op = jnp.concatenate([pr[:, :chunk], -pi[:, :chunk]], axis=-1)
    taps = _bmm_f32(x_op, y_op).reshape(g, chunk, h, h)
    r0 = taps.transpose(0, 3, 1, 2).reshape(g, h, chunk * h)
    prr = pr[:, chunk - 1::-1][:, :chunk]
    pir = pi[:, chunk - 1::-1][:, :chunk]
    ws_re = prr[:, :, None, :] * jnp.swapaxes(bbar_re, 1, 2)[:, None] - pir[:, :, None, :] * jnp.swapaxes(bbar_im, 1, 2)[:, None]
    ws_im = prr[:, :, None, :] * jnp.swapaxes(bbar_im, 1, 2)[:, None] + pir[:, :, None, :] * jnp.swapaxes(bbar_re, 1, 2)[:, None]
    ws = jnp.concatenate([ws_re, ws_im], axis=-1).reshape(g, chunk * h, 2 * n)
    p1r = pr[:, 1:chunk + 1]
    p1i = pi[:, 1:chunk + 1]
    wo_re = c_re[:, None] * p1r[:, :, None, :] - c_im[:, None] * p1i[:, :, None, :]
    wo_im = -(c_re[:, None] * p1i[:, :, None, :] + c_im[:, None] * p1r[:, :, None, :])
    wo = jnp.concatenate([wo_re, wo_im], axis=-1)
    wo = wo.transpose(0, 3, 1, 2).reshape(g, 2 * n, chunk * h)
    sr, si = power(chunk * (2 ** jnp.arange(max(nsteps, 1))))
    a1 = jnp.concatenate([sr, sr], axis=-1)
    a2 = jnp.concatenate([-si, si], axis=-1)
    return r0, ws.astype(BF16), wo.astype(BF16), a1, a2


def _s5_scan(up, ops, bsz, seq):
    r0, ws, wo, a1, a2 = ops
    g, h, chunk = S5_GROUPS, S5_GROUP, S5_CHUNK
    chunks = seq // chunk
    m = bsz * chunks
    nsteps = a1.shape[1] if chunks > 1 else 0
    n2 = ws.shape[2]
    ns = a1.shape[1]
    return pl.pallas_call(
        functools.partial(_s5_kernel, chunks=chunks, nsteps=nsteps),
        grid=(g,),
        in_specs=[pl.BlockSpec((None, m, chunk * h), lambda i: (i, 0, 0)),
                  pl.BlockSpec((None, h, chunk * h), lambda i: (i, 0, 0)),
                  pl.BlockSpec((None, chunk * h, n2), lambda i: (i, 0, 0)),
                  pl.BlockSpec((None, n2, chunk * h), lambda i: (i, 0, 0)),
                  pl.BlockSpec((None, ns, n2), lambda i: (i, 0, 0)),
                  pl.BlockSpec((None, ns, n2), lambda i: (i, 0, 0))],
        out_specs=pl.BlockSpec((None, m, chunk * h), lambda i: (i, 0, 0)),
        out_shape=jax.ShapeDtypeStruct((g, m, chunk * h), F32),
        scratch_shapes=[pltpu.VMEM((chunk * h, chunk * h), BF16)],
        compiler_params=_params(("parallel",)),
        name="s5_scan",
    )(up, r0, ws, wo, a1, a2)


def _s5_glu_kernel(yp_ref, u_ref, d_ref, w_ref, b_ref, o_ref, y_ref):
    chunk = S5_CHUNK
    n_chunks = yp_ref.shape[1]
    per = LANES // S5_GROUP
    for s1 in range(chunk // per):
        for g1 in range(S5_WIDTH // LANES):
            by_step = _piece_transpose([yp_ref[g1 * per + g2, :, s1 * LANES:(s1 + 1) * LANES] for g2 in range(per)])
            for s2 in range(per):
                y_ref[g1, pl.ds(s1 * per + s2, n_chunks, stride=chunk), :] = by_step[s2]
    y_ssm = jnp.concatenate([y_ref[g1] for g1 in range(S5_WIDTH // LANES)], axis=1)
    y = y_ssm + d_ref[...] * u_ref[...]
    gl = jax.nn.gelu(y)
    gate = jnp.dot(gl.astype(BF16), w_ref[...], preferred_element_type=F32) + b_ref[...]
    o_ref[...] = (gl * jax.nn.sigmoid(gate)).astype(o_ref.dtype)


def _s5_glu(yp, u, d, w_glu, b_glu, tm):
    t, c = u.shape
    return pl.pallas_call(
        _s5_glu_kernel,
        grid=(t // tm,),
        in_specs=[pl.BlockSpec((S5_GROUPS, tm // S5_CHUNK, c), lambda i: (0, i, 0)),
                  pl.BlockSpec((tm, c), lambda i: (i, 0)),
                  pl.BlockSpec((1, c), lambda i: (0, 0)),
                  pl.BlockSpec((c, c), lambda i: (0, 0)),
                  pl.BlockSpec((1, c), lambda i: (0, 0))],
        out_specs=pl.BlockSpec((tm, c), lambda i: (i, 0)),
        out_shape=jax.ShapeDtypeStruct((t, c), BF16),
        scratch_shapes=[pltpu.VMEM((c // LANES, tm, LANES), F32)],
        compiler_params=_params(("parallel",)),
        name="s5_glu",
    )(yp, u, d.reshape(1, c), w_glu.astype(BF16), b_glu.reshape(1, c))


def _sb_steps(tiles, tri, scale):
    dh = SB_HEAD_DIM
    nk = tiles[0][1].shape[0]
    left = lax.broadcasted_iota(jnp.int32, (nk, 2 * dh), 1) < dh
    pre = []
    stack = []
    for q2, k2, v2, causal, _ in tiles:
        zero = jnp.zeros_like(k2)
        k_bd = jnp.concatenate([jnp.where(left, k2, zero), jnp.where(left, zero, k2)], axis=0)
        z2 = lax.dot_general(q2, k_bd, (((1,), (1,)), ((), ())), preferred_element_type=F32) * scale
        per_head = []
        for hh in range(2):
            z = z2[:, hh * nk:(hh + 1) * nk]
            if causal is not None:
                z = jnp.where(causal, z, ATT_MASKED_SCORE)
            sp = jnp.maximum(z, 0.0) + jnp.log(1.0 + jnp.exp(-jnp.abs(z)))
            log_keep = -sp
            hi = log_keep.astype(BF16)
            lo = (log_keep - hi.astype(F32)).astype(BF16)
            stack += [hi, lo]
            per_head.append((z - sp, log_keep))
        pre.append(per_head)
    tq = tiles[0][0].shape[0]
    sums = jnp.dot(jnp.concatenate(stack, axis=0), tri, preferred_element_type=F32)
    results = []
    for ti, (q2, k2, v2, causal, r_in) in enumerate(tiles):
        ws, rs = [], []
        for hh in range(2):
            log_beta, log_keep = pre[ti][hh]
            base = (ti * 2 + hh) * 2 * tq
            tail = sums[base:base + tq] + sums[base + tq:base + 2 * tq]
            if r_in is not None:
                tail = tail + r_in[hh]
            ws.append(jnp.exp(log_beta + tail).astype(BF16))
            rs.append(jnp.sum(log_keep, axis=1, keepdims=True))
        zero = jnp.zeros_like(v2)
        v_bd = jnp.concatenate([jnp.where(left, v2, zero), jnp.where(left, zero, v2)], axis=0)
        out = jnp.dot(jnp.concatenate(ws, axis=1), v_bd, preferred_element_type=F32)
        results.append((out, rs))
    return results


def _sb_attn_kernel(q_ref, k_ref, v_ref, o_ref, acc_ref, r_ref, tri_ref, *, seq, tq, scale):
    dh = SB_HEAD_DIM
    win = 2 * tq
    row = lax.broadcasted_iota(jnp.int32, (win, win), 0)
    col = lax.broadcasted_iota(jnp.int32, (win, win), 1)
    tri_ref[...] = jnp.where(row > col, 1.0, 0.0).astype(BF16)
    assert ATT_HEADS_PER_STEP == 2

    def q_tiles(p, carry):
        first, where = [], []
        for ti in range(ATT_TILES_PER_ITER):
            i = p * ATT_TILES_PER_ITER + ti
            q0 = pl.multiple_of(i * tq, tq)
            kblk = jnp.maximum(i - 1, 0)
            k0 = pl.multiple_of(kblk * tq, tq)
            q_pos = q0 + lax.broadcasted_iota(jnp.int32, (tq, win), 0)
            causal = (k0 + lax.broadcasted_iota(jnp.int32, (tq, win), 1)) < q_pos
            first.append((q_ref[pl.ds(q0, tq), :], k_ref[pl.ds(k0, win), :], v_ref[pl.ds(k0, win), :], causal, None))
            where.append((q0, kblk))
        rmax = []
        for ti, (out, rs) in enumerate(_sb_steps(first, tri_ref[...], scale)):
            acc_ref[ti] = out
            r_ref[2 * ti] = rs[0]
            r_ref[2 * ti + 1] = rs[1]
            rmax.append(jnp.maximum(jnp.max(rs[0]), jnp.max(rs[1])))
        for ti, (q0, kblk) in enumerate(where):
            def cond(c):
                j, rm = c
                return jnp.logical_and(j >= 0, rm > ATT_LOG_UNDERFLOW)

            def body(c, ti=ti, q0=q0):
                j, _ = c
                kj = pl.multiple_of(j * tq, tq)
                r_old = [r_ref[2 * ti], r_ref[2 * ti + 1]]
                tile = (q_ref[pl.ds(q0, tq), :], k_ref[pl.ds(kj, tq), :], v_ref[pl.ds(kj, tq), :], None, r_old)
                (out, rs), = _sb_steps([tile], tri_ref[:tq, :tq], scale)
                acc_ref[ti] += out
                r_new = [r_old[0] + rs[0], r_old[1] + rs[1]]
                r_ref[2 * ti] = r_new[0]
                r_ref[2 * ti + 1] = r_new[1]
                return j - 1, jnp.maximum(jnp.max(r_new[0]), jnp.max(r_new[1]))

            lax.while_loop(cond, body, (kblk - 1, rmax[ti]))
            o_ref[pl.ds(q0, tq), :] = acc_ref[ti].astype(o_ref.dtype)
        return carry

    lax.fori_loop(0, seq // (tq * ATT_TILES_PER_ITER), q_tiles, 0)


def _sb_attention(qkv, bsz, seq):
    tq = ATT_TQ
    assert seq % (tq * ATT_TILES_PER_ITER) == 0 and seq >= 2 * tq
    wcols = ATT_HEADS_PER_STEP * SB_HEAD_DIM
    nhp = SB_WIDTH // wcols
    kern = functools.partial(_sb_attn_kernel, seq=seq, tq=tq, scale=1.0 / math.sqrt(SB_HEAD_DIM))
    return pl.pallas_call(
        kern,
        grid=(bsz, nhp),
        in_specs=[pl.BlockSpec((None, seq, wcols), lambda b, h: (b, 0, h)),
                  pl.BlockSpec((None, seq, wcols), lambda b, h: (b, 0, nhp + h)),
                  pl.BlockSpec((None, seq, wcols), lambda b, h: (b, 0, 2 * nhp + h))],
        out_specs=pl.BlockSpec((None, seq, wcols), lambda b, h: (b, 0, h)),
        out_shape=jax.ShapeDtypeStruct((bsz, seq, SB_WIDTH), BF16),
        scratch_shapes=[pltpu.VMEM((ATT_TILES_PER_ITER, tq, wcols), F32),
                        pltpu.VMEM((ATT_HEADS_PER_STEP * ATT_TILES_PER_ITER, tq, 1), F32),
                        pltpu.VMEM((2 * tq, 2 * tq), BF16)],
        compiler_params=_params(("parallel", "parallel")),
        name="sb_attention",
    )(qkv, qkv, qkv)


def _pool_in_kernel(x_ref, w_ref, o_ref, halo_ref, *, seq, tm):
    i = pl.program_id(0)
    hmax = POOL_WINDOWS[-1]
    acc = jnp.dot(x_ref[...].astype(BF16), w_ref[...], preferred_element_type=F32)
    t0 = (i * tm) % seq

    @pl.when(t0 == 0)
    def _():
        halo_ref[...] = jnp.zeros_like(halo_ref)

    ext = jnp.concatenate([halo_ref[...], acc], axis=0)
    t_idx = t0 + lax.broadcasted_iota(jnp.int32, (tm, POOL_GROUP_WIDTH), 0)
    for gi, win in enumerate(POOL_WINDOWS):
        cs = slice(gi * POOL_GROUP_WIDTH, (gi + 1) * POOL_GROUP_WIDTH)
        e = ext[:, cs]
        span = 1
        while span < win:
            e = e + pltpu.roll(e, span, 0)
            span *= 2
        wsum = e[hmax:, :]
        cnt = jnp.minimum(t_idx + 1, win).astype(F32)
        o_ref[:, cs] = (wsum / cnt - acc[:, cs]).astype(o_ref.dtype)
    halo_ref[...] = acc[tm - hmax:, :]


def _pool_in(x, w, seq, tm):
    t, k = x.shape
    n = w.shape[1]
    return pl.pallas_call(
        functools.partial(_pool_in_kernel, seq=seq, tm=tm),
        grid=(t // tm,),
        in_specs=[pl.BlockSpec((tm, k), lambda i: (i, 0)),
                  pl.BlockSpec((k, n), lambda i: (0, 0))],
        out_specs=pl.BlockSpec((tm, n), lambda i: (i, 0)),
        out_shape=jax.ShapeDtypeStruct((t, n), BF16),
        scratch_shapes=[pltpu.VMEM((POOL_WINDOWS[-1], n), F32)],
        compiler_params=_params(("arbitrary",)),
        name="pool_in",
    )(x, w)


def _layer_norm(x, g, b):
    mu = jnp.mean(x, axis=-1, keepdims=True)
    xc = x - mu
    var = jnp.mean(xc * xc, axis=-1, keepdims=True)
    return xc * lax.rsqrt(var + LN_EPS) * g + b


def _route(h, rwh_ref, rb_ref, base_ref, tm):
    h_hi = h.astype(BF16)
    h_lo = (h - h_hi.astype(F32)).astype(BF16)
    both = jnp.dot(h_hi, rwh_ref[...], preferred_element_type=F32)
    logits = (both[:, :LANES] + both[:, LANES:]
              + jnp.dot(h_lo, rwh_ref[:, :LANES], preferred_element_type=F32)) + rb_ref[...]
    lt = logits.T[:N_EXPERTS, :]
    mx = jnp.max(lt, axis=0, keepdims=True)
    ex = jnp.exp(lt - mx)
    probs = ex / jnp.sum(ex, axis=0, keepdims=True)
    p = [probs[e:e + 1, :] for e in range(N_EXPERTS)]

    def first_max(vals):
        best, idx = vals[0], jnp.zeros_like(vals[0], dtype=jnp.int32)
        for n in range(1, len(vals)):
            upd = vals[n] > best
            idx = jnp.where(upd, n, idx)
            best = jnp.where(upd, vals[n], best)
        return best, idx

    scores = []
    for gi in range(N_EXPERT_GROUPS):
        p0, p1, p2, p3 = p[4 * gi:4 * gi + 4]
        a, b = jnp.maximum(p0, p1), jnp.minimum(p0, p1)
        c, d = jnp.maximum(p2, p3), jnp.minimum(p2, p3)
        scores.append(jnp.maximum(a, c) + jnp.maximum(jnp.minimum(a, c), jnp.maximum(b, d)))
    _, g_sel = first_max(scores)
    sel = []
    for n in range(EXPERTS_PER_GROUP):
        v = p[n]
        for gi in range(1, N_EXPERT_GROUPS):
            v = jnp.where(g_sel == gi, p[4 * gi + n], v)
        sel.append(v)
    v1, i1 = first_max(sel)
    v2, i2 = first_max([jnp.where(i1 == n, -1.0, sel[n]) for n in range(EXPERTS_PER_GROUP)])
    den = v1 + v2
    e1 = g_sel * EXPERTS_PER_GROUP + i1
    e2 = g_sel * EXPERTS_PER_GROUP + i2
    e_iota = lax.broadcasted_iota(jnp.int32, (N_EXPERTS, tm), 0)
    oh1 = e_iota == e1
    oh2 = e_iota == e2
    oh = jnp.where(jnp.logical_or(oh1, oh2), 1.0, 0.0)
    r_i = lax.broadcasted_iota(jnp.int32, (tm, tm), 0)
    c_i = lax.broadcasted_iota(jnp.int32, (tm, tm), 1)
    before = jnp.where(r_i < c_i, 1.0, 0.0).astype(BF16)
    cnt = jnp.dot(oh.astype(BF16), before, preferred_element_type=F32) + base_ref[...]
    rank1 = jnp.sum(jnp.where(oh1, cnt, 0.0), axis=0, keepdims=True)
    rank2 = jnp.sum(jnp.where(oh2, cnt, 0.0), axis=0, keepdims=True)
    base_ref[...] += jnp.sum(oh, axis=1, keepdims=True)
    e_out = jnp.concatenate([e1, e2], axis=0)
    w_out = jnp.concatenate([v1 / den, v2 / den], axis=0)
    r_out = jnp.concatenate([rank1, rank2], axis=0).astype(jnp.int32)
    return e_out, w_out, r_out


def _proj_ln_kernel(*refs, n_x, grouped, tm, n_tiles):
    xs = refs[:n_x]
    ws = refs[n_x:2 * n_x]
    pos = 2 * n_x
    if grouped:
        wg_ref, sc_ref = refs[pos:pos + 2]
        pos += 2
    res_ref, g_ref, b_ref, rwh_ref, rb_ref = refs[pos:pos + 5]
    h_ref, e_ref, gw_ref, rk_ref, cnt_ref, base_ref, mix_a, mix_b = refs[pos + 5:pos + 13]
    i = pl.program_id(0)
    mix_bufs = (mix_a, mix_b)
    sm = tm // PROJ_SLABS

    @pl.when(i == 0)
    def _():
        base_ref[...] = jnp.zeros_like(base_ref)

    def project(slot):
        for sl in range(PROJ_SLABS):
            rs = slice(sl * sm, (sl + 1) * sm)
            if grouped:
                x = xs[0][rs, :]
                parts = []
                for gi in range(len(POOL_WINDOWS)):
                    cs = slice(gi * POOL_GROUP_WIDTH, (gi + 1) * POOL_GROUP_WIDTH)
                    parts.append(jnp.dot(x[:, cs], wg_ref[gi], preferred_element_type=F32))
                y = (jnp.concatenate(parts, axis=1) * sc_ref[...]).astype(BF16)
                mix = jnp.dot(y, ws[0][...], preferred_element_type=F32)
            else:
                mix = jnp.dot(xs[0][rs, :], ws[0][...], preferred_element_type=F32)
                for n in range(1, n_x):
                    mix = mix + jnp.dot(xs[n][rs, :], ws[n][...], preferred_element_type=F32)
            mix_bufs[slot][rs, :] = mix

    def normalise_and_route(slot):
        for sl in range(PROJ_SLABS):
            rs = slice(sl * sm, (sl + 1) * sm)
            h = _layer_norm(ALPHA * res_ref[rs, :] + mix_bufs[slot][rs, :], g_ref[...], b_ref[...])
            h_ref[rs, :] = h
            e_out, w_out, r_out = _route(h, rwh_ref, rb_ref, base_ref, sm)
            e_ref[:, rs] = e_out
            gw_ref[:, rs] = w_out
            rk_ref[:, rs] = r_out
        cnt_ref[...] = jnp.broadcast_to(base_ref[...], cnt_ref.shape)

    def step(project_slot, finish_slot):
        if project_slot is not None:
            project(project_slot)
        if finish_slot is not None:
            normalise_and_route(finish_slot)

    middle = jnp.logical_and(i >= 1, i < n_tiles)
    pl.when(i == 0)(lambda: step(0, None))
    pl.when(jnp.logical_and(middle, i % 2 == 1))(lambda: step(1, 0))
    pl.when(jnp.logical_and(middle, i % 2 == 0))(lambda: step(0, 1))
    pl.when(i == n_tiles)(lambda: step(None, (n_tiles - 1) % 2))


def _proj_ln_route(xs, ws, res, ln_g, ln_b, router, tm, grouped=None):
    t, d = res.shape
    rwh, rb = router
    n_x = len(xs)
    n_tiles = t // tm
    full = lambda a: pl.BlockSpec(a.shape, lambda i, nd=a.ndim: (0,) * nd, pipeline_mode=pl.Buffered(1))
    cur = lambda i: (jnp.minimum(i, n_tiles - 1), 0)
    prv = lambda i: (jnp.maximum(i - 1, 0), 0)
    prv_t = lambda i: (0, jnp.maximum(i - 1, 0))
    in_specs = [pl.BlockSpec((tm, x.shape[1]), cur) for x in xs]
    in_specs += [full(w) for w in ws]
    args = list(xs) + list(ws)
    if grouped is not None:
        in_specs += [full(a) for a in grouped]
        args += list(grouped)
    in_specs += [pl.BlockSpec((tm, d), prv), full(ln_g), full(ln_b), full(rwh), full(rb)]
    args += [res, ln_g, ln_b, rwh, rb]
    pair = lambda dt: jax.ShapeDtypeStruct((2, t), dt)
    out = pl.pallas_call(
        functools.partial(_proj_ln_kernel, n_x=n_x, grouped=grouped is not None, tm=tm, n_tiles=n_tiles),
        grid=(n_tiles + 1,),
        in_specs=in_specs,
        out_specs=[pl.BlockSpec((tm, d), prv),
                   pl.BlockSpec((2, tm), prv_t),
                   pl.BlockSpec((2, tm), prv_t),
                   pl.BlockSpec((2, tm), prv_t),
                   pl.BlockSpec((N_EXPERTS, LANES), lambda i: (0, 0))],
        out_shape=[jax.ShapeDtypeStruct((t, d), F32), pair(jnp.int32), pair(F32), pair(jnp.int32),
                   jax.ShapeDtypeStruct((N_EXPERTS, LANES), F32)],
        scratch_shapes=[pltpu.VMEM((N_EXPERTS, 1), F32), pltpu.VMEM((tm, d), F32), pltpu.VMEM((tm, d), F32)],
        compiler_params=_params(("arbitrary",)),
        name="proj_ln_route",
    )(*args)
    h, e, gw, rk, cnt = out
    return h, e, gw, rk, cnt[:, 0]


def _dispatch_kernel(p1_ref, p2_ref, fill_ref, h_ref, xs_ref, zero_ref, sem, *, tm, n_fill):
    base = pl.program_id(0) * tm

    @pl.when(pl.program_id(0) == 0)
    def _():
        zero_ref[...] = jnp.zeros_like(zero_ref)
        n_zero = zero_ref.shape[0]
        fill = lambda k: pltpu.make_async_copy(
            zero_ref, xs_ref.at[pl.ds(pl.multiple_of(jnp.maximum(fill_ref[k], 0), n_zero), n_zero)], sem.at[2])
        for k in range(n_fill):
            pl.when(fill_ref[k] >= 0)(lambda k=k: fill(k).start())
        for k in range(n_fill):
            pl.when(fill_ref[k] >= 0)(lambda k=k: fill(k).wait())

    def issue(r, carry):
        src = h_ref.at[pl.ds(r, 1)]
        pltpu.make_async_copy(src, xs_ref.at[pl.ds(p1_ref[base + r], 1)], sem.at[0]).start()
        pltpu.make_async_copy(src, xs_ref.at[pl.ds(p2_ref[base + r], 1)], sem.at[1]).start(
            priority=SECOND_STREAM_PRIORITY)
        return carry

    lax.fori_loop(0, tm, issue, 0, unroll=DMA_ISSUE_UNROLL)
    pltpu.make_async_copy(h_ref, xs_ref.at[pl.ds(0, tm)], sem.at[0]).wait()
    pltpu.make_async_copy(h_ref, xs_ref.at[pl.ds(0, tm)], sem.at[1]).wait()


def _dispatch(h, pos1, pos2, fill_rows, n_rows, tm, tm_fill):
    t, d = h.shape
    return pl.pallas_call(
        functools.partial(_dispatch_kernel, tm=tm, n_fill=fill_rows.shape[0]),
        grid_spec=pltpu.PrefetchScalarGridSpec(
            num_scalar_prefetch=3,
            grid=(t // tm,),
            in_specs=[pl.BlockSpec((tm, d), lambda i, p1, p2, fr: (i, 0))],
            out_specs=pl.BlockSpec(memory_space=pl.ANY),
            scratch_shapes=[pltpu.VMEM((tm_fill, d), h.dtype), pltpu.SemaphoreType.DMA((3,))],
        ),
        out_shape=jax.ShapeDtypeStruct((n_rows, d), h.dtype),
        compiler_params=_params(("arbitrary",)),
        name="moe_dispatch",
    )(pos1, pos2, fill_rows, h)


def _moe_kernel(te_ref, slot_ref, nxt_ref, lo_ref, hi_ref, na_ref, x_ref, wg_hbm, wu_hbm, wd_hbm, o_ref,
                wg_buf, wu_buf, wd_buf, st_g, st_u, st_d, sem, *, layer):
    i = pl.program_id(0)
    rg = wg_buf.shape[1] // MOE_W_GROUPS
    rd = wd_buf.shape[1] // MOE_W_GROUPS

    def copies(e, c, s):
        r0g = pl.multiple_of(c * rg, rg)
        r0d = pl.multiple_of(c * rd, rd)
        return (pltpu.make_async_copy(wg_hbm.at[layer, e, pl.ds(r0g, rg)], st_g.at[s], sem.at[s, 0]),
                pltpu.make_async_copy(wu_hbm.at[layer, e, pl.ds(r0g, rg)], st_u.at[s], sem.at[s, 1]),
                pltpu.make_async_copy(wd_hbm.at[layer, e, pl.ds(r0d, rd)], st_d.at[s], sem.at[s, 2]))

    def start(e, c):
        for cp in copies(e, c, c % 2):
            cp.start()

    def finish(e, c, w_slot):
        s = c % 2
        for cp in copies(e, c, s):
            cp.wait()
        wg_buf[w_slot, pl.ds(pl.multiple_of(c * rg, rg), rg), :] = st_g[s].astype(BF16)
        wu_buf[w_slot, pl.ds(pl.multiple_of(c * rg, rg), rg), :] = st_u[s].astype(BF16)
        wd_buf[w_slot, pl.ds(pl.multiple_of(c * rd, rd), rd), :] = st_d[s].astype(BF16)

    def fetch(e, lo, hi, w_slot):
        def step(c, carry):
            @pl.when(c + 1 < hi)
            def _():
                start(e, c + 1)
            finish(e, c, w_slot)
            return carry
        lax.fori_loop(lo, hi, step, 0)

    @pl.when(i == 0)
    def _():
        start(te_ref[0], 0)
        fetch(te_ref[0], 0, MOE_W_GROUPS, slot_ref[0])

    @pl.when(i < na_ref[0])
    def _():
        lo, hi, nxt, w_slot = lo_ref[i], hi_ref[i], nxt_ref[i], slot_ref[i]

        def issue_first(_, carry):
            start(nxt, lo)
            return carry
        lax.fori_loop(0, (hi > lo).astype(jnp.int32), issue_first, 0)

        x = x_ref[...].astype(BF16)
        gate = jnp.dot(x, wg_buf[w_slot], preferred_element_type=F32)
        up = jnp.dot(x, wu_buf[w_slot], preferred_element_type=F32)
        hid = (jax.nn.silu(gate) * up).astype(BF16)
        o_ref[...] = jnp.dot(hid, wd_buf[w_slot], preferred_element_type=F32)
        fetch(nxt, lo, hi, 1 - w_slot)

    @pl.when(i >= na_ref[0])
    def _():
        o_ref[...] = jnp.zeros_like(o_ref)


def _moe_experts(xs, tables, n_active, w_gate, w_up, w_down, layer, tm):
    p, d = xs.shape
    f = w_gate.shape[3]
    row_blk = lambda i, te, sl, nx, lo, hi, na: (jnp.minimum(i, na[0] - 1), 0)
    hbm = pl.BlockSpec(memory_space=pl.ANY)
    return pl.pallas_call(
        functools.partial(_moe_kernel, layer=layer),
        grid_spec=pltpu.PrefetchScalarGridSpec(
            num_scalar_prefetch=6,
            grid=(p // tm,),
            in_specs=[pl.BlockSpec((tm, d), row_blk), hbm, hbm, hbm],
            out_specs=pl.BlockSpec((tm, d), lambda i, te, sl, nx, lo, hi, na: (i, 0)),
            scratch_shapes=[pltpu.VMEM((2, d, f), BF16), pltpu.VMEM((2, d, f), BF16), pltpu.VMEM((2, f, d), BF16),
                            pltpu.VMEM((2, d // MOE_W_GROUPS, f), F32), pltpu.VMEM((2, d // MOE_W_GROUPS, f), F32),
                            pltpu.VMEM((2, f // MOE_W_GROUPS, d), F32), pltpu.SemaphoreType.DMA((2, 3))],
        ),
        out_shape=jax.ShapeDtypeStruct((p, d), F32),
        compiler_params=_params(("arbitrary",)),
        name="moe_experts",
    )(*tables, n_active, xs, w_gate, w_up, w_down)


def _combine_ln_kernel(p1_ref, p2_ref, h_ref, g1_ref, g2_ref, lg_ref, lb_ref, ys_ref, o_ref,
                       buf1a, buf1b, buf2a, buf2b, sem, *, tm, n_tiles):
    i = pl.program_id(0)
    rows = COMBINE_ROWS_PER_TRIP
    bufs = ((buf1a, buf2a), (buf1b, buf2b))

    def gather(r0, slot):
        base = i * tm
        b1, b2 = bufs[slot]
        for r in range(rows):
            p1 = p1_ref[base + r0 + r]
            p2 = p2_ref[base + r0 + r]
            pltpu.make_async_copy(ys_ref.at[pl.ds(p1, 1)], b1.at[pl.ds(r0 + r, 1)], sem.at[slot, 0]).start()
            pltpu.make_async_copy(ys_ref.at[pl.ds(p2, 1)], b2.at[pl.ds(r0 + r, 1)], sem.at[slot, 1]).start(
                priority=SECOND_STREAM_PRIORITY)

    def finish(r0, slot):
        b1, b2 = bufs[slot]
        rs = pl.ds(r0, rows)
        ffn = g1_ref[rs, :] * b1[rs, :] + g2_ref[rs, :] * b2[rs, :]
        o_ref[rs, :] = _layer_norm(ALPHA * h_ref[rs, :] + ffn, lg_ref[...], lb_ref[...])

    def step(gather_slot, finish_slot):
        if finish_slot is not None:
            b1, b2 = bufs[finish_slot]
            pltpu.make_async_copy(ys_ref.at[pl.ds(0, tm)], b1, sem.at[finish_slot, 0]).wait()
            pltpu.make_async_copy(ys_ref.at[pl.ds(0, tm)], b2, sem.at[finish_slot, 1]).wait()

        def trip(k, carry):
            r0 = pl.multiple_of(k * rows, rows)
            if gather_slot is not None:
                gather(r0, gather_slot)
            if finish_slot is not None:
                finish(r0, finish_slot)
            return carry
        lax.fori_loop(0, tm // rows, trip, 0)

    middle = jnp.logical_and(i >= 1, i < n_tiles)
    pl.when(i == 0)(lambda: step(0, None))
    pl.when(jnp.logical_and(middle, i % 2 == 1))(lambda: step(1, 0))
    pl.when(jnp.logical_and(middle, i % 2 == 0))(lambda: step(0, 1))
    pl.when(i == n_tiles)(lambda: step(None, (n_tiles - 1) % 2))


def _combine_ln(h, ys, pos1, pos2, g1, g2, ln_g, ln_b, tm):
    t, d = h.shape
    n_tiles = t // tm
    row = lambda i, p1, p2: (jnp.maximum(i - 1, 0), 0)
    const = lambda i, p1, p2: (0, 0)
    return pl.pallas_call(
        functools.partial(_combine_ln_kernel, tm=tm, n_tiles=n_tiles),
        grid_spec=pltpu.PrefetchScalarGridSpec(
            num_scalar_prefetch=2,
            grid=(n_tiles + 1,),
            in_specs=[pl.BlockSpec((tm, d), row),
                      pl.BlockSpec((tm, 1), row),
                      pl.BlockSpec((tm, 1), row),
                      pl.BlockSpec((1, d), const),
                      pl.BlockSpec((1, d), const),
                      pl.BlockSpec(memory_space=pl.ANY)],
            out_specs=pl.BlockSpec((tm, d), row),
            scratch_shapes=[pltpu.VMEM((tm, d), F32)] * 4 + [pltpu.SemaphoreType.DMA((2, 2))],
        ),
        out_shape=jax.ShapeDtypeStruct((t, d), F32),
        compiler_params=_params(("arbitrary",)),
        name="moe_combine_ln",
    )(pos1, pos2, h, g1, g2, ln_g, ln_b, ys)


def _moe_block(h, e, gw, rk, counts, w_gate, w_up, w_down, layer, ln_g, ln_b, tm_tok):
    t, d = h.shape
    tm = MOE_TM
    n_rows = 2 * t + N_EXPERTS * tm
    n_tiles = n_rows // tm
    cnt = counts.astype(jnp.int32)
    padded = ((cnt + tm - 1) // tm) * tm
    ends = jnp.cumsum(padded)
    offs = ends - padded
    e_ids = jnp.arange(N_EXPERTS, dtype=jnp.int32)
    pos = jnp.sum(jnp.where(e[..., None] == e_ids, offs, 0), axis=-1) + rk
    n_active = jnp.maximum(ends[-1] // tm, 1)
    tile_start = jnp.minimum(jnp.arange(n_tiles, dtype=jnp.int32), n_active - 1) * tm
    tile_expert = jnp.minimum(jnp.sum((ends[None, :] <= tile_start[:, None]).astype(jnp.int32), axis=1),
                              N_EXPERTS - 1)
    last_tile = jnp.where(padded > 0, ends - tm, -1)
    tail = ends[-1] + jnp.arange(N_EXPERTS, dtype=jnp.int32) * tm
    fill_rows = jnp.concatenate([last_tile, jnp.where(tail < n_rows, tail, -1)]).astype(jnp.int32)
    xs = _dispatch(h, pos[0], pos[1], fill_rows, n_rows, tm_tok, tm)
    idx = jnp.arange(n_tiles, dtype=jnp.int32)
    is_first = jnp.concatenate([jnp.ones((1,), bool), tile_expert[1:] != tile_expert[:-1]])
    run_id = jnp.cumsum(is_first.astype(jnp.int32)) - 1
    run_start = lax.cummax(jnp.where(is_first, idx, 0))
    same_run = run_id[:, None] == run_id[None, :]
    run_len = jnp.sum((same_run & (idx[None, :] < n_active)).astype(jnp.int32), axis=1)
    nxt_tile = run_start + run_len
    nxt = jnp.sum(jnp.where(idx[None, :] == nxt_tile[:, None], tile_expert[None, :], 0), axis=1)
    has_next = nxt_tile < n_active
    k_in_run = idx - run_start
    lo = jnp.where(has_next, (MOE_W_GROUPS * k_in_run) // jnp.maximum(run_len, 1), 0)
    hi = jnp.where(has_next, (MOE_W_GROUPS * (k_in_run + 1)) // jnp.maximum(run_len, 1), 0)
    tables = [a.astype(jnp.int32) for a in (tile_expert, run_id % 2, jnp.where(has_next, nxt, 0), lo, hi)]
    ys = _moe_experts(xs, tables, n_active.reshape(1).astype(jnp.int32), w_gate, w_up, w_down, layer, tm)
    return _combine_ln(h, ys, pos[0], pos[1], gw[0].reshape(t, 1), gw[1].reshape(t, 1), ln_g, ln_b, tm_tok)


def kernel(x, ab_w_in, ab_lambda_re, ab_lambda_im, ab_log_dt, ab_b_re, ab_b_im, ab_c_re, ab_c_im, ab_d,
           ab_w_glu, ab_b_glu, ab_w_out, c_w_in, c_w_group, c_scale, c_w_out, ln_g, ln_b, router_w,
           router_b, moe_w_gate, moe_w_up, moe_w_down):
    bsz, seq, d = x.shape
    t = bsz * seq
    tm = min(ROW_TILE, t)
    assert t % tm == 0 and seq % tm == 0 and seq % S5_CHUNK == 0
    row = lambda a: a.reshape(1, -1)

    rw = jnp.zeros((d, LANES), F32).at[:, :N_EXPERTS].set(router_w)
    rw_hi = rw.astype(BF16)
    rw_lo = (rw - rw_hi.astype(F32)).astype(BF16)
    rb = jnp.zeros((1, LANES), F32).at[0, :N_EXPERTS].set(router_b)
    router = (jnp.concatenate([rw_hi, rw_lo], axis=1), rb)

    h = x.reshape(t, d)

    wg_b, wu_b, wd_b = moe_w_gate, moe_w_up, moe_w_down
    u, u_chunked, qkv = _in_proj_ab(h, ab_w_in[0].astype(BF16), min(2 * ROW_TILE, t))
    chunks = seq // S5_CHUNK
    nsteps = max(int(math.log2(chunks)), 0)
    assert (1 << nsteps) == chunks
    ops = _s5_operands(ab_lambda_re[0], ab_lambda_im[0], ab_log_dt[0], ab_b_re[0], ab_b_im[0],
                       ab_c_re[0], ab_c_im[0], S5_CHUNK, nsteps)
    y_ssm = _s5_scan(u_chunked, ops, bsz, seq)
    y_a = _s5_glu(y_ssm, u, ab_d[0], ab_w_glu[0], ab_b_glu[0], tm)
    y_b = _sb_attention(qkv.reshape(bsz, seq, 3 * SB_WIDTH), bsz, seq).reshape(t, SB_WIDTH)
    w_out = ab_w_out[0].astype(BF16)
    h, e, gw, rk, cnt = _proj_ln_route([y_a, y_b], [w_out[:S5_WIDTH], w_out[S5_WIDTH:]], h,
                                       row(ln_g[0, 0]), row(ln_b[0, 0]), router, tm)
    h = _moe_block(h, e, gw, rk, cnt, wg_b, wu_b, wd_b, 0, row(ln_g[0, 1]), row(ln_b[0, 1]), tm)

    pooled = _pool_in(h, c_w_in[0].astype(BF16), seq, tm)
    h, e, gw, rk, cnt = _proj_ln_route([pooled], [c_w_out[0].astype(BF16)], h,
                                       row(ln_g[1, 0]), row(ln_b[1, 0]), router, tm,
                                       grouped=(c_w_group[0].astype(BF16), row(c_scale[0])))
    h = _moe_block(h, e, gw, rk, cnt, wg_b, wu_b, wd_b, 1, row(ln_g[1, 1]), row(ln_b[1, 1]), tm)
    return h.reshape(bsz, seq, d)
```

```python
import functools
import math

import jax
import jax.numpy as jnp
from jax import lax
from jax.experimental import pallas as pl
from jax.experimental.pallas import tpu as pltpu

S5_WIDTH = 512
S5_GROUP = 16
S5_GROUPS = 32
S5_STATE = 64
SB_HEAD_DIM = 128
SB_HEADS = 12
SB_WIDTH = SB_HEADS * SB_HEAD_DIM
POOL_WINDOWS = (2, 4, 8, 16)
POOL_GROUP_WIDTH = 512
N_EXPERTS = 16
N_EXPERT_GROUPS = 4
EXPERTS_PER_GROUP = 4
DEPTH = 2
ALPHA = (2.0 * DEPTH) ** 0.25
LN_EPS = 1e-5

LANES = 128
VMEM_LIMIT = 56 * 1024 * 1024

S5_CHUNK = 32
ATT_TQ = 128
ATT_HEADS_PER_STEP = 2
ATT_TILES_PER_ITER = 8
ATT_LOG_UNDERFLOW = -88.0
ATT_MASKED_SCORE = -1e30
MOE_TM = 256
MOE_W_GROUPS = 8
ROW_TILE = 512
PROJ_SLABS = 2
DMA_ISSUE_UNROLL = 8
COMBINE_ROWS_PER_TRIP = 64

BF16 = jnp.bfloat16
F32 = jnp.float32


def _params(sem, vmem=VMEM_LIMIT):
    return pltpu.CompilerParams(dimension_semantics=sem, vmem_limit_bytes=vmem)


def _piece_transpose(blocks):
    v = list(blocks)
    n = len(v)
    assert n * S5_GROUP == LANES
    piece = lax.broadcasted_iota(jnp.int32, v[0].shape, 1) // S5_GROUP
    d = n // 2
    while d >= 1:
        upper = (piece & d) != 0
        for q in range(n):
            if q & d == 0:
                a, b = v[q], v[q + d]
                v[q] = jnp.where(upper, pltpu.roll(b, d * S5_GROUP, 1), a)
                v[q + d] = jnp.where(upper, b, pltpu.roll(a, LANES - d * S5_GROUP, 1))
        d //= 2
    return v


def _in_proj_ab_kernel(x_ref, w_ref, u_ref, up_ref, qkv_ref, xb_ref, us_ref):
    j = pl.program_id(1)

    @pl.when(j == 0)
    def _():
        xb_ref[...] = x_ref[...].astype(BF16)

    acc = jnp.dot(xb_ref[...], w_ref[...], preferred_element_type=F32)

    @pl.when(j == 0)
    def _():
        u_ref[...] = acc
        chunk = S5_CHUNK
        n_chunks = acc.shape[0] // chunk
        per = LANES // S5_GROUP
        for g1 in range(S5_WIDTH // LANES):
            us_ref[g1] = acc[:, g1 * LANES:(g1 + 1) * LANES]
        for s1 in range(chunk // per):
            for g1 in range(S5_WIDTH // LANES):
                rows = [us_ref[g1, pl.ds(s1 * per + s2, n_chunks, stride=chunk), :] for s2 in range(per)]
                regrouped = _piece_transpose(rows)
                for g2 in range(per):
                    up_ref[g1 * per + g2, :, s1 * LANES:(s1 + 1) * LANES] = regrouped[g2].astype(up_ref.dtype)

    @pl.when(j > 0)
    def _():
        qkv_ref[...] = acc.astype(qkv_ref.dtype)


def _in_proj_ab(x, w, tm):
    m, k = x.shape
    tn = S5_WIDTH
    n_blocks = w.shape[1] // tn
    return pl.pallas_call(
        _in_proj_ab_kernel,
        grid=(m // tm, n_blocks),
        in_specs=[pl.BlockSpec((tm, k), lambda i, j: (i, 0)),
                  pl.BlockSpec((k, tn), lambda i, j: (0, j))],
        out_specs=[pl.BlockSpec((tm, tn), lambda i, j: (i, 0)),
                   pl.BlockSpec((S5_GROUPS, tm // S5_CHUNK, tn), lambda i, j: (0, i, 0)),
                   pl.BlockSpec((tm, tn), lambda i, j: (i, jnp.maximum(j - 1, 0)))],
        out_shape=[jax.ShapeDtypeStruct((m, tn), F32),
                   jax.ShapeDtypeStruct((S5_GROUPS, m // S5_CHUNK, tn), BF16),
                   jax.ShapeDtypeStruct((m, w.shape[1] - tn), BF16)],
        scratch_shapes=[pltpu.VMEM((tm, k), BF16), pltpu.VMEM((tn // LANES, tm, LANES), F32)],
        compiler_params=_params(("parallel", "arbitrary")),
        name="in_proj_ab",
    )(x, w)


def _bmm_f32_kernel(x_ref, y_ref, o_ref):
    o_ref[...] = jnp.dot(x_ref[...], y_ref[...], preferred_element_type=F32,
                         precision=lax.Precision.HIGHEST)


def _bmm_f32(x, y):
    g, m, k = x.shape
    n = y.shape[2]
    return pl.pallas_call(
        _bmm_f32_kernel,
        grid=(g,),
        in_specs=[pl.BlockSpec((None, m, k), lambda i: (i, 0, 0)),
                  pl.BlockSpec((None, k, n), lambda i: (i, 0, 0))],
        out_specs=pl.BlockSpec((None, m, n), lambda i: (i, 0, 0)),
        out_shape=jax.ShapeDtypeStruct((g, m, n), F32),
        compiler_params=_params(("parallel",)),
        name="s5_taps",
    )(x, y)


def _s5_kernel(u_ref, r0_ref, ws_ref, wo_ref, a1_ref, a2_ref, o_ref, wt_ref, *, chunks, nsteps):
    h = S5_GROUP
    r0 = r0_ref[...]
    lane = lax.broadcasted_iota(jnp.int32, r0.shape, 1)
    for j in range(r0.shape[1] // h):
        blk = r0 if j == 0 else jnp.where(lane >= j * h, pltpu.roll(r0, j * h, 1), 0.0)
        wt_ref[j * h:(j + 1) * h, :] = blk.astype(BF16)
    u = u_ref[...]
    y_intra = jnp.dot(u, wt_ref[...], preferred_element_type=F32)
    s = jnp.dot(u, ws_ref[...], preferred_element_type=F32)
    m = s.shape[0]
    c_idx = lax.broadcasted_iota(jnp.int32, s.shape, 0) % chunks
    half = s.shape[1] // 2
    for k in range(nsteps):
        d = 1 << k
        sh = jnp.where(c_idx >= d, pltpu.roll(s, d, 0), 0.0)
        s = s + a1_ref[k:k + 1, :] * sh + a2_ref[k:k + 1, :] * pltpu.roll(sh, half, 1)
    s_prev = jnp.where(c_idx >= 1, pltpu.roll(s, 1, 0), 0.0)
    y_inter = jnp.dot(s_prev.astype(BF16), wo_ref[...], preferred_element_type=F32)
    o_ref[...] = y_intra + y_inter


def _s5_operands(lam_re, lam_im, log_dt, b_re, b_im, c_re, c_im, chunk, nsteps):
    g, n = lam_re.shape
    h = S5_GROUP
    dt = jnp.exp(log_dt)[:, None]
    lr, li = lam_re, lam_im
    mag = jnp.exp(lr * dt)
    lbr = mag * jnp.cos(li * dt)
    lbi = mag * jnp.sin(li * dt)
    den = lr * lr + li * li
    coef_re = ((lbr - 1.0) * lr + lbi * li) / den
    coef_im = (lbi * lr - (lbr - 1.0) * li) / den
    bbar_re = coef_re[..., None] * b_re - coef_im[..., None] * b_im
    bbar_im = coef_re[..., None] * b_im + coef_im[..., None] * b_re

    def power(mult):
        mult = mult.astype(F32)[None, :, None]
        pm = jnp.exp(lr[:, None, :] * dt[:, None, :] * mult)
        ang = li[:, None, :] * dt[:, None, :] * mult
        return pm * jnp.cos(ang), pm * jnp.sin(ang)

    pr, pi = power(jnp.arange(chunk + 1))
    cre = jnp.swapaxes(c_re, 1, 2)
    cim = jnp.swapaxes(c_im, 1, 2)
    cb_re = cre[:, :, :, None] * bbar_re[:, :, None, :] - cim[:, :, :, None] * bbar_im[:, :, None, :]
    cb_im = cre[:, :, :, None] * bbar_im[:, :, None, :] + cim[:, :, :, None] * bbar_re[:, :, None, :]
    y_op = jnp.concatenate([cb_re, cb_im], axis=1).reshape(g, 2 * n, h * h)
    x_op = jnp.concatenate([pr[:, :chunk], -pi[:, :chunk]], axis=-1)
    taps = _bmm_f32(x_op, y_op).reshape(g, chunk, h, h)
    r0 = taps.transpose(0, 3, 1, 2).reshape(g, h, chunk * h)
    prr = pr[:, chunk - 1::-1][:, :chunk]
    pir = pi[:, chunk - 1::-1][:, :chunk]
    ws_re = prr[:, :, None, :] * jnp.swapaxes(bbar_re, 1, 2)[:, None] - pir[:, :, None, :] * jnp.swapaxes(bbar_im, 1, 2)[:, None]
    ws_im = prr[:, :, None, :] * jnp.swapaxes(bbar_im, 1, 2)[:, None] + pir[:, :, None, :] * jnp.swapaxes(bbar_re, 1, 2)[:, None]
    ws = jnp.concatenate([ws_re, ws_im], axis=-1).reshape(g, chunk * h, 2 * n)
    p1r = pr[:, 1:chunk + 1]
    p1i = pi[:, 1:chunk + 1]
    wo_re = c_re[:, None] * p1r[:, :, None, :] - c_im[:, None] * p1i[:, :, None, :]
    wo_im = -(c_re[:, None] * p1i[:, :, None, :] + c_im[:, None] * p1r[:, :, None, :])
    wo = jnp.concatenate([wo_re, wo_im], axis=-1)
    wo = wo.transpose(0, 3, 1, 2).reshape(g, 2 * n, chunk * h)
    sr, si = power(chunk * (2 ** jnp.arange(max(nsteps, 1))))
    a1 = jnp.concatenate([sr, sr], axis=-1)
    a2 = jnp.concatenate([-si, si], axis=-1)
    return r0, ws.astype(BF16), wo.astype(BF16), a1, a2


def _s5_scan(up, ops, bsz, seq):
    r0, ws, wo, a1, a2 = ops
    g, h, chunk = S5_GROUPS, S5_GROUP, S5_CHUNK
    chunks = seq // chunk
    m = bsz * chunks
    nsteps = a1.shape[1] if chunks > 1 else 0
    n2 = ws.shape[2]
    ns = a1.shape[1]
    return pl.pallas_call(
        functools.partial(_s5_kernel, chunks=chunks, nsteps=nsteps),
        grid=(g,),
        in_specs=[pl.BlockSpec((None, m, chunk * h), lambda i: (i, 0, 0)),
                  pl.BlockSpec((None, h, chunk * h), lambda i: (i, 0, 0)),
                  pl.BlockSpec((None, chunk * h, n2), lambda i: (i, 0, 0)),
                  pl.BlockSpec((None, n2, chunk * h), lambda i: (i, 0, 0)),
                  pl.BlockSpec((None, ns, n2), lambda i: (i, 0, 0)),
                  pl.BlockSpec((None, ns, n2), lambda i: (i, 0, 0))],
        out_specs=pl.BlockSpec((None, m, chunk * h), lambda i: (i, 0, 0)),
        out_shape=jax.ShapeDtypeStruct((g, m, chunk * h), F32),
        scratch_shapes=[pltpu.VMEM((chunk * h, chunk * h), BF16)],
        compiler_params=_params(("parallel",)),
        name="s5_scan",
    )(up, r0, ws, wo, a1, a2)


def _s5_glu_kernel(yp_ref, u_ref, d_ref, w_ref, b_ref, o_ref, y_ref):
    chunk = S5_CHUNK
    n_chunks = yp_ref.shape[1]
    per = LANES // S5_GROUP
    for s1 in range(chunk // per):
        for g1 in range(S5_WIDTH // LANES):
            by_step = _piece_transpose([yp_ref[g1 * per + g2, :, s1 * LANES:(s1 + 1) * LANES] for g2 in range(per)])
            for s2 in range(per):
                y_ref[g1, pl.ds(s1 * per + s2, n_chunks, stride=chunk), :] = by_step[s2]
    y_ssm = jnp.concatenate([y_ref[g1] for g1 in range(S5_WIDTH // LANES)], axis=1)
    y = y_ssm + d_ref[...] * u_ref[...]
    gl = jax.nn.gelu(y)
    gate = jnp.dot(gl.astype(BF16), w_ref[...], preferred_element_type=F32) + b_ref[...]
    o_ref[...] = (gl * jax.nn.sigmoid(gate)).astype(o_ref.dtype)


def _s5_glu(yp, u, d, w_glu, b_glu, tm):
    t, c = u.shape
    return pl.pallas_call(
        _s5_glu_kernel,
        grid=(t // tm,),
        in_specs=[pl.BlockSpec((S5_GROUPS, tm // S5_CHUNK, c), lambda i: (0, i, 0)),
                  pl.BlockSpec((tm, c), lambda i: (i, 0)),
                  pl.BlockSpec((1, c), lambda i: (0, 0)),
                  pl.BlockSpec((c, c), lambda i: (0, 0)),
                  pl.BlockSpec((1, c), lambda i: (0, 0))],
        out_specs=pl.BlockSpec((tm, c), lambda i: (i, 0)),
        out_shape=jax.ShapeDtypeStruct((t, c), BF16),
        scratch_shapes=[pltpu.VMEM((c // LANES, tm, LANES), F32)],
        compiler_params=_params(("parallel",)),
        name="s5_glu",
    )(yp, u, d.reshape(1, c), w_glu.astype(BF16), b_glu.reshape(1, c))


def _sb_steps(tiles, tri, scale):
    dh = SB_HEAD_DIM
    nk = tiles[0][1].shape[0]
    left = lax.broadcasted_iota(jnp.int32, (nk, 2 * dh), 1) < dh
    pre = []
    stack = []
    for q2, k2, v2, causal, _ in tiles:
        zero = jnp.zeros_like(k2)
        k_bd = jnp.concatenate([jnp.where(left, k2, zero), jnp.where(left, zero, k2)], axis=0)
        z2 = lax.dot_general(q2, k_bd, (((1,), (1,)), ((), ())), preferred_element_type=F32) * scale
        per_head = []
        for hh in range(2):
            z = z2[:, hh * nk:(hh + 1) * nk]
            if causal is not None:
                z = jnp.where(causal, z, ATT_MASKED_SCORE)
            sp = jnp.maximum(z, 0.0) + jnp.log(1.0 + jnp.exp(-jnp.abs(z)))
            log_keep = -sp
            hi = log_keep.astype(BF16)
            lo = (log_keep - hi.astype(F32)).astype(BF16)
            stack += [hi, lo]
            per_head.append((z - sp, log_keep))
        pre.append(per_head)
    tq = tiles[0][0].shape[0]
    sums = jnp.dot(jnp.concatenate(stack, axis=0), tri, preferred_element_type=F32)
    results = []
    for ti, (q2, k2, v2, causal, r_in) in enumerate(tiles):
        ws, rs = [], []
        for hh in range(2):
            log_beta, log_keep = pre[ti][hh]
            base = (ti * 2 + hh) * 2 * tq
            tail = sums[base:base + tq] + sums[base + tq:base + 2 * tq]
            if r_in is not None:
                tail = tail + r_in[hh]
            ws.append(jnp.exp(log_beta + tail).astype(BF16))
            rs.append(jnp.sum(log_keep, axis=1, keepdims=True))
        zero = jnp.zeros_like(v2)
        v_bd = jnp.concatenate([jnp.where(left, v2, zero), jnp.where(left, zero, v2)], axis=0)
        out = jnp.dot(jnp.concatenate(ws, axis=1), v_bd, preferred_element_type=F32)
        results.append((out, rs))
    return results


def _sb_attn_kernel(q_ref, k_ref, v_ref, o_ref, acc_ref, r_ref, tri_ref, *, seq, tq, scale):
    dh = SB_HEAD_DIM
    win = 2 * tq
    row = lax.broadcasted_iota(jnp.int32, (win, win), 0)
    col = lax.broadcasted_iota(jnp.int32, (win, win), 1)
    tri_ref[...] = jnp.where(row > col, 1.0, 0.0).astype(BF16)
    assert ATT_HEADS_PER_STEP == 2

    def q_tiles(p, carry):
        first, where = [], []
        for ti in range(ATT_TILES_PER_ITER):
            i = p * ATT_TILES_PER_ITER + ti
            q0 = pl.multiple_of(i * tq, tq)
            kblk = jnp.maximum(i - 1, 0)
            k0 = pl.multiple_of(kblk * tq, tq)
            q_pos = q0 + lax.broadcasted_iota(jnp.int32, (tq, win), 0)
            causal = (k0 + lax.broadcasted_iota(jnp.int32, (tq, win), 1)) < q_pos
            first.append((q_ref[pl.ds(q0, tq), :], k_ref[pl.ds(k0, win), :], v_ref[pl.ds(k0, win), :], causal, None))
            where.append((q0, kblk))
        rmax = []
        for ti, (out, rs) in enumerate(_sb_steps(first, tri_ref[...], scale)):
            acc_ref[ti] = out
            r_ref[2 * ti] = rs[0]
            r_ref[2 * ti + 1] = rs[1]
            rmax.append(jnp.maximum(jnp.max(rs[0]), jnp.max(rs[1])))
        for ti, (q0, kblk) in enumerate(where):
            def cond(c):
                j, rm = c
                return jnp.logical_and(j >= 0, rm > ATT_LOG_UNDERFLOW)

            def body(c, ti=ti, q0=q0):
                j, _ = c
                kj = pl.multiple_of(j * tq, tq)
                r_old = [r_ref[2 * ti], r_ref[2 * ti + 1]]
                tile = (q_ref[pl.ds(q0, tq), :], k_ref[pl.ds(kj, tq), :], v_ref[pl.ds(kj, tq), :], None, r_old)
                (out, rs), = _sb_steps([tile], tri_ref[:tq, :tq], scale)
                acc_ref[ti] += out
                r_new = [r_old[0] + rs[0], r_old[1] + rs[1]]
                r_ref[2 * ti] = r_new[0]
                r_ref[2 * ti + 1] = r_new[1]
                return j - 1, jnp.maximum(jnp.max(r_new[0]), jnp.max(r_new[1]))

            lax.while_loop(cond, body, (kblk - 1, rmax[ti]))
            o_ref[pl.ds(q0, tq), :] = acc_ref[ti].astype(o_ref.dtype)
        return carry

    lax.fori_loop(0, seq // (tq * ATT_TILES_PER_ITER), q_tiles, 0)


def _sb_attention(qkv, bsz, seq):
    tq = ATT_TQ
    assert seq % (tq * ATT_TILES_PER_ITER) == 0 and seq >= 2 * tq
    wcols = ATT_HEADS_PER_STEP * SB_HEAD_DIM
    nhp = SB_WIDTH // wcols
    kern = functools.partial(_sb_attn_kernel, seq=seq, tq=tq, scale=1.0 / math.sqrt(SB_HEAD_DIM))
    return pl.pallas_call(
        kern,
        grid=(bsz, nhp),
        in_specs=[pl.BlockSpec((None, seq, wcols), lambda b, h: (b, 0, h)),
                  pl.BlockSpec((None, seq, wcols), lambda b, h: (b, 0, nhp + h)),
                  pl.BlockSpec((None, seq, wcols), lambda b, h: (b, 0, 2 * nhp + h))],
        out_specs=pl.BlockSpec((None, seq, wcols), lambda b, h: (b, 0, h)),
        out_shape=jax.ShapeDtypeStruct((bsz, seq, SB_WIDTH), BF16),
        scratch_shapes=[pltpu.VMEM((ATT_TILES_PER_ITER, tq, wcols), F32),
                        pltpu.VMEM((ATT_HEADS_PER_STEP * ATT_TILES_PER_ITER, tq, 1), F32),
                        pltpu.VMEM((2 * tq, 2 * tq), BF16)],
        compiler_params=_params(("parallel", "parallel")),
        name="sb_attention",
    )(qkv, qkv, qkv)


def _pool_in_kernel(x_ref, w_ref, o_ref, halo_ref, *, seq, tm):
    i = pl.program_id(0)
    hmax = POOL_WINDOWS[-1]
    acc = jnp.dot(x_ref[...].astype(BF16), w_ref[...], preferred_element_type=F32)
    t0 = (i * tm) % seq

    @pl.when(t0 == 0)
    def _():
        halo_ref[...] = jnp.zeros_like(halo_ref)

    ext = jnp.concatenate([halo_ref[...], acc], axis=0)
    t_idx = t0 + lax.broadcasted_iota(jnp.int32, (tm, POOL_GROUP_WIDTH), 0)
    for gi, win in enumerate(POOL_WINDOWS):
        cs = slice(gi * POOL_GROUP_WIDTH, (gi + 1) * POOL_GROUP_WIDTH)
        e = ext[:, cs]
        span = 1
        while span < win:
            e = e + pltpu.roll(e, span, 0)
            span *= 2
        wsum = e[hmax:, :]
        cnt = jnp.minimum(t_idx + 1, win).astype(F32)
        o_ref[:, cs] = (wsum / cnt - acc[:, cs]).astype(o_ref.dtype)
    halo_ref[...] = acc[tm - hmax:, :]


def _pool_in(x, w, seq, tm):
    t, k = x.shape
    n = w.shape[1]
    return pl.pallas_call(
        functools.partial(_pool_in_kernel, seq=seq, tm=tm),
        grid=(t // tm,),
        in_specs=[pl.BlockSpec((tm, k), lambda i: (i, 0)),
                  pl.BlockSpec((k, n), lambda i: (0, 0))],
        out_specs=pl.BlockSpec((tm, n), lambda i: (i, 0)),
        out_shape=jax.ShapeDtypeStruct((t, n), BF16),
        scratch_shapes=[pltpu.VMEM((POOL_WINDOWS[-1], n), F32)],
        compiler_params=_params(("arbitrary",)),
        name="pool_in",
    )(x, w)


def _layer_norm(x, g, b):
    mu = jnp.mean(x, axis=-1, keepdims=True)
    xc = x - mu
    var = jnp.mean(xc * xc, axis=-1, keepdims=True)
    return xc * lax.rsqrt(var + LN_EPS) * g + b


def _route(h, rwh_ref, rb_ref, base_ref, tm):
    h_hi = h.astype(BF16)
    h_lo = (h - h_hi.astype(F32)).astype(BF16)
    both = jnp.dot(h_hi, rwh_ref[...], preferred_element_type=F32)
    logits = (both[:, :LANES] + both[:, LANES:]
              + jnp.dot(h_lo, rwh_ref[:, :LANES], preferred_element_type=F32)) + rb_ref[...]
    lt = logits.T[:N_EXPERTS, :]
    mx = jnp.max(lt, axis=0, keepdims=True)
    ex = jnp.exp(lt - mx)
    probs = ex / jnp.sum(ex, axis=0, keepdims=True)
    p = [probs[e:e + 1, :] for e in range(N_EXPERTS)]

    def first_max(vals):
        best, idx = vals[0], jnp.zeros_like(vals[0], dtype=jnp.int32)
        for n in range(1, len(vals)):
            upd = vals[n] > best
            idx = jnp.where(upd, n, idx)
            best = jnp.where(upd, vals[n], best)
        return best, idx

    scores = []
    for gi in range(N_EXPERT_GROUPS):
        p0, p1, p2, p3 = p[4 * gi:4 * gi + 4]
        a, b = jnp.maximum(p0, p1), jnp.minimum(p0, p1)
        c, d = jnp.maximum(p2, p3), jnp.minimum(p2, p3)
        scores.append(jnp.maximum(a, c) + jnp.maximum(jnp.minimum(a, c), jnp.maximum(b, d)))
    _, g_sel = first_max(scores)
    sel = []
    for n in range(EXPERTS_PER_GROUP):
        v = p[n]
        for gi in range(1, N_EXPERT_GROUPS):
            v = jnp.where(g_sel == gi, p[4 * gi + n], v)
        sel.append(v)
    v1, i1 = first_max(sel)
    v2, i2 = first_max([jnp.where(i1 == n, -1.0, sel[n]) for n in range(EXPERTS_PER_GROUP)])
    den = v1 + v2
    e1 = g_sel * EXPERTS_PER_GROUP + i1
    e2 = g_sel * EXPERTS_PER_GROUP + i2
    e_iota = lax.broadcasted_iota(jnp.int32, (N_EXPERTS, tm), 0)
    oh1 = e_iota == e1
    oh2 = e_iota == e2
    oh = jnp.where(jnp.logical_or(oh1, oh2), 1.0, 0.0)
    r_i = lax.broadcasted_iota(jnp.int32, (tm, tm), 0)
    c_i = lax.broadcasted_iota(jnp.int32, (tm, tm), 1)
    before = jnp.where(r_i < c_i, 1.0, 0.0).astype(BF16)
    cnt = jnp.dot(oh.astype(BF16), before, preferred_element_type=F32) + base_ref[...]
    rank1 = jnp.sum(jnp.where(oh1, cnt, 0.0), axis=0, keepdims=True)
    rank2 = jnp.sum(jnp.where(oh2, cnt, 0.0), axis=0, keepdims=True)
    base_ref[...] += jnp.sum(oh, axis=1, keepdims=True)
    e_out = jnp.concatenate([e1, e2], axis=0)
    w_out = jnp.concatenate([v1 / den, v2 / den], axis=0)
    r_out = jnp.concatenate([rank1, rank2], axis=0).astype(jnp.int32)
    return e_out, w_out, r_out


def _proj_ln_kernel(*refs, n_x, grouped, tm, n_tiles):
    xs = refs[:n_x]
    ws = refs[n_x:2 * n_x]
    pos = 2 * n_x
    if grouped:
        wg_ref, sc_ref = refs[pos:pos + 2]
        pos += 2
    res_ref, g_ref, b_ref, rwh_ref, rb_ref = refs[pos:pos + 5]
    h_ref, e_ref, gw_ref, rk_ref, cnt_ref, base_ref, mix_a, mix_b = refs[pos + 5:pos + 13]
    i = pl.program_id(0)
    mix_bufs = (mix_a, mix_b)
    sm = tm // PROJ_SLABS

    @pl.when(i == 0)
    def _():
        base_ref[...] = jnp.zeros_like(base_ref)

    def project(slot):
        for sl in range(PROJ_SLABS):
            rs = slice(sl * sm, (sl + 1) * sm)
            if grouped:
                x = xs[0][rs, :]
                parts = []
                for gi in range(len(POOL_WINDOWS)):
                    cs = slice(gi * POOL_GROUP_WIDTH, (gi + 1) * POOL_GROUP_WIDTH)
                    parts.append(jnp.dot(x[:, cs], wg_ref[gi], preferred_element_type=F32))
                y = (jnp.concatenate(parts, axis=1) * sc_ref[...]).astype(BF16)
                mix = jnp.dot(y, ws[0][...], preferred_element_type=F32)
            else:
                mix = jnp.dot(xs[0][rs, :], ws[0][...], preferred_element_type=F32)
                for n in range(1, n_x):
                    mix = mix + jnp.dot(xs[n][rs, :], ws[n][...], preferred_element_type=F32)
            mix_bufs[slot][rs, :] = mix

    def normalise_and_route(slot):
        for sl in range(PROJ_SLABS):
            rs = slice(sl * sm, (sl + 1) * sm)
            h = _layer_norm(ALPHA * res_ref[rs, :] + mix_bufs[slot][rs, :], g_ref[...], b_ref[...])
            h_ref[rs, :] = h
            e_out, w_out, r_out = _route(h, rwh_ref, rb_ref, base_ref, sm)
            e_ref[:, rs] = e_out
            gw_ref[:, rs] = w_out
            rk_ref[:, rs] = r_out
        cnt_ref[...] = jnp.broadcast_to(base_ref[...], cnt_ref.shape)

    def step(project_slot, finish_slot):
        if project_slot is not None:
            project(project_slot)
        if finish_slot is not None:
            normalise_and_route(finish_slot)

    middle = jnp.logical_and(i >= 1, i < n_tiles)
    pl.when(i == 0)(lambda: step(0, None))
    pl.when(jnp.logical_and(middle, i % 2 == 1))(lambda: step(1, 0))
    pl.when(jnp.logical_and(middle, i % 2 == 0))(lambda: step(0, 1))
    pl.when(i == n_tiles)(lambda: step(None, (n_tiles - 1) % 2))


def _proj_ln_route(xs, ws, res, ln_g, ln_b, router, tm, grouped=None):
    t, d = res.shape
    rwh, rb = router
    n_x = len(xs)
    n_tiles = t // tm
    full = lambda a: pl.BlockSpec(a.shape, lambda i, nd=a.ndim: (0,) * nd, pipeline_mode=pl.Buffered(1))
    cur = lambda i: (jnp.minimum(i, n_tiles - 1), 0)
    prv = lambda i: (jnp.maximum(i - 1, 0), 0)
    prv_t = lambda i: (0, jnp.maximum(i - 1, 0))
    in_specs = [pl.BlockSpec((tm, x.shape[1]), cur) for x in xs]
    in_specs += [full(w) for w in ws]
    args = list(xs) + list(ws)
    if grouped is not None:
        in_specs += [full(a) for a in grouped]
        args += list(grouped)
    in_specs += [pl.BlockSpec((tm, d), prv), full(ln_g), full(ln_b), full(rwh), full(rb)]
    args += [res, ln_g, ln_b, rwh, rb]
    pair = lambda dt: jax.ShapeDtypeStruct((2, t), dt)
    out = pl.pallas_call(
        functools.partial(_proj_ln_kernel, n_x=n_x, grouped=grouped is not None, tm=tm, n_tiles=n_tiles),
        grid=(n_tiles + 1,),
        in_specs=in_specs,
        out_specs=[pl.BlockSpec((tm, d), prv),
                   pl.BlockSpec((2, tm), prv_t),
                   pl.BlockSpec((2, tm), prv_t),
                   pl.BlockSpec((2, tm), prv_t),
                   pl.BlockSpec((N_EXPERTS, LANES), lambda i: (0, 0))],
        out_shape=[jax.ShapeDtypeStruct((t, d), F32), pair(jnp.int32), pair(F32), pair(jnp.int32),
                   jax.ShapeDtypeStruct((N_EXPERTS, LANES), F32)],
        scratch_shapes=[pltpu.VMEM((N_EXPERTS, 1), F32), pltpu.VMEM((tm, d), F32), pltpu.VMEM((tm, d), F32)],
        compiler_params=_params(("arbitrary",)),
        name="proj_ln_route",
    )(*args)
    h, e, gw, rk, cnt = out
    return h, e, gw, rk, cnt[:, 0]


def _dispatch_kernel(p1_ref, p2_ref, fill_ref, h_ref, xs_ref, zero_ref, sem, *, tm, n_fill):
    base = pl.program_id(0) * tm

    @pl.when(pl.program_id(0) == 0)
    def _():
        zero_ref[...] = jnp.zeros_like(zero_ref)
        n_zero = zero_ref.shape[0]
        fill = lambda k: pltpu.make_async_copy(
            zero_ref, xs_ref.at[pl.ds(pl.multiple_of(jnp.maximum(fill_ref[k], 0), n_zero), n_zero)], sem.at[2])
        for k in range(n_fill):
            pl.when(fill_ref[k] >= 0)(lambda k=k: fill(k).start())
        for k in range(n_fill):
            pl.when(fill_ref[k] >= 0)(lambda k=k: fill(k).wait())

    def issue(r, carry):
        src = h_ref.at[pl.ds(base + r, 1)]
        pltpu.make_async_copy(src, xs_ref.at[pl.ds(p1_ref[base + r], 1)], sem.at[0]).start()
        pltpu.make_async_copy(src, xs_ref.at[pl.ds(p2_ref[base + r], 1)], sem.at[1]).start()
        return carry

    lax.fori_loop(0, tm, issue, 0, unroll=DMA_ISSUE_UNROLL)
    pltpu.make_async_copy(h_ref.at[pl.ds(0, tm)], xs_ref.at[pl.ds(0, tm)], sem.at[0]).wait()
    pltpu.make_async_copy(h_ref.at[pl.ds(0, tm)], xs_ref.at[pl.ds(0, tm)], sem.at[1]).wait()


def _dispatch(h, pos1, pos2, fill_rows, n_rows, tm, tm_fill):
    t, d = h.shape
    return pl.pallas_call(
        functools.partial(_dispatch_kernel, tm=tm, n_fill=fill_rows.shape[0]),
        grid_spec=pltpu.PrefetchScalarGridSpec(
            num_scalar_prefetch=3,
            grid=(t // tm,),
            in_specs=[pl.BlockSpec(memory_space=pl.ANY)],
            out_specs=pl.BlockSpec(memory_space=pl.ANY),
            scratch_shapes=[pltpu.VMEM((tm_fill, d), h.dtype), pltpu.SemaphoreType.DMA((3,))],
        ),
        out_shape=jax.ShapeDtypeStruct((n_rows, d), h.dtype),
        compiler_params=_params(("arbitrary",)),
        name="moe_dispatch",
    )(pos1, pos2, fill_rows, h)


def _moe_kernel(te_ref, slot_ref, nxt_ref, lo_ref, hi_ref, na_ref, x_ref, wg_hbm, wu_hbm, wd_hbm, o_ref,
                wg_buf, wu_buf, wd_buf, st_g, st_u, st_d, sem, *, layer):
    i = pl.program_id(0)
    rg = wg_buf.shape[1] // MOE_W_GROUPS
    rd = wd_buf.shape[1] // MOE_W_GROUPS

    def copies(e, c, s):
        r0g = pl.multiple_of(c * rg, rg)
        r0d = pl.multiple_of(c * rd, rd)
        return (pltpu.make_async_copy(wg_hbm.at[layer, e, pl.ds(r0g, rg)], st_g.at[s], sem.at[s, 0]),
                pltpu.make_async_copy(wu_hbm.at[layer, e, pl.ds(r0g, rg)], st_u.at[s], sem.at[s, 1]),
                pltpu.make_async_copy(wd_hbm.at[layer, e, pl.ds(r0d, rd)], st_d.at[s], sem.at[s, 2]))

    def start(e, c):
        for cp in copies(e, c, c % 2):
            cp.start()

    def finish(e, c, w_slot):
        s = c % 2
        for cp in copies(e, c, s):
            cp.wait()
        wg_buf[w_slot, pl.ds(pl.multiple_of(c * rg, rg), rg), :] = st_g[s].astype(BF16)
        wu_buf[w_slot, pl.ds(pl.multiple_of(c * rg, rg), rg), :] = st_u[s].astype(BF16)
        wd_buf[w_slot, pl.ds(pl.multiple_of(c * rd, rd), rd), :] = st_d[s].astype(BF16)

    def fetch(e, lo, hi, w_slot):
        def step(c, carry):
            @pl.when(c + 1 < hi)
            def _():
                start(e, c + 1)
            finish(e, c, w_slot)
            return carry
        lax.fori_loop(lo, hi, step, 0)

    @pl.when(i == 0)
    def _():
        start(te_ref[0], 0)
        fetch(te_ref[0], 0, MOE_W_GROUPS, slot_ref[0])

    @pl.when(i < na_ref[0])
    def _():
        lo, hi, nxt, w_slot = lo_ref[i], hi_ref[i], nxt_ref[i], slot_ref[i]

        def issue_first(_, carry):
            start(nxt, lo)
            return carry
        lax.fori_loop(0, (hi > lo).astype(jnp.int32), issue_first, 0)

        x = x_ref[...].astype(BF16)
        gate = jnp.dot(x, wg_buf[w_slot], preferred_element_type=F32)
        up = jnp.dot(x, wu_buf[w_slot], preferred_element_type=F32)
        hid = (jax.nn.silu(gate) * up).astype(BF16)
        o_ref[...] = jnp.dot(hid, wd_buf[w_slot], preferred_element_type=F32)
        fetch(nxt, lo, hi, 1 - w_slot)

    @pl.when(i >= na_ref[0])
    def _():
        o_ref[...] = jnp.zeros_like(o_ref)


def _moe_experts(xs, tables, n_active, w_gate, w_up, w_down, layer, tm):
    p, d = xs.shape
    f = w_gate.shape[3]
    row_blk = lambda i, te, sl, nx, lo, hi, na: (jnp.minimum(i, na[0] - 1), 0)
    hbm = pl.BlockSpec(memory_space=pl.ANY)
    return pl.pallas_call(
        functools.partial(_moe_kernel, layer=layer),
        grid_spec=pltpu.PrefetchScalarGridSpec(
            num_scalar_prefetch=6,
            grid=(p // tm,),
            in_specs=[pl.BlockSpec((tm, d), row_blk), hbm, hbm, hbm],
            out_specs=pl.BlockSpec((tm, d), lambda i, te, sl, nx, lo, hi, na: (i, 0)),
            scratch_shapes=[pltpu.VMEM((2, d, f), BF16), pltpu.VMEM((2, d, f), BF16), pltpu.VMEM((2, f, d), BF16),
                            pltpu.VMEM((2, d // MOE_W_GROUPS, f), F32), pltpu.VMEM((2, d // MOE_W_GROUPS, f), F32),
                            pltpu.VMEM((2, f // MOE_W_GROUPS, d), F32), pltpu.SemaphoreType.DMA((2, 3))],
        ),
        out_shape=jax.ShapeDtypeStruct((p, d), F32),
        compiler_params=_params(("arbitrary",)),
        name="moe_experts",
    )(*tables, n_active, xs, w_gate, w_up, w_down)


def _combine_ln_kernel(p1_ref, p2_ref, h_ref, g1_ref, g2_ref, lg_ref, lb_ref, ys_ref, o_ref,
                       buf1a, buf1b, buf2a, buf2b, sem, *, tm, n_tiles):
    i = pl.program_id(0)
    rows = COMBINE_ROWS_PER_TRIP
    bufs = ((buf1a, buf2a), (buf1b, buf2b))

    def gather(r0, slot):
        base = i * tm
        b1, b2 = bufs[slot]
        for r in range(rows):
            p1 = p1_ref[base + r0 + r]
            p2 = p2_ref[base + r0 + r]
            pltpu.make_async_copy(ys_ref.at[pl.ds(p1, 1)], b1.at[pl.ds(r0 + r, 1)], sem.at[slot, 0]).start()
            pltpu.make_async_copy(ys_ref.at[pl.ds(p2, 1)], b2.at[pl.ds(r0 + r, 1)], sem.at[slot, 1]).start()

    def finish(r0, slot):
        b1, b2 = bufs[slot]
        rs = pl.ds(r0, rows)
        ffn = g1_ref[rs, :] * b1[rs, :] + g2_ref[rs, :] * b2[rs, :]
        o_ref[rs, :] = _layer_norm(ALPHA * h_ref[rs, :] + ffn, lg_ref[...], lb_ref[...])

    def step(gather_slot, finish_slot):
        if finish_slot is not None:
            b1, b2 = bufs[finish_slot]
            pltpu.make_async_copy(ys_ref.at[pl.ds(0, tm)], b1, sem.at[finish_slot, 0]).wait()
            pltpu.make_async_copy(ys_ref.at[pl.ds(0, tm)], b2, sem.at[finish_slot, 1]).wait()

        def trip(k, carry):
            r0 = pl.multiple_of(k * rows, rows)
            if gather_slot is not None:
                gather(r0, gather_slot)
            if finish_slot is not None:
                finish(r0, finish_slot)
            return carry
        lax.fori_loop(0, tm // rows, trip, 0)

    middle = jnp.logical_and(i >= 1, i < n_tiles)
    pl.when(i == 0)(lambda: step(0, None))
    pl.when(jnp.logical_and(middle, i % 2 == 1))(lambda: step(1, 0))
    pl.when(jnp.logical_and(middle, i % 2 == 0))(lambda: step(0, 1))
    pl.when(i == n_tiles)(lambda: step(None, (n_tiles - 1) % 2))


def _combine_ln(h, ys, pos1, pos2, g1, g2, ln_g, ln_b, tm):
    t, d = h.shape
    n_tiles = t // tm
    row = lambda i, p1, p2: (jnp.maximum(i - 1, 0), 0)
    const = lambda i, p1, p2: (0, 0)
    return pl.pallas_call(
        functools.partial(_combine_ln_kernel, tm=tm, n_tiles=n_tiles),
        grid_spec=pltpu.PrefetchScalarGridSpec(
            num_scalar_prefetch=2,
            grid=(n_tiles + 1,),
            in_specs=[pl.BlockSpec((tm, d), row),
                      pl.BlockSpec((tm, 1), row),
                      pl.BlockSpec((tm, 1), row),
                      pl.BlockSpec((1, d), const),
                      pl.BlockSpec((1, d), const),
                      pl.BlockSpec(memory_space=pl.ANY)],
            out_specs=pl.BlockSpec((tm, d), row),
            scratch_shapes=[pltpu.VMEM((tm, d), F32)] * 4 + [pltpu.SemaphoreType.DMA((2, 2))],
        ),
        out_shape=jax.ShapeDtypeStruct((t, d), F32),
        compiler_params=_params(("arbitrary",)),
        name="moe_combine_ln",
    )(pos1, pos2, h, g1, g2, ln_g, ln_b, ys)


def _moe_block(h, e, gw, rk, counts, w_gate, w_up, w_down, layer, ln_g, ln_b, tm_tok):
    t, d = h.shape
    tm = MOE_TM
    n_rows = 2 * t + N_EXPERTS * tm
    n_tiles = n_rows // tm
    cnt = counts.astype(jnp.int32)
    padded = ((cnt + tm - 1) // tm) * tm
    ends = jnp.cumsum(padded)
    offs = ends - padded
    e_ids = jnp.arange(N_EXPERTS, dtype=jnp.int32)
    pos = jnp.sum(jnp.where(e[..., None] == e_ids, offs, 0), axis=-1) + rk
    n_active = jnp.maximum(ends[-1] // tm, 1)
    tile_start = jnp.minimum(jnp.arange(n_tiles, dtype=jnp.int32), n_active - 1) * tm
    tile_expert = jnp.minimum(jnp.sum((ends[None, :] <= tile_start[:, None]).astype(jnp.int32), axis=1),
                              N_EXPERTS - 1)
    last_tile = jnp.where(padded > 0, ends - tm, -1)
    tail = ends[-1] + jnp.arange(N_EXPERTS, dtype=jnp.int32) * tm
    fill_rows = jnp.concatenate([last_tile, jnp.where(tail < n_rows, tail, -1)]).astype(jnp.int32)
    xs = _dispatch(h, pos[0], pos[1], fill_rows, n_rows, tm_tok, tm)
    idx = jnp.arange(n_tiles, dtype=jnp.int32)
    is_first = jnp.concatenate([jnp.ones((1,), bool), tile_expert[1:] != tile_expert[:-1]])
    run_id = jnp.cumsum(is_first.astype(jnp.int32)) - 1
    run_start = lax.cummax(jnp.where(is_first, idx, 0))
    same_run = run_id[:, None] == run_id[None, :]
    run_len = jnp.sum((same_run & (idx[None, :] < n_active)).astype(jnp.int32), axis=1)
    nxt_tile = run_start + run_len
    nxt = jnp.sum(jnp.where(idx[None, :] == nxt_tile[:, None], tile_expert[None, :], 0), axis=1)
    has_next = nxt_tile < n_active
    k_in_run = idx - run_start
    lo = jnp.where(has_next, (MOE_W_GROUPS * k_in_run) // jnp.maximum(run_len, 1), 0)
    hi = jnp.where(has_next, (MOE_W_GROUPS * (k_in_run + 1)) // jnp.maximum(run_len, 1), 0)
    tables = [a.astype(jnp.int32) for a in (tile_expert, run_id % 2, jnp.where(has_next, nxt, 0), lo, hi)]
    ys = _moe_experts(xs, tables, n_active.reshape(1).astype(jnp.int32), w_gate, w_up, w_down, layer, tm)
    return _combine_ln(h, ys, pos[0], pos[1], gw[0].reshape(t, 1), gw[1].reshape(t, 1), ln_g, ln_b, tm_tok)


def kernel(x, ab_w_in, ab_lambda_re, ab_lambda_im, ab_log_dt, ab_b_re, ab_b_im, ab_c_re, ab_c_im, ab_d,
           ab_w_glu, ab_b_glu, ab_w_out, c_w_in, c_w_group, c_scale, c_w_out, ln_g, ln_b, router_w,
           router_b, moe_w_gate, moe_w_up, moe_w_down):
    bsz, seq, d = x.shape
    t = bsz * seq
    tm = min(ROW_TILE, t)
    assert t % tm == 0 and seq % tm == 0 and seq % S5_CHUNK == 0
    row = lambda a: a.reshape(1, -1)

    rw = jnp.zeros((d, LANES), F32).at[:, :N_EXPERTS].set(router_w)
    rw_hi = rw.astype(BF16)
    rw_lo = (rw - rw_hi.astype(F32)).astype(BF16)
    rb = jnp.zeros((1, LANES), F32).at[0, :N_EXPERTS].set(router_b)
    router = (jnp.concatenate([rw_hi, rw_lo], axis=1), rb)

    h = x.reshape(t, d)

    wg_b, wu_b, wd_b = moe_w_gate, moe_w_up, moe_w_down
    u, u_chunked, qkv = _in_proj_ab(h, ab_w_in[0].astype(BF16), min(2 * ROW_TILE, t))
    chunks = seq // S5_CHUNK
    nsteps = max(int(math.log2(chunks)), 0)
    assert (1 << nsteps) == chunks
    ops = _s5_operands(ab_lambda_re[0], ab_lambda_im[0], ab_log_dt[0], ab_b_re[0], ab_b_im[0],
                       ab_c_re[0], ab_c_im[0], S5_CHUNK, nsteps)
    y_ssm = _s5_scan(u_chunked, ops, bsz, seq)
    y_a = _s5_glu(y_ssm, u, ab_d[0], ab_w_glu[0], ab_b_glu[0], tm)
    y_b = _sb_attention(qkv.reshape(bsz, seq, 3 * SB_WIDTH), bsz, seq).reshape(t, SB_WIDTH)
    w_out = ab_w_out[0].astype(BF16)
    h, e, gw, rk, cnt = _proj_ln_route([y_a, y_b], [w_out[:S5_WIDTH], w_out[S5_WIDTH:]], h,
                                       row(ln_g[0, 0]), row(ln_b[0, 0]), router, tm)
    h = _moe_block(h, e, gw, rk, cnt, wg_b, wu_b, wd_b, 0, row(ln_g[0, 1]), row(ln_b[0, 1]), tm)

    pooled = _pool_in(h, c_w_in[0].astype(BF16), seq, tm)
    h, e, gw, rk, cnt = _proj_ln_route([pooled], [c_w_out[0].astype(BF16)], h,
                                       row(ln_g[1, 0]), row(ln_b[1, 0]), router, tm,
                                       grouped=(c_w_group[0].astype(BF16), row(c_scale[0])))
    h = _moe_block(h, e, gw, rk, cnt, wg_b, wu_b, wd_b, 1, row(ln_g[1, 1]), row(ln_b[1, 1]), tm)
    return h.reshape(bsz, seq, d)
```
